```python
import math
import jax
import jax.numpy as jnp
from jax import lax
import numpy as np

D_MODEL = 2048
BATCH = 4
SEQ = 4096
DEPTH = 2

CHUNK = 64
Q_BLOCK = 128
EPS = 1e-6
ROPE_BASE = 10000.0
NEG_INF = -1e30
F32 = jnp.float32

A_HEADS = 8
A_DK = 128
A_DV = 128
A_CONV = 4
B_HEADS = 8
B_NOPE = 128
B_ROPE = 64
B_DV = 128
B_Q_LORA = 512
B_KV_LORA = 512
C_HEADS = 8
C_DK = 128
C_DV = 128
D_HEADS = 8
D_DH = 128
FFN_HIDDEN = -(-8 * D_MODEL // (3 * 256)) * 256

AB_SIZES = (A_HEADS * (2 * A_DK + A_DV), A_HEADS * A_DV, A_HEADS, A_HEADS, B_Q_LORA, B_KV_LORA, B_ROPE)
CD_SIZES = (C_HEADS * C_DK, C_HEADS * C_DK, C_HEADS * C_DV, C_HEADS * C_DV,
            D_HEADS * D_DH, D_HEADS * D_DH, D_HEADS * D_DH, D_HEADS)
AB_OUT = A_HEADS * A_DV + B_HEADS * B_DV
CD_OUT = C_HEADS * C_DV + D_HEADS * D_DH

kernel_name = 'hybrid_deltanet_mla_retention_fox_trunk'


def _split(x, sizes):
    return jnp.split(x, np.cumsum(sizes)[:-1].tolist(), axis=-1)


def _rms_norm(x, w):
    xf = x.astype(F32)
    y = xf * lax.rsqrt(jnp.mean(xf * xf, axis=-1, keepdims=True) + EPS)
    return (y * w.astype(F32)).astype(x.dtype)


def _l2_norm(x):
    xf = x.astype(F32)
    return xf * lax.rsqrt(jnp.sum(xf * xf, axis=-1, keepdims=True) + EPS)


def _rope_tables(positions, dim):
    inv_freq = ROPE_BASE ** (-jnp.arange(0, dim, 2, dtype=F32) / dim)
    ang = positions.astype(F32)[..., None] * inv_freq
    return jnp.cos(ang), jnp.sin(ang)


def _apply_rope(x, cos, sin):
    if x.ndim == 4:
        cos, sin = cos[:, :, None, :], sin[:, :, None, :]
    x1, x2 = jnp.split(x.astype(F32), 2, axis=-1)
    return jnp.concatenate([x1 * cos - x2 * sin, x2 * cos + x1 * sin], axis=-1).astype(x.dtype)


def _causal_conv(x, w):
    k_len, ch = w.shape
    return lax.conv_general_dilated(x, w[:, None, :].astype(x.dtype), window_strides=(1,),
                                    padding=[(k_len - 1, 0)],
                                    dimension_numbers=('NWC', 'WIO', 'NWC'),
                                    feature_group_count=ch)


def _to_chunks(t):
    b, s, h, d = t.shape
    return t.reshape(b, s // CHUNK, CHUNK, h, d).transpose(0, 3, 1, 2, 4)


def _from_chunks(t):
    b, h, n, c, d = t.shape
    return t.transpose(0, 2, 3, 1, 4).reshape(b, n * c, h, d)


def _to_query_blocks(t):
    b, s = t.shape[:2]
    return jnp.moveaxis(t.reshape((b, s // Q_BLOCK, Q_BLOCK) + t.shape[2:]), 1, 0)


def _gated_delta_rule(q, k, v, g, beta):
    b, s, h, dk = q.shape
    dv = v.shape[-1]
    qc, kc, vc = _to_chunks(q), _to_chunks(k), _to_chunks(v)
    g_cum = jnp.cumsum(_to_chunks(g[..., None])[..., 0], axis=-1)
    bc = _to_chunks(beta[..., None])[..., 0]
    tri = jnp.tril(jnp.ones((CHUNK, CHUNK), dtype=bool))
    strict = jnp.tril(jnp.ones((CHUNK, CHUNK), dtype=bool), -1)
    decay = jnp.exp(jnp.where(tri, g_cum[..., :, None] - g_cum[..., None, :], -jnp.inf))
    kk = jnp.einsum('bhnid,bhnjd->bhnij', kc, kc)
    lower = jnp.where(strict, bc[..., :, None] * kk * decay, 0.0) + jnp.eye(CHUNK, dtype=F32)
    u = lax.linalg.triangular_solve(lower, vc * bc[..., None], left_side=True, lower=True,
                                    unit_diagonal=True)
    w = lax.linalg.triangular_solve(lower, kc * (bc * jnp.exp(g_cum))[..., None], left_side=True,
                                    lower=True, unit_diagonal=True)
    qk = jnp.where(tri, jnp.einsum('bhnid,bhnjd->bhnij', qc, kc) * decay, 0.0)
    q_dec = qc * jnp.exp(g_cum)[..., None]
    g_last = g_cum[..., -1]
    k_dec = kc * jnp.exp(g_last[..., None] - g_cum)[..., None]

    def step(state, xs):
        u_i, w_i, qk_i, qd_i, kd_i, gl_i = xs
        v_new = u_i - jnp.einsum('bhck,bhkv->bhcv', w_i, state)
        o_i = jnp.einsum('bhck,bhkv->bhcv', qd_i, state) + jnp.einsum('bhij,bhjv->bhiv', qk_i, v_new)
        state = state * jnp.exp(gl_i)[..., None, None] + jnp.einsum('bhck,bhcv->bhkv', kd_i, v_new)
        return state, o_i

    xs = tuple(jnp.moveaxis(t, 2, 0) for t in (u, w, qk, q_dec, k_dec, g_last))
    _, o = lax.scan(step, jnp.zeros((b, h, dk, dv), F32), xs)
    return _from_chunks(jnp.moveaxis(o, 0, 2))


def _retention(q, k, v):
    q, k, v = q.astype(F32), k.astype(F32), v.astype(F32)
    b, s, h, dk = q.shape
    dv = v.shape[-1]
    log_gamma = jnp.log1p(-jnp.exp2(-5.0 - jnp.arange(h, dtype=F32)))
    idx = jnp.arange(CHUNK, dtype=F32)
    rel = idx[:, None] - idx[None, :]
    intra_decay = jnp.where(rel >= 0, jnp.exp(log_gamma[:, None, None] * jnp.maximum(rel, 0.0)), 0.0)
    qc, kc, vc = _to_chunks(q), _to_chunks(k), _to_chunks(v)
    scores = jnp.einsum('bhnid,bhnjd->bhnij', qc, kc) * intra_decay[None, :, None]
    inner = jnp.einsum('bhnij,bhnjv->bhniv', scores, vc)
    xi = jnp.exp(log_gamma[:, None] * (idx + 1.0))
    zeta = jnp.exp(log_gamma[:, None] * (CHUNK - 1.0 - idx))
    gamma_chunk = jnp.exp(log_gamma * CHUNK)

    def step(state, xs):
        q_i, k_i, v_i = xs
        cross = jnp.einsum('bhck,bhkv->bhcv', q_i, state) * xi[None, :, :, None]
        state = state * gamma_chunk[None, :, None, None] + jnp.einsum(
            'bhck,bhcv->bhkv', k_i * zeta[None, :, :, None], v_i)
        return state, cross

    xs = tuple(jnp.moveaxis(t, 2, 0) for t in (qc, kc, vc))
    _, cross = lax.scan(step, jnp.zeros((b, h, dk, dv), F32), xs)
    return _from_chunks(inner + jnp.moveaxis(cross, 0, 2))


def _mla_attention(q_nope, q_pe, k_nope, k_pe, v):
    b, s, h, _ = q_nope.shape
    nb = s // Q_BLOCK
    scale = (B_NOPE + B_ROPE) ** -0.5
    key_chunk = jnp.arange(s) // CHUNK

    def block(args):
        i, qn, qp = args
        sc = jnp.einsum('bqhd,bkhd->bhqk', qn, k_nope) + jnp.einsum('bqhr,bkr->bhqk', qp, k_pe)
        q_chunk = (i * Q_BLOCK + jnp.arange(Q_BLOCK)) // CHUNK
        sc = jnp.where(key_chunk[None, :] <= q_chunk[:, None], sc.astype(F32) * scale, NEG_INF)
        p = jax.nn.softmax(sc, axis=-1).astype(v.dtype)
        return jnp.einsum('bhqk,bkhd->bqhd', p, v)

    o = lax.map(block, (jnp.arange(nb), _to_query_blocks(q_nope), _to_query_blocks(q_pe)))
    return jnp.moveaxis(o, 0, 1).reshape(b, s, h, -1)


def _forgetting_attention(q, k, v, log_f):
    b, s, h, dh = q.shape
    nb = s // Q_BLOCK
    scale = dh ** -0.5
    cum = jnp.cumsum(log_f, axis=1)
    cum_k = jnp.swapaxes(cum, 1, 2)
    key_idx = jnp.arange(s)

    def block(args):
        i, qb, cq = args
        sc = jnp.einsum('bqhd,bkhd->bhqk', qb, k).astype(F32) * scale
        sc = sc + jnp.swapaxes(cq, 1, 2)[..., None] - cum_k[:, :, None, :]
        q_idx = i * Q_BLOCK + jnp.arange(Q_BLOCK)
        sc = jnp.where(key_idx[None, :] <= q_idx[:, None], sc, NEG_INF)
        p = jax.nn.softmax(sc, axis=-1).astype(v.dtype)
        return jnp.einsum('bhqk,bkhd->bqhd', p, v)

    o = lax.map(block, (jnp.arange(nb), _to_query_blocks(q), _to_query_blocks(cum)))
    return jnp.moveaxis(o, 0, 1).reshape(b, s, h, -1)


def _mixer_ab(h, cos_b, sin_b, w_in, conv_w, a_log, dt_bias, out_norm, q_norm, w_uq, kv_norm,
              w_ukv, w_out):
    b, s, _ = h.shape
    qkv, z, a_in, b_in, cq, ckv, kpe = _split(h @ w_in, AB_SIZES)
    qkv = jax.nn.silu(_causal_conv(qkv, conv_w))
    q, k, v = _split(qkv, (A_HEADS * A_DK, A_HEADS * A_DK, A_HEADS * A_DV))
    q = _l2_norm(q.reshape(b, s, A_HEADS, A_DK)) * (A_DK ** -0.5)
    k = _l2_norm(k.reshape(b, s, A_HEADS, A_DK))
    v = v.reshape(b, s, A_HEADS, A_DV).astype(F32)
    beta = jax.nn.sigmoid(b_in.astype(F32))
    g = -jnp.exp(a_log.astype(F32)) * jax.nn.softplus(a_in.astype(F32) + dt_bias.astype(F32))
    o_a = _gated_delta_rule(q, k, v, g, beta)
    o_a = _rms_norm(o_a, out_norm) * jax.nn.silu(z.reshape(b, s, A_HEADS, A_DV).astype(F32))
    o_a = o_a.reshape(b, s, A_HEADS * A_DV).astype(h.dtype)
    q_b = (_rms_norm(cq, q_norm) @ w_uq).reshape(b, s, B_HEADS, B_NOPE + B_ROPE)
    q_nope = q_b[..., :B_NOPE]
    q_pe = _apply_rope(q_b[..., B_NOPE:], cos_b, sin_b)
    kv = (_rms_norm(ckv, kv_norm) @ w_ukv).reshape(b, s, B_HEADS, B_NOPE + B_DV)
    k_nope, v_b = kv[..., :B_NOPE], kv[..., B_NOPE:]
    k_pe = _apply_rope(kpe, cos_b, sin_b)
    o_b = _mla_attention(q_nope, q_pe, k_nope, k_pe, v_b).reshape(b, s, B_HEADS * B_DV)
    return jnp.concatenate([o_a, o_b], axis=-1) @ w_out


def _mixer_cd(h, cos_c, sin_c, w_in, group_norm, forget_bias, w_out):
    b, s, _ = h.shape
    qc, kc, vc, gc, qd, kd, vd, fd = _split(h @ w_in, CD_SIZES)
    qc = _apply_rope(qc.reshape(b, s, C_HEADS, C_DK), cos_c, sin_c)
    kc = _apply_rope(kc.reshape(b, s, C_HEADS, C_DK), cos_c, sin_c) * (C_DK ** -0.5)
    o_c = _retention(qc, kc, vc.reshape(b, s, C_HEADS, C_DV))
    o_c = _rms_norm(o_c, group_norm) * jax.nn.silu(gc.reshape(b, s, C_HEADS, C_DV).astype(F32))
    o_c = o_c.reshape(b, s, C_HEADS * C_DV).astype(h.dtype)
    log_f = jax.nn.log_sigmoid(fd.astype(F32) + forget_bias.astype(F32))
    o_d = _forgetting_attention(qd.reshape(b, s, D_HEADS, D_DH), kd.reshape(b, s, D_HEADS, D_DH),
                                vd.reshape(b, s, D_HEADS, D_DH), log_f)
    o_d = o_d.reshape(b, s, D_HEADS * D_DH)
    return jnp.concatenate([o_c, o_d], axis=-1) @ w_out


def _swiglu(h, w_gate, w_up, w_down):
    return (jax.nn.silu(h @ w_gate) * (h @ w_up)) @ w_down


def setup_inputs(seed: int = 0) -> dict:
    key = jax.random.key(seed)
    ks = iter(jax.random.split(key, 40))
    n_even = (DEPTH + 1) // 2
    n_odd = DEPTH // 2

    def normal(shape, scale):
        return jax.random.normal(next(ks), shape, F32) * scale

    def gain(shape):
        return 1.0 + normal(shape, 0.02)

    x = normal((BATCH, SEQ, D_MODEL), 1.0)
    c = normal((BATCH, D_MODEL), 1.0)
    offset = jax.random.randint(next(ks), (BATCH, 1), 0, 8192, dtype=jnp.int32)
    positions = offset + jnp.arange(SEQ, dtype=jnp.int32)[None, :]
    ada_w = normal((DEPTH, D_MODEL, 6 * D_MODEL), 0.5 * D_MODEL ** -0.5)
    ada_b = normal((DEPTH, 6 * D_MODEL), 0.02)
    mix_pre_norm = gain((DEPTH, D_MODEL))
    mix_post_norm = gain((DEPTH, D_MODEL))
    ffn_pre_norm = gain((DEPTH, D_MODEL))
    ffn_post_norm = gain((DEPTH, D_MODEL))
    ab_w_in = normal((n_even, D_MODEL, sum(AB_SIZES)), D_MODEL ** -0.5)
    ab_conv_w = normal((n_even, A_CONV, A_HEADS * (2 * A_DK + A_DV)), A_CONV ** -0.5)
    ab_a_log = jnp.log(jax.random.uniform(next(ks), (n_even, A_HEADS), F32, 1.0, 16.0))
    dt = jnp.exp(jax.random.uniform(next(ks), (n_even, A_HEADS), F32, math.log(1e-3), math.log(1e-1)))
    ab_dt_bias = dt + jnp.log(-jnp.expm1(-dt))
    ab_out_norm = gain((n_even, A_DV))
    ab_q_norm = gain((n_even, B_Q_LORA))
    ab_w_uq = normal((n_even, B_Q_LORA, B_HEADS * (B_NOPE + B_ROPE)), B_Q_LORA ** -0.5)
    ab_kv_norm = gain((n_even, B_KV_LORA))
    ab_w_ukv = normal((n_even, B_KV_LORA, B_HEADS * (B_NOPE + B_DV)), B_KV_LORA ** -0.5)
    ab_w_out = normal((n_even, AB_OUT, D_MODEL), AB_OUT ** -0.5)
    cd_w_in = normal((n_odd, D_MODEL, sum(CD_SIZES)), D_MODEL ** -0.5)
    cd_group_norm = gain((n_odd, C_HEADS, C_DV))
    cd_forget_bias = 3.0 + normal((n_odd, D_HEADS), 0.1)
    cd_w_out = normal((n_odd, CD_OUT, D_MODEL), CD_OUT ** -0.5)
    ffn_w_gate = normal((DEPTH, D_MODEL, FFN_HIDDEN), D_MODEL ** -0.5)
    ffn_w_up = normal((DEPTH, D_MODEL, FFN_HIDDEN), D_MODEL ** -0.5)
    ffn_w_down = normal((DEPTH, FFN_HIDDEN, D_MODEL), FFN_HIDDEN ** -0.5)
    return {'x': x, 'c': c, 'positions': positions, 'ada_w': ada_w, 'ada_b': ada_b,
            'mix_pre_norm': mix_pre_norm, 'mix_post_norm': mix_post_norm,
            'ffn_pre_norm': ffn_pre_norm, 'ffn_post_norm': ffn_post_norm,
            'ab_w_in': ab_w_in, 'ab_conv_w': ab_conv_w, 'ab_a_log': ab_a_log,
            'ab_dt_bias': ab_dt_bias, 'ab_out_norm': ab_out_norm, 'ab_q_norm': ab_q_norm,
            'ab_w_uq': ab_w_uq, 'ab_kv_norm': ab_kv_norm, 'ab_w_ukv': ab_w_ukv,
            'ab_w_out': ab_w_out, 'cd_w_in': cd_w_in, 'cd_group_norm': cd_group_norm,
            'cd_forget_bias': cd_forget_bias, 'cd_w_out': cd_w_out,
            'ffn_w_gate': ffn_w_gate, 'ffn_w_up': ffn_w_up, 'ffn_w_down': ffn_w_down}


def reference(x, c, positions, ada_w, ada_b, mix_pre_norm, mix_post_norm, ffn_pre_norm,
              ffn_post_norm, ab_w_in, ab_conv_w, ab_a_log, ab_dt_bias, ab_out_norm, ab_q_norm,
              ab_w_uq, ab_kv_norm, ab_w_ukv, ab_w_out, cd_w_in, cd_group_norm, cd_forget_bias,
              cd_w_out, ffn_w_gate, ffn_w_up, ffn_w_down):
    cos_b, sin_b = _rope_tables(positions, B_ROPE)
    cos_c, sin_c = _rope_tables(positions, C_DK)
    cond = jax.nn.silu(c)
    for layer in range(DEPTH):
        mod = cond @ ada_w[layer] + ada_b[layer]
        shift_m, scale_m, gate_m, shift_f, scale_f, gate_f = [m[:, None, :] for m in jnp.split(mod, 6, axis=-1)]
        h = _rms_norm(x, mix_pre_norm[layer]) * (1.0 + scale_m) + shift_m
        j = layer // 2
        if layer % 2 == 0:
            y = _mixer_ab(h, cos_b, sin_b, ab_w_in[j], ab_conv_w[j], ab_a_log[j], ab_dt_bias[j],
                          ab_out_norm[j], ab_q_norm[j], ab_w_uq[j], ab_kv_norm[j], ab_w_ukv[j],
                          ab_w_out[j])
        else:
            y = _mixer_cd(h, cos_c, sin_c, cd_w_in[j], cd_group_norm[j], cd_forget_bias[j],
                          cd_w_out[j])
        x = x + gate_m * _rms_norm(y, mix_post_norm[layer])
        h = _rms_norm(x, ffn_pre_norm[layer]) * (1.0 + scale_f) + shift_f
        y = _swiglu(h, ffn_w_gate[layer], ffn_w_up[layer], ffn_w_down[layer])
        x = x + gate_f * _rms_norm(y, ffn_post_norm[layer])
    return x
```

```python
import functools

import jax
import jax.numpy as jnp
import numpy as np
from jax import lax
from jax.experimental import pallas as pl
from jax.experimental.pallas import tpu as pltpu

F32 = jnp.float32
BF16 = jnp.bfloat16

D_MODEL = 2048
CHUNK = 64
EPS = 1e-6
ROPE_BASE = 10000.0
NEG_INF = -1e30

A_HEADS, A_DK, A_DV, A_CONV = 8, 128, 128, 4
B_HEADS, B_NOPE, B_ROPE, B_DV, B_Q_LORA, B_KV_LORA = 8, 128, 64, 128, 512, 512
C_HEADS, C_DK, C_DV = 8, 128, 128
D_HEADS, D_DH = 8, 128
LANES = 128

VMEM_LIMIT_BYTES = 56 * 1024 * 1024

RET_CHUNK = 128


def _cparams(n_axes):
    return pltpu.CompilerParams(dimension_semantics=("arbitrary",) * n_axes,
                                vmem_limit_bytes=VMEM_LIMIT_BYTES)


def _dot(a, b):
    return jnp.dot(a, b, preferred_element_type=F32)


def _dot_nt(a, b):
    return lax.dot_general(a, b, (((1,), (1,)), ((), ())), preferred_element_type=F32)


def _dot_tn(a, b):
    return lax.dot_general(a, b, (((0,), (0,)), ((), ())), preferred_element_type=F32)


def _split_bf16(a):
    hi = a.astype(BF16)
    lo = (a - hi.astype(F32)).astype(BF16)
    return hi, lo


def _dot3(a, b):
    ah, al = _split_bf16(a)
    bh, bl = _split_bf16(b)
    return _dot(ah, bh) + _dot(ah, bl) + _dot(al, bh)


def _sigmoid(x):
    return 1.0 / (1.0 + jnp.exp(-x))


def _silu(x):
    return x * _sigmoid(x)


def _softplus(x):
    return jnp.maximum(x, 0.0) + jnp.log(1.0 + jnp.exp(-jnp.abs(x)))


def _rms(x, w):
    return x * lax.rsqrt(jnp.mean(x * x, axis=-1, keepdims=True) + EPS) * w


def _ada_kernel(c_ref, w_ref, b_ref, o_ref):
    cond = _silu(c_ref[...]).astype(BF16)
    o_ref[...] = _dot(cond, w_ref[...].astype(BF16)) + b_ref[...]


def _ada(c_pad, ada_w, ada_b):
    depth, d, n = ada_w.shape
    rows = c_pad.shape[0]
    tn = 1024
    return pl.pallas_call(
        _ada_kernel,
        grid=(depth, n // tn),
        in_specs=[pl.BlockSpec((rows, d), lambda l, j: (0, 0)),
                  pl.BlockSpec((None, d, tn), lambda l, j: (l, 0, j)),
                  pl.BlockSpec((None, 1, tn), lambda l, j: (l, 0, j))],
        out_specs=pl.BlockSpec((None, rows, tn), lambda l, j: (l, 0, j)),
        out_shape=jax.ShapeDtypeStruct((depth, rows, n), F32),
        compiler_params=_cparams(2),
        name="ada_mod",
    )(c_pad, ada_w, ada_b.reshape(depth, 1, n))


def _rope_kernel(pos_ref, fb_ref, mb_ref, sb_ref, fc_ref, sc_ref, cb_ref, snb_ref, cc_ref, snc_ref):
    pos = pos_ref[...]
    ang_b = pos * fb_ref[...]
    cb_ref[...] = jnp.cos(ang_b) * mb_ref[...]
    snb_ref[...] = jnp.sin(ang_b) * sb_ref[...]
    ang_c = pos * fc_ref[...]
    cc_ref[...] = jnp.cos(ang_c)
    snc_ref[...] = jnp.sin(ang_c) * sc_ref[...]


def _rope_tables(pos_col):
    t = pos_col.shape[0]
    tb = min(1024, t)
    inv_b = ROPE_BASE ** (-jnp.arange(0, B_ROPE, 2, dtype=F32) / B_ROPE)
    inv_c = ROPE_BASE ** (-jnp.arange(0, C_DK, 2, dtype=F32) / C_DK)
    z32 = jnp.zeros((B_ROPE // 2,), F32)
    o32 = jnp.ones((B_ROPE // 2,), F32)
    o64 = jnp.ones((C_DK // 2,), F32)
    fb = jnp.concatenate([inv_b, z32, inv_b, z32])[None, :]
    mb = jnp.concatenate([o32, z32, o32, z32])[None, :]
    sb = jnp.concatenate([-o32, z32, o32, z32])[None, :]
    fc = jnp.concatenate([inv_c, inv_c])[None, :]
    sc = jnp.concatenate([-o64, o64])[None, :]
    row = pl.BlockSpec((1, LANES), lambda i: (0, 0))
    tab = pl.BlockSpec((tb, LANES), lambda i: (i, 0))
    shp = jax.ShapeDtypeStruct((t, LANES), F32)
    return pl.pallas_call(
        _rope_kernel,
        grid=(t // tb,),
        in_specs=[pl.BlockSpec((tb, 1), lambda i: (i, 0)), row, row, row, row, row],
        out_specs=[tab, tab, tab, tab],
        out_shape=[shp, shp, shp, shp],
        compiler_params=_cparams(1),
        name="rope_tables",
    )(pos_col, fb, mb, sb, fc, sc)


def _prenorm_kernel(x_ref, mod_ref, w_ref, h_ref, *, shift_idx, scale_idx):
    y = _rms(x_ref[...], w_ref[...])
    h = y * (1.0 + mod_ref[scale_idx:scale_idx + 1, :]) + mod_ref[shift_idx:shift_idx + 1, :]
    h_ref[...] = h.astype(h_ref.dtype)


def _prenorm(x2, mod_l, w, seq, shift_idx, scale_idx):
    t, d = x2.shape
    tm = min(512, seq)
    bpb = seq // tm
    return pl.pallas_call(
        functools.partial(_prenorm_kernel, shift_idx=shift_idx, scale_idx=scale_idx),
        grid=(t // tm,),
        in_specs=[pl.BlockSpec((tm, d), lambda i: (i, 0)),
                  pl.BlockSpec((None, 6, d), lambda i: (i // bpb, 0, 0)),
                  pl.BlockSpec((1, d), lambda i: (0, 0))],
        out_specs=pl.BlockSpec((tm, d), lambda i: (i, 0)),
        out_shape=jax.ShapeDtypeStruct((t, d), BF16),
        compiler_params=_cparams(1),
        name="prenorm",
    )(x2, mod_l, w.reshape(1, d))


def _mm_kernel(a_ref, w_ref, o_ref):
    o_ref[...] = _dot(a_ref[...], w_ref[...]).astype(o_ref.dtype)


def _mm(a, w, out_dtype, tm=512, tn=1024):
    m, k = a.shape
    n = w.shape[1]
    tm = min(tm, m)
    tn = min(tn, n)
    return pl.pallas_call(
        _mm_kernel,
        grid=(n // tn, m // tm),
        in_specs=[pl.BlockSpec((tm, k), lambda j, i: (i, 0)),
                  pl.BlockSpec((k, tn), lambda j, i: (0, j))],
        out_specs=pl.BlockSpec((tm, tn), lambda j, i: (i, j)),
        out_shape=jax.ShapeDtypeStruct((m, n), out_dtype),
        compiler_params=_cparams(2),
        name="matmul",
    )(a, w)


def _mm_post_kernel(a_ref, w_ref, x_ref, mod_ref, modn_ref, pw_ref, nw_ref, *rest, nk, gate_idx,
                    nshift_idx, nscale_idx, emit_h):
    if emit_h:
        xo_ref, h_ref, acc_ref = rest
    else:
        xo_ref, acc_ref = rest
    k = pl.program_id(1)

    @pl.when(k == 0)
    def _():
        acc_ref[...] = jnp.zeros_like(acc_ref)

    acc_ref[...] += _dot(a_ref[...], w_ref[...])

    @pl.when(k == nk - 1)
    def _():
        yn = _rms(acc_ref[...], pw_ref[...])
        xn = x_ref[...] + mod_ref[gate_idx:gate_idx + 1, :] * yn
        xo_ref[...] = xn
        if emit_h:
            hn = _rms(xn, nw_ref[...])
            hn = hn * (1.0 + modn_ref[nscale_idx:nscale_idx + 1, :]) + modn_ref[nshift_idx:nshift_idx + 1, :]
            h_ref[...] = hn.astype(h_ref.dtype)


def _mm_post(a, w, x2, mod_l, modn_l, post_w, next_w, seq, gate_idx, nshift_idx, nscale_idx, emit_h,
             tm=512, tk=512):
    m, kdim = a.shape
    d = w.shape[1]
    tm = min(tm, seq)
    tk = min(tk, kdim)
    nk = kdim // tk
    bpb = seq // tm
    row = pl.BlockSpec((1, d), lambda i, k: (0, 0))
    modspec = pl.BlockSpec((None, 6, d), lambda i, k: (i // bpb, 0, 0))
    xspec = pl.BlockSpec((tm, d), lambda i, k: (i, 0))
    out_specs = [xspec]
    out_shape = [jax.ShapeDtypeStruct((m, d), F32)]
    if emit_h:
        out_specs.append(xspec)
        out_shape.append(jax.ShapeDtypeStruct((m, d), BF16))
    res = pl.pallas_call(
        functools.partial(_mm_post_kernel, nk=nk, gate_idx=gate_idx, nshift_idx=nshift_idx,
                          nscale_idx=nscale_idx, emit_h=emit_h),
        grid=(m // tm, nk),
        in_specs=[pl.BlockSpec((tm, tk), lambda i, k: (i, k)),
                  pl.BlockSpec((tk, d), lambda i, k: (k, 0)),
                  xspec, modspec, modspec, row, row],
        out_specs=out_specs,
        out_shape=out_shape,
        scratch_shapes=[pltpu.VMEM((tm, d), F32)],
        compiler_params=_cparams(2),
        name="matmul_post",
    )(a, w, x2, mod_l, modn_l, post_w.reshape(1, d), next_w.reshape(1, d))
    return res if emit_h else (res[0], None)


def _ffn_up_kernel(a_ref, wg_ref, wu_ref, o_ref):
    a = a_ref[...]
    g = _dot(a, wg_ref[...])
    u = _dot(a, wu_ref[...])
    o_ref[...] = (_silu(g) * u).astype(o_ref.dtype)


def _ffn_up(a, wg, wu, tm=512, tn=512):
    m, k = a.shape
    n = wg.shape[1]
    tm = min(tm, m)
    wspec = pl.BlockSpec((k, tn), lambda j, i: (0, j))
    return pl.pallas_call(
        _ffn_up_kernel,
        grid=(n // tn, m // tm),
        in_specs=[pl.BlockSpec((tm, k), lambda j, i: (i, 0)), wspec, wspec],
        out_specs=pl.BlockSpec((tm, tn), lambda j, i: (i, j)),
        out_shape=jax.ShapeDtypeStruct((m, n), BF16),
        compiler_params=_cparams(2),
        name="ffn_up",
    )(a, wg, wu)


def _gdn_gate_kernel(s_ref, alog_ref, dt_ref, o_ref):
    s = s_ref[...]
    g = -jnp.exp(alog_ref[...]) * _softplus(s + dt_ref[...])
    beta = _sigmoid(s)
    lane = lax.broadcasted_iota(jnp.int32, s.shape, 1)
    o_ref[...] = jnp.where(lane < A_HEADS, g, beta)


def _gdn_gates(slab_src, col_block, a_log, dt_bias):
    t = slab_src.shape[0]
    tb = min(1024, t)
    pad = jnp.zeros((LANES - A_HEADS,), F32)
    alog_row = jnp.concatenate([a_log.astype(F32), pad])[None, :]
    dt_row = jnp.concatenate([dt_bias.astype(F32), pad])[None, :]
    row = pl.BlockSpec((1, LANES), lambda i: (0, 0))
    return pl.pallas_call(
        _gdn_gate_kernel,
        grid=(t // tb,),
        in_specs=[pl.BlockSpec((tb, LANES), lambda i: (i, col_block)), row, row],
        out_specs=pl.BlockSpec((tb, LANES), lambda i: (i, 0)),
        out_shape=jax.ShapeDtypeStruct((t, LANES), F32),
        compiler_params=_cparams(1),
        name="gdn_gates",
    )(slab_src, alog_row, dt_row)


GDN_HG = 2
GDN_TB = 256


def _unit_lower_inverse(n_mat):
    c = n_mat.shape[0]
    r = lax.broadcasted_iota(jnp.int32, (c, c), 0)
    col = lax.broadcasted_iota(jnp.int32, (c, c), 1)
    p = jnp.where(r == col, 1.0, 0.0) - n_mat
    m = _dot3(n_mat, n_mat)
    power = 2
    while True:
        p = p + _dot3(p, m)
        power *= 2
        if power >= c:
            break
        m = _dot3(m, m)
    return p


def _gdn_kernel(q_ref, k_ref, v_ref, z_ref, cwq_ref, cwk_ref, cwv_ref, gcol_ref, bcol_ref, grow_ref, onw_ref,
                o_ref, xq_ref, xk_ref, xv_ref, st_ref, *, tb, hg):
    j = pl.program_id(2)
    hist = 8

    for x_ref, src_ref in ((xq_ref, q_ref), (xk_ref, k_ref), (xv_ref, v_ref)):
        @pl.when(j == 0)
        def _():
            x_ref[0:hist, :] = jnp.zeros((hist, x_ref.shape[1]), F32)

        @pl.when(j > 0)
        def _():
            x_ref[0:hist, :] = x_ref[tb:tb + hist, :]

        x_ref[hist:hist + tb, :] = src_ref[...]

    @pl.when(j == 0)
    def _():
        st_ref[...] = jnp.zeros_like(st_ref)

    def conv_silu(x_ref, cw_ref):
        acc = None
        for i in range(A_CONV):
            off = hist - (A_CONV - 1) + i
            term = x_ref[off:off + tb, :] * cw_ref[i:i + 1, :]
            acc = term if acc is None else acc + term
        return _silu(acc)

    q_all = conv_silu(xq_ref, cwq_ref)
    k_all = conv_silu(xk_ref, cwk_ref)
    v_all = conv_silu(xv_ref, cwv_ref)

    c = CHUNK
    r = lax.broadcasted_iota(jnp.int32, (c, c), 0)
    col = lax.broadcasted_iota(jnp.int32, (c, c), 1)
    tri = r >= col
    strict = r > col

    for hh in range(hg):
        lanes = slice(hh * A_DK, (hh + 1) * A_DK)
        qh = q_all[:, lanes]
        kh = k_all[:, lanes]
        vh = v_all[:, lanes]
        qh = qh * lax.rsqrt(jnp.sum(qh * qh, axis=-1, keepdims=True) + EPS) * (A_DK ** -0.5)
        kh = kh * lax.rsqrt(jnp.sum(kh * kh, axis=-1, keepdims=True) + EPS)
        state = st_ref[hh]
        for ci in range(tb // c):
            rows = slice(ci * c, (ci + 1) * c)
            qc, kc, vc = qh[rows], kh[rows], vh[rows]
            g_col = gcol_ref[hh, rows, :]
            b_col = bcol_ref[hh, rows, :]
            g_row = grow_ref[hh][:, rows]
            gc_col = jnp.sum(jnp.where(tri, g_row, 0.0), axis=1, keepdims=True)
            gc_row = jnp.sum(jnp.where(r <= col, g_col, 0.0), axis=0, keepdims=True)
            decay = jnp.where(tri, jnp.exp(jnp.minimum(gc_col - gc_row, 0.0)), 0.0)
            kb = kc.astype(BF16)
            kk = _dot_nt(kb, kb)
            n_mat = jnp.where(strict, b_col * kk * decay, 0.0)
            t_inv = _unit_lower_inverse(n_mat)
            e_gc = jnp.exp(gc_col)
            u = _dot3(t_inv, vc * b_col)
            w = _dot3(t_inv, kc * (b_col * e_gc))
            qk = jnp.where(tri, _dot_nt(qc.astype(BF16), kb) * decay, 0.0)
            g_last = gc_col[c - 1:c, :]
            q_dec = qc * e_gc
            k_dec = kc * jnp.exp(g_last - gc_col)
            sb = state.astype(BF16)
            v_new = u - _dot(w.astype(BF16), sb)
            vnb = v_new.astype(BF16)
            o_c = _dot(q_dec.astype(BF16), sb) + _dot(qk.astype(BF16), vnb)
            state = state * jnp.exp(g_last) + _dot_tn(k_dec.astype(BF16), vnb)
            zc = z_ref[rows, lanes]
            o_ref[rows, lanes] = (_rms(o_c, onw_ref[...]) * _silu(zc)).astype(o_ref.dtype)
        st_ref[hh] = state


def _gdn(proj, conv_w, g_col, b_col, g_row, out_norm, batch, seq):
    t = proj.shape[0]
    tb = min(GDN_TB, seq)
    hg = GDN_HG
    w = hg * A_DK
    nj = seq // tb
    ng = A_HEADS // hg
    kq = A_HEADS * A_DK
    off_k = kq // w
    off_v = 2 * kq // w
    off_z = 3 * kq // w

    def colspec(off):
        return pl.BlockSpec((tb, w), lambda b, g, j: (b * nj + j, off + g))

    def cwspec(off):
        return pl.BlockSpec((A_CONV, w), lambda b, g, j: (0, off + g))

    colv = pl.BlockSpec((hg, tb, 1), lambda b, g, j: (g, b * nj + j, 0))
    rowv = pl.BlockSpec((hg, 1, tb), lambda b, g, j: (g, 0, b * nj + j))
    return pl.pallas_call(
        functools.partial(_gdn_kernel, tb=tb, hg=hg),
        grid=(batch, ng, nj),
        in_specs=[colspec(0), colspec(off_k), colspec(off_v), colspec(off_z),
                  cwspec(0), cwspec(off_k), cwspec(off_v), colv, colv, rowv,
                  pl.BlockSpec((1, A_DV), lambda b, g, j: (0, 0))],
        out_specs=pl.BlockSpec((tb, w), lambda b, g, j: (b * nj + j, g)),
        out_shape=jax.ShapeDtypeStruct((t, A_HEADS * A_DV), BF16),
        scratch_shapes=[pltpu.VMEM((tb + 8, w), F32), pltpu.VMEM((tb + 8, w), F32),
                        pltpu.VMEM((tb + 8, w), F32), pltpu.VMEM((hg, A_DK, A_DV), F32)],
        compiler_params=_cparams(3),
        name="gated_deltanet",
    )(proj, proj, proj, proj, conv_w, conv_w, conv_w, g_col, b_col, g_row, out_norm.reshape(1, A_DV))


def _rope_lanes(x, cos, sin_signed):
    return x * cos + pltpu.roll(x, LANES // 2, 1) * sin_signed


def _mla_q_kernel(cq_ref, nw_ref, w_ref, cos_ref, sin_ref, o_ref):
    n = _rms(cq_ref[...], nw_ref[...]).astype(BF16)
    cos = cos_ref[...]
    sin = sin_ref[...]
    for h in range(B_HEADS):
        qh = _dot(n, w_ref[:, h * 256:(h + 1) * 256])
        o_ref[:, h * 256:h * 256 + LANES] = qh[:, :LANES].astype(o_ref.dtype)
        o_ref[:, h * 256 + LANES:(h + 1) * 256] = _rope_lanes(qh[:, LANES:], cos, sin).astype(o_ref.dtype)


def _mla_q(proj, col_block, q_norm, w_uq_p, cos_b, sin_b):
    t = proj.shape[0]
    tm = min(512, t)
    n = w_uq_p.shape[1]
    return pl.pallas_call(
        _mla_q_kernel,
        grid=(t // tm,),
        in_specs=[pl.BlockSpec((tm, B_Q_LORA), lambda i: (i, col_block)),
                  pl.BlockSpec((1, B_Q_LORA), lambda i: (0, 0)),
                  pl.BlockSpec((B_Q_LORA, n), lambda i: (0, 0)),
                  pl.BlockSpec((tm, LANES), lambda i: (i, 0)),
                  pl.BlockSpec((tm, LANES), lambda i: (i, 0))],
        out_specs=pl.BlockSpec((tm, n), lambda i: (i, 0)),
        out_shape=jax.ShapeDtypeStruct((t, n), BF16),
        compiler_params=_cparams(1),
        name="mla_q_up",
    )(proj, q_norm.reshape(1, B_Q_LORA), w_uq_p, cos_b, sin_b)


def _mla_kv_kernel(ckv_ref, nw_ref, w_ref, kpe_ref, cos_ref, sin_ref, k_ref, v_ref):
    n = _rms(ckv_ref[...], nw_ref[...]).astype(BF16)
    kpe = _rope_lanes(kpe_ref[...], cos_ref[...], sin_ref[...]).astype(k_ref.dtype)
    nk = B_HEADS * B_NOPE
    kn = _dot(n, w_ref[:, :nk])
    for h in range(B_HEADS):
        k_ref[:, h * 256:h * 256 + LANES] = kn[:, h * LANES:(h + 1) * LANES].astype(k_ref.dtype)
        k_ref[:, h * 256 + LANES:(h + 1) * 256] = kpe
    v_ref[...] = _dot(n, w_ref[:, nk:]).astype(v_ref.dtype)


def _mla_kv(proj, col_block, kv_norm, w_ukv_p, slab, cos_b, sin_b):
    t = proj.shape[0]
    tm = min(512, t)
    n = w_ukv_p.shape[1]
    tab = pl.BlockSpec((tm, LANES), lambda i: (i, 0))
    return pl.pallas_call(
        _mla_kv_kernel,
        grid=(t // tm,),
        in_specs=[pl.BlockSpec((tm, B_KV_LORA), lambda i: (i, col_block)),
                  pl.BlockSpec((1, B_KV_LORA), lambda i: (0, 0)),
                  pl.BlockSpec((B_KV_LORA, n), lambda i: (0, 0)),
                  tab, tab, tab],
        out_specs=[pl.BlockSpec((tm, B_HEADS * 256), lambda i: (i, 0)),
                   pl.BlockSpec((tm, B_HEADS * B_DV), lambda i: (i, 0))],
        out_shape=[jax.ShapeDtypeStruct((t, B_HEADS * 256), BF16),
                   jax.ShapeDtypeStruct((t, B_HEADS * B_DV), BF16)],
        compiler_params=_cparams(1),
        name="mla_kv_up",
    )(proj, kv_norm.reshape(1, B_KV_LORA), w_ukv_p, slab, cos_b, sin_b)


ATTN_T = 512


def _attn_kernel(*refs, tq, mask_gran, has_bias):
    if has_bias:
        q_ref, k_ref, v_ref, cq_ref, ck_ref, o_ref, m_ref, l_ref, acc_ref = refs
    else:
        q_ref, k_ref, v_ref, o_ref, m_ref, l_ref, acc_ref = refs
    qi = pl.program_id(2)
    q = q_ref[...]
    m_ref[...] = jnp.full(m_ref.shape, NEG_INF, F32)
    l_ref[...] = jnp.zeros_like(l_ref)
    acc_ref[...] = jnp.zeros_like(acc_ref)

    def step(jb, masked):
        start = pl.multiple_of(jb * tq, tq)
        k = k_ref[pl.ds(start, tq), :]
        v = v_ref[pl.ds(start, tq), :]
        s = _dot_nt(q, k)
        if has_bias:
            s = s + cq_ref[...] - ck_ref[jb]
        if masked:
            r = lax.broadcasted_iota(jnp.int32, (tq, tq), 0)
            c = lax.broadcasted_iota(jnp.int32, (tq, tq), 1)
            if mask_gran > 1:
                shift = int(np.log2(mask_gran))
                r = lax.shift_right_logical(r, shift)
                c = lax.shift_right_logical(c, shift)
            s = jnp.where(c <= r, s, NEG_INF)
        m_prev = m_ref[...]
        m_new = jnp.maximum(m_prev, jnp.max(s, axis=1, keepdims=True))
        alpha = jnp.exp(m_prev - m_new)
        p = jnp.exp(s - m_new)
        l_ref[...] = alpha * l_ref[...] + jnp.sum(p, axis=1, keepdims=True)
        acc_ref[...] = alpha * acc_ref[...] + _dot(p.astype(v.dtype), v)
        m_ref[...] = m_new

    def body(jb, carry):
        step(jb, False)
        return carry

    lax.fori_loop(0, qi, body, 0)
    step(qi, True)
    o_ref[...] = (acc_ref[...] / l_ref[...]).astype(o_ref.dtype)


def _attention(q, k, v, dk, dv, heads, batch, seq, mask_gran, q_off=0, k_off=0, v_off=0, bias=None):
    t = q.shape[0]
    tq = min(ATTN_T, seq)
    nq = seq // tq
    in_specs = [pl.BlockSpec((tq, dk), lambda b, h, i: (b * nq + i, q_off + h)),
                pl.BlockSpec((seq, dk), lambda b, h, i: (b, k_off + h)),
                pl.BlockSpec((seq, dv), lambda b, h, i: (b, v_off + h))]
    args = [q, k, v]
    if bias is not None:
        cq, ck = bias
        in_specs += [pl.BlockSpec((None, None, tq, 1), lambda b, h, i: (b, h, i, 0)),
                     pl.BlockSpec((None, None, nq, 1, tq), lambda b, h, i: (b, h, 0, 0, 0))]
        args += [cq, ck]
    return pl.pallas_call(
        functools.partial(_attn_kernel, tq=tq, mask_gran=mask_gran, has_bias=bias is not None),
        grid=(batch, heads, nq),
        in_specs=in_specs,
        out_specs=pl.BlockSpec((tq, dv), lambda b, h, i: (b * nq + i, h)),
        out_shape=jax.ShapeDtypeStruct((t, heads * dv), BF16),
        scratch_shapes=[pltpu.VMEM((tq, 1), F32), pltpu.VMEM((tq, 1), F32), pltpu.VMEM((tq, dv), F32)],
        compiler_params=_cparams(3),
        name="flash_attention",
    )(*args)


RET_HG = 2
RET_TB = 512


def _ret_kernel(q_ref, k_ref, v_ref, g_ref, cos_ref, sin_ref, dec_ref, xi_ref, zeta_ref, gch_ref, gnw_ref,
                o_ref, st_ref, *, tb, hg, c):
    j = pl.program_id(2)

    @pl.when(j == 0)
    def _():
        st_ref[...] = jnp.zeros_like(st_ref)

    cos = cos_ref[...]
    sin = sin_ref[...]
    for hh in range(hg):
        lanes = slice(hh * C_DK, (hh + 1) * C_DK)
        qh = _rope_lanes(q_ref[:, lanes], cos, sin)
        kh = _rope_lanes(k_ref[:, lanes], cos, sin)
        state = st_ref[hh]
        for ci in range(tb // c):
            rows = slice(ci * c, (ci + 1) * c)
            qb = qh[rows].astype(BF16)
            kc = kh[rows]
            vb = v_ref[rows, lanes].astype(BF16)
            scores = _dot_nt(qb, kc.astype(BF16)) * dec_ref[hh]
            inner = _dot(scores.astype(BF16), vb)
            cross = _dot(qb, state.astype(BF16)) * xi_ref[hh]
            state = state * gch_ref[hh] + _dot_tn((kc * zeta_ref[hh]).astype(BF16), vb)
            o_c = inner + cross
            o_ref[rows, lanes] = (_rms(o_c, gnw_ref[hh]) * _silu(g_ref[rows, lanes])).astype(o_ref.dtype)
        st_ref[hh] = state


def _retention(proj, cos_c, sin_c, group_norm, batch, seq):
    t = proj.shape[0]
    tb = min(RET_TB, seq)
    c = min(RET_CHUNK, seq)
    hg = RET_HG
    w = hg * C_DK
    nj = seq // tb
    ng = C_HEADS // hg
    nh = C_HEADS * C_DK // w

    log_gamma = jnp.log1p(-jnp.exp2(-5.0 - jnp.arange(C_HEADS, dtype=F32)))
    idx = jnp.arange(c, dtype=F32)
    rel = idx[:, None] - idx[None, :]
    dec = jnp.where(rel >= 0, jnp.exp(log_gamma[:, None, None] * jnp.maximum(rel, 0.0)), 0.0)
    xi = jnp.exp(log_gamma[:, None] * (idx + 1.0))
    zeta = jnp.exp(log_gamma[:, None] * (c - 1.0 - idx))
    gch = jnp.exp(log_gamma * c)
    xi_b = jnp.broadcast_to(xi[:, :, None], (C_HEADS, c, C_DV))
    zeta_b = jnp.broadcast_to(zeta[:, :, None], (C_HEADS, c, C_DK))
    gch_b = jnp.broadcast_to(gch[:, None, None], (C_HEADS, 1, C_DV))

    def colspec(off):
        return pl.BlockSpec((tb, w), lambda b, g, j: (b * nj + j, off + g))

    tab = pl.BlockSpec((tb, LANES), lambda b, g, j: (b * nj + j, 0))

    def hspec(r, cdim):
        return pl.BlockSpec((hg, r, cdim), lambda b, g, j: (g, 0, 0))

    return pl.pallas_call(
        functools.partial(_ret_kernel, tb=tb, hg=hg, c=c),
        grid=(batch, ng, nj),
        in_specs=[colspec(0), colspec(nh), colspec(2 * nh), colspec(3 * nh), tab, tab,
                  hspec(c, c), hspec(c, C_DV), hspec(c, C_DK), hspec(1, C_DV), hspec(1, C_DV)],
        out_specs=pl.BlockSpec((tb, w), lambda b, g, j: (b * nj + j, g)),
        out_shape=jax.ShapeDtypeStruct((t, C_HEADS * C_DV), BF16),
        scratch_shapes=[pltpu.VMEM((hg, C_DK, C_DV), F32)],
        compiler_params=_cparams(3),
        name="retention",
    )(proj, proj, proj, proj, cos_c, sin_c, dec, xi_b, zeta_b, gch_b, group_norm.reshape(C_HEADS, 1, C_DV))


def _fox_cum_kernel(s_ref, b_ref, o_ref, *, seq):
    x = s_ref[...] + b_ref[...]
    acc = -_softplus(-x)
    row = lax.broadcasted_iota(jnp.int32, acc.shape, 0)
    shift = 1
    while shift < seq:
        acc = acc + jnp.where(row >= shift, pltpu.roll(acc, shift, 0), 0.0)
        shift *= 2
    o_ref[...] = acc


def _fox_cum(slab, forget_bias, batch, seq):
    t = slab.shape[0]
    pad = jnp.zeros((LANES - D_HEADS,), F32)
    b_row = jnp.concatenate([forget_bias.astype(F32), pad])[None, :]
    return pl.pallas_call(
        functools.partial(_fox_cum_kernel, seq=seq),
        grid=(batch,),
        in_specs=[pl.BlockSpec((seq, LANES), lambda b: (b, 0)),
                  pl.BlockSpec((1, LANES), lambda b: (0, 0))],
        out_specs=pl.BlockSpec((seq, LANES), lambda b: (b, 0)),
        out_shape=jax.ShapeDtypeStruct((t, LANES), F32),
        compiler_params=_cparams(1),
        name="fox_cumsum",
    )(slab, b_row)


def _pe_lanes(w_pe):
    k = w_pe.shape[0]
    z = jnp.zeros((k, B_ROPE // 2), w_pe.dtype)
    return jnp.concatenate([w_pe[:, :B_ROPE // 2], z, w_pe[:, B_ROPE // 2:], z], axis=1)


def _ab_weights(w_in, w_uq, w_ukv):
    nq = A_HEADS * (2 * A_DK + A_DV)
    nz = A_HEADS * A_DV
    o = 0
    w_qkv = w_in[:, o:o + nq]; o += nq
    w_z = w_in[:, o:o + nz]; o += nz
    w_a = w_in[:, o:o + A_HEADS]; o += A_HEADS
    w_b = w_in[:, o:o + A_HEADS]; o += A_HEADS
    w_cq = w_in[:, o:o + B_Q_LORA]; o += B_Q_LORA
    w_ckv = w_in[:, o:o + B_KV_LORA]; o += B_KV_LORA
    w_kpe = w_in[:, o:o + B_ROPE]
    d = w_in.shape[0]
    w_main = jnp.concatenate([w_qkv, w_z, w_cq, w_ckv], axis=1).astype(BF16)
    w_slab = jnp.concatenate([_pe_lanes(w_kpe), w_a, w_b,
                              jnp.zeros((d, LANES - 2 * A_HEADS), w_in.dtype)], axis=1).astype(BF16)
    scale = (B_NOPE + B_ROPE) ** -0.5
    wq = (w_uq * scale).reshape(B_Q_LORA, B_HEADS, B_NOPE + B_ROPE)
    wq_p = jnp.concatenate(
        [wq[:, :, :B_NOPE],
         _pe_lanes(wq[:, :, B_NOPE:].reshape(B_Q_LORA * B_HEADS, B_ROPE)).reshape(B_Q_LORA, B_HEADS, LANES)],
        axis=2).reshape(B_Q_LORA, B_HEADS * 256).astype(BF16)
    wkv = w_ukv.reshape(B_KV_LORA, B_HEADS, B_NOPE + B_DV)
    wkv_p = jnp.concatenate([wkv[:, :, :B_NOPE].reshape(B_KV_LORA, B_HEADS * B_NOPE),
                             wkv[:, :, B_NOPE:].reshape(B_KV_LORA, B_HEADS * B_DV)], axis=1).astype(BF16)
    return w_main, w_slab, wq_p, wkv_p


def _cd_weights(w_in):
    n = C_HEADS * C_DK
    w_qc = w_in[:, 0:n]
    w_kc = w_in[:, n:2 * n] * (C_DK ** -0.5)
    w_vc = w_in[:, 2 * n:3 * n]
    w_gc = w_in[:, 3 * n:4 * n]
    w_qd = w_in[:, 4 * n:5 * n] * (D_DH ** -0.5)
    w_kd = w_in[:, 5 * n:6 * n]
    w_vd = w_in[:, 6 * n:7 * n]
    w_fd = w_in[:, 7 * n:7 * n + D_HEADS]
    d = w_in.shape[0]
    w_c = jnp.concatenate([w_qc, w_kc, w_vc, w_gc], axis=1).astype(BF16)
    w_d = jnp.concatenate([w_qd, w_kd, w_vd], axis=1).astype(BF16)
    w_f = jnp.concatenate([w_fd, jnp.zeros((d, LANES - D_HEADS), w_in.dtype)], axis=1).astype(BF16)
    return w_c, w_d, w_f


def _layer_ab(h, rope, w_in, conv_w, a_log, dt_bias, out_norm, q_norm, w_uq, kv_norm, w_ukv, batch, seq):
    cos_b, sin_b, _, _ = rope
    w_main, w_slab, wq_p, wkv_p = _ab_weights(w_in, w_uq, w_ukv)
    proj = _mm(h, w_main, F32)
    slab = _mm(h, w_slab, F32, tn=256)
    gates = _gdn_gates(slab, 1, a_log, dt_bias)
    gt = gates[:, :2 * A_HEADS].T
    t = h.shape[0]
    g_col = gt[:A_HEADS].reshape(A_HEADS, t, 1)
    b_col = gt[A_HEADS:].reshape(A_HEADS, t, 1)
    g_row = gt[:A_HEADS].reshape(A_HEADS, 1, t)
    o_a = _gdn(proj, conv_w, g_col, b_col, g_row, out_norm, batch, seq)
    nq = A_HEADS * (2 * A_DK + A_DV) + A_HEADS * A_DV
    q_b = _mla_q(proj, nq // B_Q_LORA, q_norm, wq_p, cos_b, sin_b)
    k_b, v_b = _mla_kv(proj, (nq + B_Q_LORA) // B_KV_LORA, kv_norm, wkv_p, slab, cos_b, sin_b)
    o_b = _attention(q_b, k_b, v_b, 256, B_DV, B_HEADS, batch, seq, CHUNK)
    return jnp.concatenate([o_a, o_b], axis=1)


def _layer_cd(h, rope, w_in, group_norm, forget_bias, batch, seq):
    _, _, cos_c, sin_c = rope
    w_c, w_d, w_f = _cd_weights(w_in)
    proj_c = _mm(h, w_c, F32)
    proj_d = _mm(h, w_d, BF16)
    slab = _mm(h, w_f, F32, tn=LANES)
    o_c = _retention(proj_c, cos_c, sin_c, group_norm, batch, seq)
    cum = _fox_cum(slab, forget_bias, batch, seq)
    tq = min(ATTN_T, seq)
    cum_t = cum[:, :D_HEADS].reshape(batch, seq, D_HEADS).transpose(0, 2, 1)
    cq = cum_t.reshape(batch, D_HEADS, seq, 1)
    ck = cum_t.reshape(batch, D_HEADS, seq // tq, 1, tq)
    o_d = _attention(proj_d, proj_d, proj_d, D_DH, D_DH, D_HEADS, batch, seq, 1,
                     q_off=0, k_off=D_HEADS, v_off=2 * D_HEADS, bias=(cq, ck))
    return jnp.concatenate([o_c, o_d], axis=1)


def kernel(x, c, positions, ada_w, ada_b, mix_pre_norm, mix_post_norm, ffn_pre_norm, ffn_post_norm, ab_w_in, ab_conv_w, ab_a_log, ab_dt_bias, ab_out_norm, ab_q_norm, ab_w_uq, ab_kv_norm, ab_w_ukv, ab_w_out, cd_w_in, cd_group_norm, cd_forget_bias, cd_w_out, ffn_w_gate, ffn_w_up, ffn_w_down):
    batch, seq, d = x.shape
    depth = ada_w.shape[0]
    t = batch * seq
    rows = 8
    c_pad = jnp.concatenate([c, jnp.zeros((rows - batch, d), c.dtype)], axis=0)
    mod = _ada(c_pad, ada_w, ada_b).reshape(depth, rows, 6, d)
    rope = _rope_tables(positions.astype(F32).reshape(t, 1))
    x2 = x.reshape(t, d)
    h = _prenorm(x2, mod[0], mix_pre_norm[0], seq, 0, 1)
    for layer in range(depth):
        j = layer // 2
        if layer % 2 == 0:
            y_in = _layer_ab(h, rope, ab_w_in[j], ab_conv_w[j], ab_a_log[j], ab_dt_bias[j], ab_out_norm[j],
                             ab_q_norm[j], ab_w_uq[j], ab_kv_norm[j], ab_w_ukv[j], batch, seq)
            w_out = ab_w_out[j]
        else:
            y_in = _layer_cd(h, rope, cd_w_in[j], cd_group_norm[j], cd_forget_bias[j], batch, seq)
            w_out = cd_w_out[j]
        x2, h = _mm_post(y_in, w_out.astype(BF16), x2, mod[layer], mod[layer], mix_post_norm[layer],
                         ffn_pre_norm[layer], seq, 2, 3, 4, True, tk=w_out.shape[0])
        hid = _ffn_up(h, ffn_w_gate[layer].astype(BF16), ffn_w_up[layer].astype(BF16))
        last = layer == depth - 1
        nxt = layer if last else layer + 1
        x2, h = _mm_post(hid, ffn_w_down[layer].astype(BF16), x2, mod[layer], mod[nxt], ffn_post_norm[layer],
                         mix_pre_norm[nxt], seq, 5, 0, 1, not last)
    return x2.reshape(batch, seq, d)
```

```python
import functools

import jax
import jax.numpy as jnp
import numpy as np
from jax import lax
from jax.experimental import pallas as pl
from jax.experimental.pallas import tpu as pltpu

F32 = jnp.float32
BF16 = jnp.bfloat16

D_MODEL = 2048
CHUNK = 64
EPS = 1e-6
ROPE_BASE = 10000.0
NEG_INF = -1e30

A_HEADS, A_DK, A_DV, A_CONV = 8, 128, 128, 4
B_HEADS, B_NOPE, B_ROPE, B_DV, B_Q_LORA, B_KV_LORA = 8, 128, 64, 128, 512, 512
C_HEADS, C_DK, C_DV = 8, 128, 128
D_HEADS, D_DH = 8, 128
LANES = 128
SUBLANES = 8

VMEM_LIMIT_BYTES = 56 * 1024 * 1024

RET_CHUNK = 128


def _cparams(n_axes):
    return pltpu.CompilerParams(dimension_semantics=("arbitrary",) * n_axes,
                                vmem_limit_bytes=VMEM_LIMIT_BYTES)


def _dot(a, b):
    return jnp.dot(a, b, preferred_element_type=F32)


def _dot_nt(a, b):
    return lax.dot_general(a, b, (((1,), (1,)), ((), ())), preferred_element_type=F32)


def _dot_tn(a, b):
    return lax.dot_general(a, b, (((0,), (0,)), ((), ())), preferred_element_type=F32)


def _split_bf16(a):
    hi = a.astype(BF16)
    lo = (a - hi.astype(F32)).astype(BF16)
    return hi, lo


def _dot3(a, b):
    ah, al = _split_bf16(a)
    bh, bl = _split_bf16(b)
    return _dot(ah, bh) + _dot(ah, bl) + _dot(al, bh)


def _sigmoid(x):
    return 1.0 / (1.0 + jnp.exp(-x))


def _silu(x):
    return x * _sigmoid(x)


def _softplus(x):
    return jnp.maximum(x, 0.0) + jnp.log(1.0 + jnp.exp(-jnp.abs(x)))


def _rms(x, w):
    return x * lax.rsqrt(jnp.mean(x * x, axis=-1, keepdims=True) + EPS) * w


def _ada_kernel(c_ref, w_ref, b_ref, o_ref):
    cond = _silu(c_ref[...]).astype(BF16)
    o_ref[...] = _dot(cond, w_ref[...].astype(BF16)) + b_ref[...]


def _ada(c_pad, ada_w, ada_b):
    depth, d, n = ada_w.shape
    rows = c_pad.shape[0]
    tn = 1024
    return pl.pallas_call(
        _ada_kernel,
        grid=(depth, n // tn),
        in_specs=[pl.BlockSpec((rows, d), lambda l, j: (0, 0)),
                  pl.BlockSpec((None, d, tn), lambda l, j: (l, 0, j)),
                  pl.BlockSpec((None, 1, tn), lambda l, j: (l, 0, j))],
        out_specs=pl.BlockSpec((None, rows, tn), lambda l, j: (l, 0, j)),
        out_shape=jax.ShapeDtypeStruct((depth, rows, n), F32),
        compiler_params=_cparams(2),
        name="ada_mod",
    )(c_pad, ada_w, ada_b.reshape(depth, 1, n))


def _rope_kernel(pos_ref, fb_ref, mb_ref, sb_ref, fc_ref, sc_ref, cb_ref, snb_ref, cc_ref, snc_ref):
    pos = pos_ref[...]
    ang_b = pos * fb_ref[...]
    cb_ref[...] = jnp.cos(ang_b) * mb_ref[...]
    snb_ref[...] = jnp.sin(ang_b) * sb_ref[...]
    ang_c = pos * fc_ref[...]
    cc_ref[...] = jnp.cos(ang_c)
    snc_ref[...] = jnp.sin(ang_c) * sc_ref[...]


def _rope_tables(pos_col):
    t = pos_col.shape[0]
    tb = min(1024, t)
    inv_b = ROPE_BASE ** (-jnp.arange(0, B_ROPE, 2, dtype=F32) / B_ROPE)
    inv_c = ROPE_BASE ** (-jnp.arange(0, C_DK, 2, dtype=F32) / C_DK)
    z32 = jnp.zeros((B_ROPE // 2,), F32)
    o32 = jnp.ones((B_ROPE // 2,), F32)
    o64 = jnp.ones((C_DK // 2,), F32)
    fb = jnp.concatenate([inv_b, z32, inv_b, z32])[None, :]
    mb = jnp.concatenate([o32, z32, o32, z32])[None, :]
    sb = jnp.concatenate([-o32, z32, o32, z32])[None, :]
    fc = jnp.concatenate([inv_c, inv_c])[None, :]
    sc = jnp.concatenate([-o64, o64])[None, :]
    row = pl.BlockSpec((1, LANES), lambda i: (0, 0))
    tab = pl.BlockSpec((tb, LANES), lambda i: (i, 0))
    shp = jax.ShapeDtypeStruct((t, LANES), F32)
    return pl.pallas_call(
        _rope_kernel,
        grid=(t // tb,),
        in_specs=[pl.BlockSpec((tb, 1), lambda i: (i, 0)), row, row, row, row, row],
        out_specs=[tab, tab, tab, tab],
        out_shape=[shp, shp, shp, shp],
        compiler_params=_cparams(1),
        name="rope_tables",
    )(pos_col, fb, mb, sb, fc, sc)


def _prenorm_kernel(x_ref, mod_ref, w_ref, h_ref, *, shift_idx, scale_idx):
    y = _rms(x_ref[...], w_ref[...])
    h = y * (1.0 + mod_ref[scale_idx:scale_idx + 1, :]) + mod_ref[shift_idx:shift_idx + 1, :]
    h_ref[...] = h.astype(h_ref.dtype)


def _prenorm(x2, mod_l, w, seq, shift_idx, scale_idx):
    t, d = x2.shape
    tm = min(512, seq)
    bpb = seq // tm
    return pl.pallas_call(
        functools.partial(_prenorm_kernel, shift_idx=shift_idx, scale_idx=scale_idx),
        grid=(t // tm,),
        in_specs=[pl.BlockSpec((tm, d), lambda i: (i, 0)),
                  pl.BlockSpec((None, 6, d), lambda i: (i // bpb, 0, 0)),
                  pl.BlockSpec((1, d), lambda i: (0, 0))],
        out_specs=pl.BlockSpec((tm, d), lambda i: (i, 0)),
        out_shape=jax.ShapeDtypeStruct((t, d), BF16),
        compiler_params=_cparams(1),
        name="prenorm",
    )(x2, mod_l, w.reshape(1, d))


def _mm_kernel(a_ref, w_ref, o_ref):
    o_ref[...] = _dot(a_ref[...], w_ref[...]).astype(o_ref.dtype)


def _mm(a, w, out_dtype, tm=512, tn=1024):
    m, k = a.shape
    n = w.shape[1]
    tm = min(tm, m)
    tn = min(tn, n)
    return pl.pallas_call(
        _mm_kernel,
        grid=(n // tn, m // tm),
        in_specs=[pl.BlockSpec((tm, k), lambda j, i: (i, 0)),
                  pl.BlockSpec((k, tn), lambda j, i: (0, j))],
        out_specs=pl.BlockSpec((tm, tn), lambda j, i: (i, j)),
        out_shape=jax.ShapeDtypeStruct((m, n), out_dtype),
        compiler_params=_cparams(2),
        name="matmul",
    )(a, w)


def _mm_post_kernel(a_ref, w_ref, x_ref, mod_ref, modn_ref, pw_ref, nw_ref, *rest, nk, gate_idx,
                    nshift_idx, nscale_idx, emit_h):
    if emit_h:
        xo_ref, h_ref, acc_ref = rest
    else:
        xo_ref, acc_ref = rest
    k = pl.program_id(1)

    @pl.when(k == 0)
    def _():
        acc_ref[...] = jnp.zeros_like(acc_ref)

    acc_ref[...] += _dot(a_ref[...], w_ref[...])

    @pl.when(k == nk - 1)
    def _():
        yn = _rms(acc_ref[...], pw_ref[...])
        xn = x_ref[...] + mod_ref[gate_idx:gate_idx + 1, :] * yn
        xo_ref[...] = xn
        if emit_h:
            hn = _rms(xn, nw_ref[...])
            hn = hn * (1.0 + modn_ref[nscale_idx:nscale_idx + 1, :]) + modn_ref[nshift_idx:nshift_idx + 1, :]
            h_ref[...] = hn.astype(h_ref.dtype)


def _mm_post(a, w, x2, mod_l, modn_l, post_w, next_w, seq, gate_idx, nshift_idx, nscale_idx, emit_h,
             tm=512, tk=512):
    m, kdim = a.shape
    d = w.shape[1]
    tm = min(tm, seq)
    tk = min(tk, kdim)
    nk = kdim // tk
    bpb = seq // tm
    row = pl.BlockSpec((1, d), lambda i, k: (0, 0))
    modspec = pl.BlockSpec((None, 6, d), lambda i, k: (i // bpb, 0, 0))
    xspec = pl.BlockSpec((tm, d), lambda i, k: (i, 0))
    out_specs = [xspec]
    out_shape = [jax.ShapeDtypeStruct((m, d), F32)]
    if emit_h:
        out_specs.append(xspec)
        out_shape.append(jax.ShapeDtypeStruct((m, d), BF16))
    res = pl.pallas_call(
        functools.partial(_mm_post_kernel, nk=nk, gate_idx=gate_idx, nshift_idx=nshift_idx,
                          nscale_idx=nscale_idx, emit_h=emit_h),
        grid=(m // tm, nk),
        in_specs=[pl.BlockSpec((tm, tk), lambda i, k: (i, k)),
                  pl.BlockSpec((tk, d), lambda i, k: (k, 0)),
                  xspec, modspec, modspec, row, row],
        out_specs=out_specs,
        out_shape=out_shape,
        scratch_shapes=[pltpu.VMEM((tm, d), F32)],
        compiler_params=_cparams(2),
        name="matmul_post",
    )(a, w, x2, mod_l, modn_l, post_w.reshape(1, d), next_w.reshape(1, d))
    return res if emit_h else (res[0], None)


def _ffn_up_kernel(a_ref, wg_ref, wu_ref, o_ref):
    a = a_ref[...]
    g = _dot(a, wg_ref[...])
    u = _dot(a, wu_ref[...])
    o_ref[...] = (_silu(g) * u).astype(o_ref.dtype)


def _ffn_up(a, wg, wu, tm=512, tn=512):
    m, k = a.shape
    n = wg.shape[1]
    tm = min(tm, m)
    wspec = pl.BlockSpec((k, tn), lambda j, i: (0, j))
    return pl.pallas_call(
        _ffn_up_kernel,
        grid=(n // tn, m // tm),
        in_specs=[pl.BlockSpec((tm, k), lambda j, i: (i, 0)), wspec, wspec],
        out_specs=pl.BlockSpec((tm, tn), lambda j, i: (i, j)),
        out_shape=jax.ShapeDtypeStruct((m, n), BF16),
        compiler_params=_cparams(2),
        name="ffn_up",
    )(a, wg, wu)


def _gdn_gate_kernel(s_ref, alog_ref, dt_ref, o_ref):
    s = s_ref[...]
    g = -jnp.exp(alog_ref[...]) * _softplus(s + dt_ref[...])
    beta = _sigmoid(s)
    lane = lax.broadcasted_iota(jnp.int32, s.shape, 1)
    o_ref[...] = jnp.where(lane < A_HEADS, g, beta)


def _gdn_gates(slab_src, col_block, a_log, dt_bias):
    t = slab_src.shape[0]
    tb = min(1024, t)
    pad = jnp.zeros((LANES - A_HEADS,), F32)
    alog_row = jnp.concatenate([a_log.astype(F32), pad])[None, :]
    dt_row = jnp.concatenate([dt_bias.astype(F32), pad])[None, :]
    row = pl.BlockSpec((1, LANES), lambda i: (0, 0))
    return pl.pallas_call(
        _gdn_gate_kernel,
        grid=(t // tb,),
        in_specs=[pl.BlockSpec((tb, LANES), lambda i: (i, col_block)), row, row],
        out_specs=pl.BlockSpec((tb, LANES), lambda i: (i, 0)),
        out_shape=jax.ShapeDtypeStruct((t, LANES), F32),
        compiler_params=_cparams(1),
        name="gdn_gates",
    )(slab_src, alog_row, dt_row)


GDN_HG = 2
GDN_TB = 256


def _unit_lower_inverses(n_mats):
    c = n_mats[0].shape[0]
    r = lax.broadcasted_iota(jnp.int32, (c, c), 0)
    col = lax.broadcasted_iota(jnp.int32, (c, c), 1)
    eye = jnp.where(r == col, 1.0, 0.0)
    ps = [eye - n for n in n_mats]
    ms = [_dot3(n, n) for n in n_mats]
    power = 2
    while True:
        ps = [p + _dot3(p, m) for p, m in zip(ps, ms)]
        power *= 2
        if power >= c:
            break
        ms = [_dot3(m, m) for m in ms]
    return ps


def _gdn_kernel(q_ref, k_ref, v_ref, z_ref, cwq_ref, cwk_ref, cwv_ref, gcol_ref, bcol_ref, grow_ref, onw_ref,
                o_ref, xq_ref, xk_ref, xv_ref, st_ref, *, tb, hg):
    j = pl.program_id(2)

    @pl.when(j == 0)
    def _():
        st_ref[...] = jnp.zeros_like(st_ref)
        for h_ref in (xq_ref, xk_ref, xv_ref):
            h_ref[...] = jnp.zeros_like(h_ref)

    def conv_silu(src_ref, hist_ref, cw_ref):
        x = src_ref[...]
        prev = hist_ref[...]
        hrows = prev.shape[0]
        row = lax.broadcasted_iota(jnp.int32, prev.shape, 0)
        acc = x * cw_ref[A_CONV - 1:A_CONV, :]
        for s in range(1, A_CONV):
            xs = pltpu.roll(x, s, 0)
            top = jnp.where(row < s, pltpu.roll(prev, s, 0), xs[0:hrows])
            shifted = jnp.concatenate([top, xs[hrows:]], axis=0)
            acc = acc + shifted * cw_ref[A_CONV - 1 - s:A_CONV - s, :]
        hist_ref[...] = x[tb - hrows:tb]
        return _silu(acc)

    q_all = conv_silu(q_ref, xq_ref, cwq_ref)
    k_all = conv_silu(k_ref, xk_ref, cwk_ref)
    v_all = conv_silu(v_ref, xv_ref, cwv_ref)

    c = CHUNK
    r = lax.broadcasted_iota(jnp.int32, (c, c), 0)
    col = lax.broadcasted_iota(jnp.int32, (c, c), 1)
    tri = r >= col
    strict = r > col

    nch = tb // c
    items = [(hh, ci) for hh in range(hg) for ci in range(nch)]
    qn, kn, vn = [], [], []
    for hh in range(hg):
        lanes = slice(hh * A_DK, (hh + 1) * A_DK)
        qh = q_all[:, lanes]
        kh = k_all[:, lanes]
        qn.append(qh * lax.rsqrt(jnp.sum(qh * qh, axis=-1, keepdims=True) + EPS) * (A_DK ** -0.5))
        kn.append(kh * lax.rsqrt(jnp.sum(kh * kh, axis=-1, keepdims=True) + EPS))
        vn.append(v_all[:, lanes])

    def rows_of(ci):
        return slice(ci * c, (ci + 1) * c)

    qc = [qn[hh][rows_of(ci)] for hh, ci in items]
    kc = [kn[hh][rows_of(ci)] for hh, ci in items]
    vc = [vn[hh][rows_of(ci)] for hh, ci in items]
    b_col = [bcol_ref[hh, rows_of(ci), :] for hh, ci in items]
    gc_col = [jnp.sum(jnp.where(tri, grow_ref[hh][:, rows_of(ci)], 0.0), axis=1, keepdims=True)
              for hh, ci in items]
    gc_row = [jnp.sum(jnp.where(r <= col, gcol_ref[hh, rows_of(ci), :], 0.0), axis=0, keepdims=True)
              for hh, ci in items]
    decay = [jnp.where(tri, jnp.exp(jnp.minimum(a - b, 0.0)), 0.0) for a, b in zip(gc_col, gc_row)]
    e_gc = [jnp.exp(a) for a in gc_col]
    g_last = [a[c - 1:c, :] for a in gc_col]
    kb = [x.astype(BF16) for x in kc]
    kk = [_dot_nt(x, x) for x in kb]
    qk = [_dot_nt(x.astype(BF16), y) for x, y in zip(qc, kb)]
    n_mat = [jnp.where(strict, b * x * d, 0.0) for b, x, d in zip(b_col, kk, decay)]
    qk = [jnp.where(tri, x * d, 0.0).astype(BF16) for x, d in zip(qk, decay)]
    t_inv = _unit_lower_inverses(n_mat)
    wu = [_dot3(t, jnp.concatenate([k * (b * e), v * b], axis=1)).astype(BF16)
          for t, k, v, b, e in zip(t_inv, kc, vc, b_col, e_gc)]
    kdb = [(k * jnp.exp(gl - g)).astype(BF16) for k, gl, g in zip(kc, g_last, gc_col)]
    mb = [_dot_tn(kd, x) for kd, x in zip(kdb, wu)]
    m_c = [x[:, :A_DK].astype(BF16) for x in mb]
    b_c = [x[:, A_DK:] for x in mb]
    qwu = [_dot(a, x) for a, x in zip(qk, wu)]
    q_eff = [(q * e - x[:, :A_DK]).astype(BF16) for q, e, x in zip(qc, e_gc, qwu)]
    qku = [x[:, A_DK:] for x in qwu]
    e_gl = [jnp.exp(x) for x in g_last]
    states = [st_ref[hh] for hh in range(hg)]
    o_c = [None] * len(items)
    for ci in range(nch):
        idx = [hh * nch + ci for hh in range(hg)]
        sb = [s.astype(BF16) for s in states]
        for hh in range(hg):
            o_c[idx[hh]] = _dot(q_eff[idx[hh]], sb[hh]) + qku[idx[hh]]
        states = [states[hh] * e_gl[idx[hh]] + b_c[idx[hh]] - _dot(m_c[idx[hh]], sb[hh]) for hh in range(hg)]
    for hh in range(hg):
        st_ref[hh] = states[hh]
    for i, (hh, ci) in enumerate(items):
        lanes = slice(hh * A_DK, (hh + 1) * A_DK)
        zc = z_ref[rows_of(ci), lanes]
        o_ref[rows_of(ci), lanes] = (_rms(o_c[i], onw_ref[...]) * _silu(zc)).astype(o_ref.dtype)


def _gdn(proj, conv_w, g_col, b_col, g_row, out_norm, batch, seq):
    t = proj.shape[0]
    tb = min(GDN_TB, seq)
    hg = GDN_HG
    w = hg * A_DK
    nj = seq // tb
    ng = A_HEADS // hg
    kq = A_HEADS * A_DK
    off_k = kq // w
    off_v = 2 * kq // w
    off_z = 3 * kq // w

    def colspec(off):
        return pl.BlockSpec((tb, w), lambda b, g, j: (b * nj + j, off + g))

    def cwspec(off):
        return pl.BlockSpec((A_CONV, w), lambda b, g, j: (0, off + g))

    colv = pl.BlockSpec((hg, tb, 1), lambda b, g, j: (g, b * nj + j, 0))
    rowv = pl.BlockSpec((hg, 1, tb), lambda b, g, j: (g, 0, b * nj + j))
    return pl.pallas_call(
        functools.partial(_gdn_kernel, tb=tb, hg=hg),
        grid=(batch, ng, nj),
        in_specs=[colspec(0), colspec(off_k), colspec(off_v), colspec(off_z),
                  cwspec(0), cwspec(off_k), cwspec(off_v), colv, colv, rowv,
                  pl.BlockSpec((1, A_DV), lambda b, g, j: (0, 0))],
        out_specs=pl.BlockSpec((tb, w), lambda b, g, j: (b * nj + j, g)),
        out_shape=jax.ShapeDtypeStruct((t, A_HEADS * A_DV), BF16),
        scratch_shapes=[pltpu.VMEM((SUBLANES, w), F32), pltpu.VMEM((SUBLANES, w), F32),
                        pltpu.VMEM((SUBLANES, w), F32), pltpu.VMEM((hg, A_DK, A_DV), F32)],
        compiler_params=_cparams(3),
        name="gated_deltanet",
    )(proj, proj, proj, proj, conv_w, conv_w, conv_w, g_col, b_col, g_row, out_norm.reshape(1, A_DV))


def _rope_lanes(x, cos, sin_signed):
    return x * cos + pltpu.roll(x, LANES // 2, 1) * sin_signed


def _mla_q_kernel(cq_ref, nw_ref, w_ref, cos_ref, sin_ref, o_ref):
    n = _rms(cq_ref[...], nw_ref[...]).astype(BF16)
    cos = cos_ref[...]
    sin = sin_ref[...]
    for h in range(B_HEADS):
        qh = _dot(n, w_ref[:, h * 256:(h + 1) * 256])
        o_ref[:, h * 256:h * 256 + LANES] = qh[:, :LANES].astype(o_ref.dtype)
        o_ref[:, h * 256 + LANES:(h + 1) * 256] = _rope_lanes(qh[:, LANES:], cos, sin).astype(o_ref.dtype)


def _mla_q(proj, col_block, q_norm, w_uq_p, cos_b, sin_b):
    t = proj.shape[0]
    tm = min(512, t)
    n = w_uq_p.shape[1]
    return pl.pallas_call(
        _mla_q_kernel,
        grid=(t // tm,),
        in_specs=[pl.BlockSpec((tm, B_Q_LORA), lambda i: (i, col_block)),
                  pl.BlockSpec((1, B_Q_LORA), lambda i: (0, 0)),
                  pl.BlockSpec((B_Q_LORA, n), lambda i: (0, 0)),
                  pl.BlockSpec((tm, LANES), lambda i: (i, 0)),
                  pl.BlockSpec((tm, LANES), lambda i: (i, 0))],
        out_specs=pl.BlockSpec((tm, n), lambda i: (i, 0)),
        out_shape=jax.ShapeDtypeStruct((t, n), BF16),
        compiler_params=_cparams(1),
        name="mla_q_up",
    )(proj, q_norm.reshape(1, B_Q_LORA), w_uq_p, cos_b, sin_b)


def _mla_kv_kernel(ckv_ref, nw_ref, w_ref, kpe_ref, cos_ref, sin_ref, k_ref, v_ref):
    n = _rms(ckv_ref[...], nw_ref[...]).astype(BF16)
    kpe = _rope_lanes(kpe_ref[...], cos_ref[...], sin_ref[...]).astype(k_ref.dtype)
    nk = B_HEADS * B_NOPE
    kn = _dot(n, w_ref[:, :nk])
    for h in range(B_HEADS):
        k_ref[:, h * 256:h * 256 + LANES] = kn[:, h * LANES:(h + 1) * LANES].astype(k_ref.dtype)
        k_ref[:, h * 256 + LANES:(h + 1) * 256] = kpe
    v_ref[...] = _dot(n, w_ref[:, nk:]).astype(v_ref.dtype)


def _mla_kv(proj, col_block, kv_norm, w_ukv_p, slab, cos_b, sin_b):
    t = proj.shape[0]
    tm = min(512, t)
    n = w_ukv_p.shape[1]
    tab = pl.BlockSpec((tm, LANES), lambda i: (i, 0))
    return pl.pallas_call(
        _mla_kv_kernel,
        grid=(t // tm,),
        in_specs=[pl.BlockSpec((tm, B_KV_LORA), lambda i: (i, col_block)),
                  pl.BlockSpec((1, B_KV_LORA), lambda i: (0, 0)),
                  pl.BlockSpec((B_KV_LORA, n), lambda i: (0, 0)),
                  tab, tab, tab],
        out_specs=[pl.BlockSpec((tm, B_HEADS * 256), lambda i: (i, 0)),
                   pl.BlockSpec((tm, B_HEADS * B_DV), lambda i: (i, 0))],
        out_shape=[jax.ShapeDtypeStruct((t, B_HEADS * 256), BF16),
                   jax.ShapeDtypeStruct((t, B_HEADS * B_DV), BF16)],
        compiler_params=_cparams(1),
        name="mla_kv_up",
    )(proj, kv_norm.reshape(1, B_KV_LORA), w_ukv_p, slab, cos_b, sin_b)


ATTN_T = 512


def _attn_kernel(*refs, tq, dk, dv, hg, mask_gran, has_bias):
    if has_bias:
        q_ref, k_ref, vt_ref, cq_ref, ck_ref, o_ref, m_ref, acc_ref = refs
    else:
        q_ref, k_ref, vt_ref, o_ref, m_ref, acc_ref = refs
    qi = pl.program_id(2)
    m_ref[...] = jnp.full(m_ref.shape, NEG_INF, F32)
    acc_ref[...] = jnp.zeros_like(acc_ref)
    heads = range(hg)

    def step(jb, masked):
        start = pl.multiple_of(jb * tq, tq)
        s = [_dot_nt(k_ref[pl.ds(start, tq), hh * dk:(hh + 1) * dk], q_ref[:, hh * dk:(hh + 1) * dk])
             for hh in heads]
        if has_bias:
            s = [s[hh] - ck_ref[hh, jb] for hh in heads]
        if masked:
            r = lax.broadcasted_iota(jnp.int32, (tq, tq), 0)
            c = lax.broadcasted_iota(jnp.int32, (tq, tq), 1)
            if mask_gran > 1:
                shift = int(np.log2(mask_gran))
                r = lax.shift_right_logical(r, shift)
                c = lax.shift_right_logical(c, shift)
            keep = r <= c
            s = [jnp.where(keep, s[hh], NEG_INF) for hh in heads]
        m_prev = [m_ref[hh] for hh in heads]
        m_cur = [jnp.max(s[hh], axis=0, keepdims=True) for hh in heads]
        if has_bias:
            m_cur = [m_cur[hh] + cq_ref[hh] for hh in heads]
        m_new = [jnp.maximum(m_prev[hh], m_cur[hh]) for hh in heads]
        sub = [m_new[hh] - cq_ref[hh] for hh in heads] if has_bias else m_new
        p = [jnp.exp2(s[hh] - sub[hh]).astype(BF16) for hh in heads]
        alpha = [jnp.exp2(m_prev[hh] - m_new[hh]) for hh in heads]
        for hh in heads:
            m_ref[hh] = m_new[hh]
        pv = [_dot(vt_ref[hh, jb], p[hh]) for hh in heads]
        for hh in heads:
            acc_ref[hh] = alpha[hh] * acc_ref[hh] + pv[hh]

    def body(jb, carry):
        step(jb, False)
        return carry

    lax.fori_loop(0, qi, body, 0)
    step(qi, True)
    for hh in heads:
        acc = acc_ref[hh]
        o_ref[:, hh * dv:(hh + 1) * dv] = (acc[:dv] / acc[dv:dv + 1]).T.astype(o_ref.dtype)


ATTN_HG = 4
ATTN_ONES = 16
LOG2E = 1.4426950408889634


def _attention(q, k, vt, dk, dv, heads, batch, seq, mask_gran, q_off=0, k_off=0, bias=None):
    t = q.shape[0]
    tq = min(ATTN_T, seq)
    nq = seq // tq
    hg = ATTN_HG
    qb, kb = q_off // (hg * dk), k_off // (hg * dk)
    dvx = dv + ATTN_ONES
    in_specs = [pl.BlockSpec((tq, hg * dk), lambda b, g, i: (b * nq + i, qb + g)),
                pl.BlockSpec((seq, hg * dk), lambda b, g, i: (b, kb + g)),
                pl.BlockSpec((None, hg, nq, dvx, tq), lambda b, g, i: (b, g, 0, 0, 0))]
    args = [q, k, vt]
    if bias is not None:
        cq, ck = bias
        in_specs += [pl.BlockSpec((None, hg, 1, tq), lambda b, g, i: (b, g, 0, i)),
                     pl.BlockSpec((None, hg, nq, tq, 1), lambda b, g, i: (b, g, 0, 0, 0))]
        args += [cq, ck]
    return pl.pallas_call(
        functools.partial(_attn_kernel, tq=tq, dk=dk, dv=dv, hg=hg, mask_gran=mask_gran,
                          has_bias=bias is not None),
        grid=(batch, heads // hg, nq),
        in_specs=in_specs,
        out_specs=pl.BlockSpec((tq, hg * dv), lambda b, g, i: (b * nq + i, g)),
        out_shape=jax.ShapeDtypeStruct((t, heads * dv), BF16),
        scratch_shapes=[pltpu.VMEM((hg, 1, tq), F32), pltpu.VMEM((hg, dvx, tq), F32)],
        compiler_params=_cparams(3),
        name="flash_attention",
    )(*args)


def _v_blocks_t(v2, batch, seq, heads, dv):
    tq = min(ATTN_T, seq)
    vt = v2.reshape(batch, seq // tq, tq, heads, dv).transpose(0, 3, 1, 4, 2)
    ones = jnp.ones((batch, heads, seq // tq, ATTN_ONES, tq), vt.dtype)
    return jnp.concatenate([vt, ones], axis=3)


RET_HG = 2
RET_TB = 512


def _ret_kernel(q_ref, k_ref, v_ref, g_ref, cos_ref, sin_ref, dec_ref, xi_ref, zeta_ref, gch_ref, gnw_ref,
                o_ref, st_ref, *, tb, hg, c):
    j = pl.program_id(2)

    @pl.when(j == 0)
    def _():
        st_ref[...] = jnp.zeros_like(st_ref)

    cos = cos_ref[...]
    sin = sin_ref[...]
    for hh in range(hg):
        lanes = slice(hh * C_DK, (hh + 1) * C_DK)
        qh = _rope_lanes(q_ref[:, lanes], cos, sin)
        kh = _rope_lanes(k_ref[:, lanes], cos, sin)
        state = st_ref[hh]
        for ci in range(tb // c):
            rows = slice(ci * c, (ci + 1) * c)
            qb = qh[rows].astype(BF16)
            kc = kh[rows]
            vb = v_ref[rows, lanes].astype(BF16)
            scores = _dot_nt(qb, kc.astype(BF16)) * dec_ref[hh]
            inner = _dot(scores.astype(BF16), vb)
            cross = _dot(qb, state.astype(BF16)) * xi_ref[hh]
            state = state * gch_ref[hh] + _dot_tn((kc * zeta_ref[hh]).astype(BF16), vb)
            o_c = inner + cross
            o_ref[rows, lanes] = (_rms(o_c, gnw_ref[hh]) * _silu(g_ref[rows, lanes])).astype(o_ref.dtype)
        st_ref[hh] = state


def _retention(proj, cos_c, sin_c, group_norm, batch, seq):
    t = proj.shape[0]
    tb = min(RET_TB, seq)
    c = min(RET_CHUNK, seq)
    hg = RET_HG
    w = hg * C_DK
    nj = seq // tb
    ng = C_HEADS // hg
    nh = C_HEADS * C_DK // w

    log_gamma = jnp.log1p(-jnp.exp2(-5.0 - jnp.arange(C_HEADS, dtype=F32)))
    idx = jnp.arange(c, dtype=F32)
    rel = idx[:, None] - idx[None, :]
    dec = jnp.where(rel >= 0, jnp.exp(log_gamma[:, None, None] * jnp.maximum(rel, 0.0)), 0.0)
    xi = jnp.exp(log_gamma[:, None] * (idx + 1.0))
    zeta = jnp.exp(log_gamma[:, None] * (c - 1.0 - idx))
    gch = jnp.exp(log_gamma * c)
    xi_b = jnp.broadcast_to(xi[:, :, None], (C_HEADS, c, C_DV))
    zeta_b = jnp.broadcast_to(zeta[:, :, None], (C_HEADS, c, C_DK))
    gch_b = jnp.broadcast_to(gch[:, None, None], (C_HEADS, 1, C_DV))

    def colspec(off):
        return pl.BlockSpec((tb, w), lambda b, g, j: (b * nj + j, off + g))

    tab = pl.BlockSpec((tb, LANES), lambda b, g, j: (b * nj + j, 0))

    def hspec(r, cdim):
        return pl.BlockSpec((hg, r, cdim), lambda b, g, j: (g, 0, 0))

    return pl.pallas_call(
        functools.partial(_ret_kernel, tb=tb, hg=hg, c=c),
        grid=(batch, ng, nj),
        in_specs=[colspec(0), colspec(nh), colspec(2 * nh), colspec(3 * nh), tab, tab,
                  hspec(c, c), hspec(c, C_DV), hspec(c, C_DK), hspec(1, C_DV), hspec(1, C_DV)],
        out_specs=pl.BlockSpec((tb, w), lambda b, g, j: (b * nj + j, g)),
        out_shape=jax.ShapeDtypeStruct((t, C_HEADS * C_DV), BF16),
        scratch_shapes=[pltpu.VMEM((hg, C_DK, C_DV), F32)],
        compiler_params=_cparams(3),
        name="retention",
    )(proj, proj, proj, proj, cos_c, sin_c, dec, xi_b, zeta_b, gch_b, group_norm.reshape(C_HEADS, 1, C_DV))


def _fox_cum_kernel(s_ref, b_ref, o_ref, *, seq):
    x = s_ref[...] + b_ref[...]
    acc = -_softplus(-x)
    row = lax.broadcasted_iota(jnp.int32, acc.shape, 0)
    shift = 1
    while shift < seq:
        acc = acc + jnp.where(row >= shift, pltpu.roll(acc, shift, 0), 0.0)
        shift *= 2
    o_ref[...] = acc * LOG2E


def _fox_cum(slab, forget_bias, batch, seq):
    t = slab.shape[0]
    pad = jnp.zeros((LANES - D_HEADS,), F32)
    b_row = jnp.concatenate([forget_bias.astype(F32), pad])[None, :]
    return pl.pallas_call(
        functools.partial(_fox_cum_kernel, seq=seq),
        grid=(batch,),
        in_specs=[pl.BlockSpec((seq, LANES), lambda b: (b, 0)),
                  pl.BlockSpec((1, LANES), lambda b: (0, 0))],
        out_specs=pl.BlockSpec((seq, LANES), lambda b: (b, 0)),
        out_shape=jax.ShapeDtypeStruct((t, LANES), F32),
        compiler_params=_cparams(1),
        name="fox_cumsum",
    )(slab, b_row)


def _pe_lanes(w_pe):
    k = w_pe.shape[0]
    z = jnp.zeros((k, B_ROPE // 2), w_pe.dtype)
    return jnp.concatenate([w_pe[:, :B_ROPE // 2], z, w_pe[:, B_ROPE // 2:], z], axis=1)


def _ab_weights(w_in, w_uq, w_ukv):
    nq = A_HEADS * (2 * A_DK + A_DV)
    nz = A_HEADS * A_DV
    o = 0
    w_qkv = w_in[:, o:o + nq]; o += nq
    w_z = w_in[:, o:o + nz]; o += nz
    w_a = w_in[:, o:o + A_HEADS]; o += A_HEADS
    w_b = w_in[:, o:o + A_HEADS]; o += A_HEADS
    w_cq = w_in[:, o:o + B_Q_LORA]; o += B_Q_LORA
    w_ckv = w_in[:, o:o + B_KV_LORA]; o += B_KV_LORA
    w_kpe = w_in[:, o:o + B_ROPE]
    d = w_in.shape[0]
    w_main = jnp.concatenate([w_qkv, w_z, w_cq, w_ckv], axis=1).astype(BF16)
    w_slab = jnp.concatenate([_pe_lanes(w_kpe), w_a, w_b,
                              jnp.zeros((d, LANES - 2 * A_HEADS), w_in.dtype)], axis=1).astype(BF16)
    scale = (B_NOPE + B_ROPE) ** -0.5 * LOG2E
    wq = (w_uq * scale).reshape(B_Q_LORA, B_HEADS, B_NOPE + B_ROPE)
    wq_p = jnp.concatenate(
        [wq[:, :, :B_NOPE],
         _pe_lanes(wq[:, :, B_NOPE:].reshape(B_Q_LORA * B_HEADS, B_ROPE)).reshape(B_Q_LORA, B_HEADS, LANES)],
        axis=2).reshape(B_Q_LORA, B_HEADS * 256).astype(BF16)
    wkv = w_ukv.reshape(B_KV_LORA, B_HEADS, B_NOPE + B_DV)
    wkv_p = jnp.concatenate([wkv[:, :, :B_NOPE].reshape(B_KV_LORA, B_HEADS * B_NOPE),
                             wkv[:, :, B_NOPE:].reshape(B_KV_LORA, B_HEADS * B_DV)], axis=1).astype(BF16)
    return w_main, w_slab, wq_p, wkv_p


def _cd_weights(w_in):
    n = C_HEADS * C_DK
    w_qc = w_in[:, 0:n]
    w_kc = w_in[:, n:2 * n] * (C_DK ** -0.5)
    w_vc = w_in[:, 2 * n:3 * n]
    w_gc = w_in[:, 3 * n:4 * n]
    w_qd = w_in[:, 4 * n:5 * n] * (D_DH ** -0.5 * LOG2E)
    w_kd = w_in[:, 5 * n:6 * n]
    w_vd = w_in[:, 6 * n:7 * n]
    w_fd = w_in[:, 7 * n:7 * n + D_HEADS]
    d = w_in.shape[0]
    w_c = jnp.concatenate([w_qc, w_kc, w_vc, w_gc], axis=1).astype(BF16)
    w_d = jnp.concatenate([w_qd, w_kd, w_vd], axis=1).astype(BF16)
    w_f = jnp.concatenate([w_fd, jnp.zeros((d, LANES - D_HEADS), w_in.dtype)], axis=1).astype(BF16)
    return w_c, w_d, w_f


def _layer_ab(h, rope, w_in, conv_w, a_log, dt_bias, out_norm, q_norm, w_uq, kv_norm, w_ukv, batch, seq):
    cos_b, sin_b, _, _ = rope
    w_main, w_slab, wq_p, wkv_p = _ab_weights(w_in, w_uq, w_ukv)
    proj = _mm(h, w_main, F32)
    slab = _mm(h, w_slab, F32, tn=256)
    gates = _gdn_gates(slab, 1, a_log, dt_bias)
    gt = gates[:, :2 * A_HEADS].T
    t = h.shape[0]
    g_col = gt[:A_HEADS].reshape(A_HEADS, t, 1)
    b_col = gt[A_HEADS:].reshape(A_HEADS, t, 1)
    g_row = gt[:A_HEADS].reshape(A_HEADS, 1, t)
    o_a = _gdn(proj, conv_w, g_col, b_col, g_row, out_norm, batch, seq)
    nq = A_HEADS * (2 * A_DK + A_DV) + A_HEADS * A_DV
    q_b = _mla_q(proj, nq // B_Q_LORA, q_norm, wq_p, cos_b, sin_b)
    k_b, v_b = _mla_kv(proj, (nq + B_Q_LORA) // B_KV_LORA, kv_norm, wkv_p, slab, cos_b, sin_b)
    o_b = _attention(q_b, k_b, _v_blocks_t(v_b, batch, seq, B_HEADS, B_DV), 256, B_DV, B_HEADS, batch, seq, CHUNK)
    return jnp.concatenate([o_a, o_b], axis=1)


def _layer_cd(h, rope, w_in, group_norm, forget_bias, batch, seq):
    _, _, cos_c, sin_c = rope
    w_c, w_d, w_f = _cd_weights(w_in)
    proj_c = _mm(h, w_c, F32)
    proj_d = _mm(h, w_d, BF16)
    slab = _mm(h, w_f, F32, tn=LANES)
    o_c = _retention(proj_c, cos_c, sin_c, group_norm, batch, seq)
    cum = _fox_cum(slab, forget_bias, batch, seq)
    tq = min(ATTN_T, seq)
    cum_t = cum[:, :D_HEADS].reshape(batch, seq, D_HEADS).transpose(0, 2, 1)
    cq = cum_t.reshape(batch, D_HEADS, 1, seq)
    ck = cum_t.reshape(batch, D_HEADS, seq // tq, tq, 1)
    vt = _v_blocks_t(proj_d[:, 2 * D_HEADS * D_DH:], batch, seq, D_HEADS, D_DH)
    o_d = _attention(proj_d, proj_d, vt, D_DH, D_DH, D_HEADS, batch, seq, 1,
                     q_off=0, k_off=D_HEADS * D_DH, bias=(cq, ck))
    return jnp.concatenate([o_c, o_d], axis=1)


def kernel(x, c, positions, ada_w, ada_b, mix_pre_norm, mix_post_norm, ffn_pre_norm, ffn_post_norm, ab_w_in, ab_conv_w, ab_a_log, ab_dt_bias, ab_out_norm, ab_q_norm, ab_w_uq, ab_kv_norm, ab_w_ukv, ab_w_out, cd_w_in, cd_group_norm, cd_forget_bias, cd_w_out, ffn_w_gate, ffn_w_up, ffn_w_down):
    batch, seq, d = x.shape
    depth = ada_w.shape[0]
    t = batch * seq
    rows = 8
    c_pad = jnp.concatenate([c, jnp.zeros((rows - batch, d), c.dtype)], axis=0)
    mod = _ada(c_pad, ada_w, ada_b).reshape(depth, rows, 6, d)
    rope = _rope_tables(positions.astype(F32).reshape(t, 1))
    x2 = x.reshape(t, d)
    h = _prenorm(x2, mod[0], mix_pre_norm[0], seq, 0, 1)
    for layer in range(depth):
        j = layer // 2
        if layer % 2 == 0:
            y_in = _layer_ab(h, rope, ab_w_in[j], ab_conv_w[j], ab_a_log[j], ab_dt_bias[j], ab_out_norm[j],
                             ab_q_norm[j], ab_w_uq[j], ab_kv_norm[j], ab_w_ukv[j], batch, seq)
            w_out = ab_w_out[j]
        else:
            y_in = _layer_cd(h, rope, cd_w_in[j], cd_group_norm[j], cd_forget_bias[j], batch, seq)
            w_out = cd_w_out[j]
        x2, h = _mm_post(y_in, w_out.astype(BF16), x2, mod[layer], mod[layer], mix_post_norm[layer],
                         ffn_pre_norm[layer], seq, 2, 3, 4, True, tk=w_out.shape[0])
        hid = _ffn_up(h, ffn_w_gate[layer].astype(BF16), ffn_w_up[layer].astype(BF16))
        last = layer == depth - 1
        nxt = layer if last else layer + 1
        x2, h = _mm_post(hid, ffn_w_down[layer].astype(BF16), x2, mod[layer], mod[nxt], ffn_post_norm[layer],
                         mix_pre_norm[nxt], seq, 5, 0, 1, not last)
    return x2.reshape(batch, seq, d)
```

```python
import functools

import jax
import jax.numpy as jnp
import numpy as np
from jax import lax
from jax.experimental import pallas as pl
from jax.experimental.pallas import tpu as pltpu

F32 = jnp.float32
BF16 = jnp.bfloat16

D_MODEL = 2048
CHUNK = 64
EPS = 1e-6
ROPE_BASE = 10000.0
NEG_INF = -1e30

A_HEADS, A_DK, A_DV, A_CONV = 8, 128, 128, 4
B_HEADS, B_NOPE, B_ROPE, B_DV, B_Q_LORA, B_KV_LORA = 8, 128, 64, 128, 512, 512
C_HEADS, C_DK, C_DV = 8, 128, 128
D_HEADS, D_DH = 8, 128
LANES = 128
SUBLANES = 8

VMEM_LIMIT_BYTES = 56 * 1024 * 1024

RET_CHUNK = 128


def _cparams(n_axes):
    return pltpu.CompilerParams(dimension_semantics=("arbitrary",) * n_axes,
                                vmem_limit_bytes=VMEM_LIMIT_BYTES)


def _dot(a, b):
    return jnp.dot(a, b, preferred_element_type=F32)


def _dot_nt(a, b):
    return lax.dot_general(a, b, (((1,), (1,)), ((), ())), preferred_element_type=F32)


def _dot_tn(a, b):
    return lax.dot_general(a, b, (((0,), (0,)), ((), ())), preferred_element_type=F32)


def _split_bf16(a):
    hi = a.astype(BF16)
    lo = (a - hi.astype(F32)).astype(BF16)
    return hi, lo


def _dot3(a, b):
    ah, al = _split_bf16(a)
    bh, bl = _split_bf16(b)
    return _dot(ah, bh) + _dot(ah, bl) + _dot(al, bh)


def _sigmoid(x):
    return 1.0 / (1.0 + jnp.exp(-x))


def _silu(x):
    return x * _sigmoid(x)


def _softplus(x):
    return jnp.maximum(x, 0.0) + jnp.log(1.0 + jnp.exp(-jnp.abs(x)))


def _rms(x, w):
    return x * lax.rsqrt(jnp.mean(x * x, axis=-1, keepdims=True) + EPS) * w


def _ada_kernel(c_ref, w_ref, b_ref, o_ref):
    cond = _silu(c_ref[...]).astype(BF16)
    o_ref[...] = _dot(cond, w_ref[...].astype(BF16)) + b_ref[...]


def _ada(c_pad, ada_w, ada_b):
    depth, d, n = ada_w.shape
    rows = c_pad.shape[0]
    tn = 1024
    return pl.pallas_call(
        _ada_kernel,
        grid=(depth, n // tn),
        in_specs=[pl.BlockSpec((rows, d), lambda l, j: (0, 0)),
                  pl.BlockSpec((None, d, tn), lambda l, j: (l, 0, j)),
                  pl.BlockSpec((None, 1, tn), lambda l, j: (l, 0, j))],
        out_specs=pl.BlockSpec((None, rows, tn), lambda l, j: (l, 0, j)),
        out_shape=jax.ShapeDtypeStruct((depth, rows, n), F32),
        compiler_params=_cparams(2),
        name="ada_mod",
    )(c_pad, ada_w, ada_b.reshape(depth, 1, n))


def _rope_kernel(pos_ref, fb_ref, mb_ref, sb_ref, fc_ref, sc_ref, cb_ref, snb_ref, cc_ref, snc_ref):
    pos = pos_ref[...]
    ang_b = pos * fb_ref[...]
    cb_ref[...] = jnp.cos(ang_b) * mb_ref[...]
    snb_ref[...] = jnp.sin(ang_b) * sb_ref[...]
    ang_c = pos * fc_ref[...]
    cc_ref[...] = jnp.cos(ang_c)
    snc_ref[...] = jnp.sin(ang_c) * sc_ref[...]


def _rope_tables(pos_col):
    t = pos_col.shape[0]
    tb = min(1024, t)
    inv_b = ROPE_BASE ** (-jnp.arange(0, B_ROPE, 2, dtype=F32) / B_ROPE)
    inv_c = ROPE_BASE ** (-jnp.arange(0, C_DK, 2, dtype=F32) / C_DK)
    z32 = jnp.zeros((B_ROPE // 2,), F32)
    o32 = jnp.ones((B_ROPE // 2,), F32)
    o64 = jnp.ones((C_DK // 2,), F32)
    fb = jnp.concatenate([inv_b, z32, inv_b, z32])[None, :]
    mb = jnp.concatenate([o32, z32, o32, z32])[None, :]
    sb = jnp.concatenate([-o32, z32, o32, z32])[None, :]
    fc = jnp.concatenate([inv_c, inv_c])[None, :]
    sc = jnp.concatenate([-o64, o64])[None, :]
    row = pl.BlockSpec((1, LANES), lambda i: (0, 0))
    tab = pl.BlockSpec((tb, LANES), lambda i: (i, 0))
    shp = jax.ShapeDtypeStruct((t, LANES), F32)
    return pl.pallas_call(
        _rope_kernel,
        grid=(t // tb,),
        in_specs=[pl.BlockSpec((tb, 1), lambda i: (i, 0)), row, row, row, row, row],
        out_specs=[tab, tab, tab, tab],
        out_shape=[shp, shp, shp, shp],
        compiler_params=_cparams(1),
        name="rope_tables",
    )(pos_col, fb, mb, sb, fc, sc)


def _prenorm_kernel(x_ref, mod_ref, w_ref, h_ref, *, shift_idx, scale_idx):
    y = _rms(x_ref[...], w_ref[...])
    h = y * (1.0 + mod_ref[scale_idx:scale_idx + 1, :]) + mod_ref[shift_idx:shift_idx + 1, :]
    h_ref[...] = h.astype(h_ref.dtype)


def _prenorm(x2, mod_l, w, seq, shift_idx, scale_idx):
    t, d = x2.shape
    tm = min(512, seq)
    bpb = seq // tm
    return pl.pallas_call(
        functools.partial(_prenorm_kernel, shift_idx=shift_idx, scale_idx=scale_idx),
        grid=(t // tm,),
        in_specs=[pl.BlockSpec((tm, d), lambda i: (i, 0)),
                  pl.BlockSpec((None, 6, d), lambda i: (i // bpb, 0, 0)),
                  pl.BlockSpec((1, d), lambda i: (0, 0))],
        out_specs=pl.BlockSpec((tm, d), lambda i: (i, 0)),
        out_shape=jax.ShapeDtypeStruct((t, d), BF16),
        compiler_params=_cparams(1),
        name="prenorm",
    )(x2, mod_l, w.reshape(1, d))


def _mm_kernel(*refs, cast_w, scaled):
    a_ref, w_ref = refs[0], refs[1]
    s_ref = refs[2] if scaled else None
    o_ref = refs[3] if scaled else refs[2]
    if cast_w:
        wb_ref = refs[-1]

        @pl.when(pl.program_id(1) == 0)
        def _():
            wb_ref[...] = w_ref[...].astype(BF16)

        w = wb_ref[...]
    else:
        w = w_ref[...]
    y = _dot(a_ref[...], w)
    if scaled:
        y = y * s_ref[...]
    o_ref[...] = y.astype(o_ref.dtype)


def _mm(a, w, out_dtype, layer=None, col0=0, ncols=None, colscale=None, tm=512, tn=1024):
    m, k = a.shape
    n = w.shape[-1] if ncols is None else ncols
    tm = min(tm, m)
    tn = min(tn, n)
    jb0 = col0 // tn
    cast_w = w.dtype != BF16
    if layer is None:
        wspec = pl.BlockSpec((k, tn), lambda j, i: (0, jb0 + j))
    else:
        wspec = pl.BlockSpec((None, k, tn), lambda j, i: (layer, 0, jb0 + j))
    in_specs = [pl.BlockSpec((tm, k), lambda j, i: (i, 0)), wspec]
    args = [a, w]
    if colscale is not None:
        in_specs.append(pl.BlockSpec((1, tn), lambda j, i: (0, j)))
        args.append(colscale.reshape(1, n).astype(F32))
    return pl.pallas_call(
        functools.partial(_mm_kernel, cast_w=cast_w, scaled=colscale is not None),
        grid=(n // tn, m // tm),
        in_specs=in_specs,
        out_specs=pl.BlockSpec((tm, tn), lambda j, i: (i, j)),
        out_shape=jax.ShapeDtypeStruct((m, n), out_dtype),
        scratch_shapes=[pltpu.VMEM((k, tn), BF16)] if cast_w else [],
        compiler_params=_cparams(2),
        name="matmul",
    )(*args)


def _mm_post_kernel(*refs, n_in, gate_idx, nshift_idx, nscale_idx, emit_h):
    a_refs = refs[:n_in]
    w_refs = refs[n_in:2 * n_in]
    x_ref, mod_ref, modn_ref, pw_ref, nw_ref = refs[2 * n_in:2 * n_in + 5]
    outs = refs[2 * n_in + 5:]
    y = _dot(a_refs[0][...], w_refs[0][...])
    for a_ref, w_ref in zip(a_refs[1:], w_refs[1:]):
        y = y + _dot(a_ref[...], w_ref[...])
    xn = x_ref[...] + mod_ref[gate_idx:gate_idx + 1, :] * _rms(y, pw_ref[...])
    outs[0][...] = xn
    if emit_h:
        hn = _rms(xn, nw_ref[...])
        hn = hn * (1.0 + modn_ref[nscale_idx:nscale_idx + 1, :]) + modn_ref[nshift_idx:nshift_idx + 1, :]
        outs[1][...] = hn.astype(outs[1].dtype)


def _mm_post(a_list, w_list, x2, mod_l, modn_l, post_w, next_w, seq, gate_idx, nshift_idx, nscale_idx, emit_h,
             tm):
    m = a_list[0].shape[0]
    d = w_list[0].shape[1]
    tm = min(tm, seq)
    bpb = seq // tm
    row = pl.BlockSpec((1, d), lambda i: (0, 0))
    modspec = pl.BlockSpec((None, 6, d), lambda i: (i // bpb, 0, 0))
    xspec = pl.BlockSpec((tm, d), lambda i: (i, 0))
    a_specs = [pl.BlockSpec((tm, a.shape[1]), lambda i: (i, 0)) for a in a_list]
    w_specs = [pl.BlockSpec(w.shape, lambda i: (0, 0), pipeline_mode=pl.Buffered(1)) for w in w_list]
    out_specs = [xspec]
    out_shape = [jax.ShapeDtypeStruct((m, d), F32)]
    if emit_h:
        out_specs.append(xspec)
        out_shape.append(jax.ShapeDtypeStruct((m, d), BF16))
    res = pl.pallas_call(
        functools.partial(_mm_post_kernel, n_in=len(a_list), gate_idx=gate_idx, nshift_idx=nshift_idx,
                          nscale_idx=nscale_idx, emit_h=emit_h),
        grid=(m // tm,),
        in_specs=a_specs + w_specs + [xspec, modspec, modspec, row, row],
        out_specs=out_specs,
        out_shape=out_shape,
        compiler_params=_cparams(1),
        name="matmul_post",
    )(*a_list, *w_list, x2, mod_l, modn_l, post_w.reshape(1, d), next_w.reshape(1, d))
    return res if emit_h else (res[0], None)


def _ffn_up_kernel(a_ref, wg_ref, wu_ref, o_ref, wgb_ref, wub_ref):
    @pl.when(pl.program_id(1) == 0)
    def _():
        wgb_ref[...] = wg_ref[...].astype(BF16)
        wub_ref[...] = wu_ref[...].astype(BF16)

    a = a_ref[...]
    g = _dot(a, wgb_ref[...])
    u = _dot(a, wub_ref[...])
    o_ref[...] = (_silu(g) * u).astype(o_ref.dtype)


def _ffn_up(a, wg, wu, layer, tm=512, tn=512):
    m, k = a.shape
    n = wg.shape[-1]
    tm = min(tm, m)
    wspec = pl.BlockSpec((None, k, tn), lambda j, i: (layer, 0, j))
    return pl.pallas_call(
        _ffn_up_kernel,
        grid=(n // tn, m // tm),
        in_specs=[pl.BlockSpec((tm, k), lambda j, i: (i, 0)), wspec, wspec],
        out_specs=pl.BlockSpec((tm, tn), lambda j, i: (i, j)),
        out_shape=jax.ShapeDtypeStruct((m, n), BF16),
        scratch_shapes=[pltpu.VMEM((k, tn), BF16), pltpu.VMEM((k, tn), BF16)],
        compiler_params=_cparams(2),
        name="ffn_up",
    )(a, wg, wu)


def _gdn_gate_kernel(s_ref, alog_ref, dt_ref, o_ref):
    s = s_ref[...]
    g = -jnp.exp(alog_ref[...]) * _softplus(s + dt_ref[...])
    beta = _sigmoid(s)
    lane = lax.broadcasted_iota(jnp.int32, s.shape, 1)
    o_ref[...] = jnp.where(lane < A_HEADS, g, beta)


def _gdn_gates(slab_src, col_block, a_log, dt_bias):
    t = slab_src.shape[0]
    tb = min(1024, t)
    pad = jnp.zeros((LANES - A_HEADS,), F32)
    alog_row = jnp.concatenate([a_log.astype(F32), pad])[None, :]
    dt_row = jnp.concatenate([dt_bias.astype(F32), pad])[None, :]
    row = pl.BlockSpec((1, LANES), lambda i: (0, 0))
    return pl.pallas_call(
        _gdn_gate_kernel,
        grid=(t // tb,),
        in_specs=[pl.BlockSpec((tb, LANES), lambda i: (i, col_block)), row, row],
        out_specs=pl.BlockSpec((tb, LANES), lambda i: (i, 0)),
        out_shape=jax.ShapeDtypeStruct((t, LANES), F32),
        compiler_params=_cparams(1),
        name="gdn_gates",
    )(slab_src, alog_row, dt_row)


GDN_HG = 4
GDN_TB = 256


def _unit_lower_inverses(n_mats):
    c = n_mats[0].shape[0]
    r = lax.broadcasted_iota(jnp.int32, (c, c), 0)
    col = lax.broadcasted_iota(jnp.int32, (c, c), 1)
    eye = jnp.where(r == col, 1.0, 0.0)
    ps = [eye - n for n in n_mats]
    ms = [_dot3(n, n) for n in n_mats]
    power = 2
    while True:
        ps = [p + _dot3(p, m) for p, m in zip(ps, ms)]
        power *= 2
        if power >= c:
            break
        ms = [_dot3(m, m) for m in ms]
    return ps


def _gdn_kernel(q_ref, k_ref, v_ref, z_ref, cwq_ref, cwk_ref, cwv_ref, gcol_ref, bcol_ref, grow_ref, onw_ref,
                o_ref, xq_ref, xk_ref, xv_ref, st_ref, *, tb, hg):
    j = pl.program_id(2)

    @pl.when(j == 0)
    def _():
        st_ref[...] = jnp.zeros_like(st_ref)
        for h_ref in (xq_ref, xk_ref, xv_ref):
            h_ref[...] = jnp.zeros_like(h_ref)

    def conv_silu(src_ref, hist_ref, cw_ref):
        x = src_ref[...]
        prev = hist_ref[...]
        hrows = prev.shape[0]
        row = lax.broadcasted_iota(jnp.int32, prev.shape, 0)
        acc = x * cw_ref[A_CONV - 1:A_CONV, :]
        for s in range(1, A_CONV):
            xs = pltpu.roll(x, s, 0)
            top = jnp.where(row < s, pltpu.roll(prev, s, 0), xs[0:hrows])
            shifted = jnp.concatenate([top, xs[hrows:]], axis=0)
            acc = acc + shifted * cw_ref[A_CONV - 1 - s:A_CONV - s, :]
        hist_ref[...] = x[tb - hrows:tb]
        return _silu(acc)

    q_all = conv_silu(q_ref, xq_ref, cwq_ref)
    k_all = conv_silu(k_ref, xk_ref, cwk_ref)
    v_all = conv_silu(v_ref, xv_ref, cwv_ref)

    c = CHUNK
    r = lax.broadcasted_iota(jnp.int32, (c, c), 0)
    col = lax.broadcasted_iota(jnp.int32, (c, c), 1)
    tri = r >= col
    strict = r > col

    nch = tb // c
    items = [(hh, ci) for hh in range(hg) for ci in range(nch)]
    qn, kn, vn = [], [], []
    for hh in range(hg):
        lanes = slice(hh * A_DK, (hh + 1) * A_DK)
        qh = q_all[:, lanes]
        kh = k_all[:, lanes]
        qn.append(qh * lax.rsqrt(jnp.sum(qh * qh, axis=-1, keepdims=True) + EPS) * (A_DK ** -0.5))
        kn.append(kh * lax.rsqrt(jnp.sum(kh * kh, axis=-1, keepdims=True) + EPS))
        vn.append(v_all[:, lanes])

    def rows_of(ci):
        return slice(ci * c, (ci + 1) * c)

    qc = [qn[hh][rows_of(ci)] for hh, ci in items]
    kc = [kn[hh][rows_of(ci)] for hh, ci in items]
    vc = [vn[hh][rows_of(ci)] for hh, ci in items]
    b_col = [bcol_ref[hh, rows_of(ci), :] for hh, ci in items]
    gc_col = [jnp.sum(jnp.where(tri, grow_ref[hh][:, rows_of(ci)], 0.0), axis=1, keepdims=True)
              for hh, ci in items]
    gc_row = [jnp.sum(jnp.where(r <= col, gcol_ref[hh, rows_of(ci), :], 0.0), axis=0, keepdims=True)
              for hh, ci in items]
    decay = [jnp.where(tri, jnp.exp(jnp.minimum(a - b, 0.0)), 0.0) for a, b in zip(gc_col, gc_row)]
    e_gc = [jnp.exp(a) for a in gc_col]
    g_last = [a[c - 1:c, :] for a in gc_col]
    kb = [x.astype(BF16) for x in kc]
    kk = [_dot_nt(x, x) for x in kb]
    qk = [_dot_nt(x.astype(BF16), y) for x, y in zip(qc, kb)]
    n_mat = [jnp.where(strict, b * x * d, 0.0) for b, x, d in zip(b_col, kk, decay)]
    qk = [jnp.where(tri, x * d, 0.0).astype(BF16) for x, d in zip(qk, decay)]
    t_inv = _unit_lower_inverses(n_mat)
    wu = [_dot3(t, jnp.concatenate([k * (b * e), v * b], axis=1)).astype(BF16)
          for t, k, v, b, e in zip(t_inv, kc, vc, b_col, e_gc)]
    kdb = [(k * jnp.exp(gl - g)).astype(BF16) for k, gl, g in zip(kc, g_last, gc_col)]
    mb = [_dot_tn(kd, x) for kd, x in zip(kdb, wu)]
    m_c = [x[:, :A_DK].astype(BF16) for x in mb]
    b_c = [x[:, A_DK:] for x in mb]
    qwu = [_dot(a, x) for a, x in zip(qk, wu)]
    q_eff = [(q * e - x[:, :A_DK]).astype(BF16) for q, e, x in zip(qc, e_gc, qwu)]
    qku = [x[:, A_DK:] for x in qwu]
    e_gl = [jnp.exp(x) for x in g_last]
    states = [st_ref[hh] for hh in range(hg)]
    o_c = [None] * len(items)
    for ci in range(nch):
        idx = [hh * nch + ci for hh in range(hg)]
        sb = [s.astype(BF16) for s in states]
        for hh in range(hg):
            o_c[idx[hh]] = _dot(q_eff[idx[hh]], sb[hh]) + qku[idx[hh]]
        states = [states[hh] * e_gl[idx[hh]] + b_c[idx[hh]] - _dot(m_c[idx[hh]], sb[hh]) for hh in range(hg)]
    for hh in range(hg):
        st_ref[hh] = states[hh]
    for i, (hh, ci) in enumerate(items):
        lanes = slice(hh * A_DK, (hh + 1) * A_DK)
        zc = z_ref[rows_of(ci), lanes]
        o_ref[rows_of(ci), lanes] = (_rms(o_c[i], onw_ref[...]) * _silu(zc)).astype(o_ref.dtype)


def _gdn(proj, conv_w, g_col, b_col, g_row, out_norm, batch, seq):
    t = proj.shape[0]
    tb = min(GDN_TB, seq)
    hg = GDN_HG
    w = hg * A_DK
    nj = seq // tb
    ng = A_HEADS // hg
    kq = A_HEADS * A_DK
    off_k = kq // w
    off_v = 2 * kq // w
    off_z = 3 * kq // w

    def colspec(off):
        return pl.BlockSpec((tb, w), lambda b, g, j: (b * nj + j, off + g))

    def cwspec(off):
        return pl.BlockSpec((A_CONV, w), lambda b, g, j: (0, off + g))

    colv = pl.BlockSpec((hg, tb, 1), lambda b, g, j: (g, b * nj + j, 0))
    rowv = pl.BlockSpec((hg, 1, tb), lambda b, g, j: (g, 0, b * nj + j))
    return pl.pallas_call(
        functools.partial(_gdn_kernel, tb=tb, hg=hg),
        grid=(batch, ng, nj),
        in_specs=[colspec(0), colspec(off_k), colspec(off_v), colspec(off_z),
                  cwspec(0), cwspec(off_k), cwspec(off_v), colv, colv, rowv,
                  pl.BlockSpec((1, A_DV), lambda b, g, j: (0, 0))],
        out_specs=pl.BlockSpec((tb, w), lambda b, g, j: (b * nj + j, g)),
        out_shape=jax.ShapeDtypeStruct((t, A_HEADS * A_DV), BF16),
        scratch_shapes=[pltpu.VMEM((SUBLANES, w), F32), pltpu.VMEM((SUBLANES, w), F32),
                        pltpu.VMEM((SUBLANES, w), F32), pltpu.VMEM((hg, A_DK, A_DV), F32)],
        compiler_params=_cparams(3),
        name="gated_deltanet",
    )(proj, proj, proj, proj, conv_w, conv_w, conv_w, g_col, b_col, g_row, out_norm.reshape(1, A_DV))


def _rope_lanes(x, cos, sin_signed):
    return x * cos + pltpu.roll(x, LANES // 2, 1) * sin_signed


def _mla_q_kernel(cq_ref, nw_ref, w_ref, cos_ref, sin_ref, o_ref):
    n = _rms(cq_ref[...], nw_ref[...]).astype(BF16)
    cos = cos_ref[...]
    sin = sin_ref[...]
    for h in range(B_HEADS):
        qh = _dot(n, w_ref[:, h * 256:(h + 1) * 256])
        o_ref[:, h * 256:h * 256 + LANES] = qh[:, :LANES].astype(o_ref.dtype)
        o_ref[:, h * 256 + LANES:(h + 1) * 256] = _rope_lanes(qh[:, LANES:], cos, sin).astype(o_ref.dtype)


def _mla_q(proj, col_block, q_norm, w_uq_p, cos_b, sin_b):
    t = proj.shape[0]
    tm = min(512, t)
    n = w_uq_p.shape[1]
    return pl.pallas_call(
        _mla_q_kernel,
        grid=(t // tm,),
        in_specs=[pl.BlockSpec((tm, B_Q_LORA), lambda i: (i, col_block)),
                  pl.BlockSpec((1, B_Q_LORA), lambda i: (0, 0)),
                  pl.BlockSpec((B_Q_LORA, n), lambda i: (0, 0)),
                  pl.BlockSpec((tm, LANES), lambda i: (i, 0)),
                  pl.BlockSpec((tm, LANES), lambda i: (i, 0))],
        out_specs=pl.BlockSpec((tm, n), lambda i: (i, 0)),
        out_shape=jax.ShapeDtypeStruct((t, n), BF16),
        compiler_params=_cparams(1),
        name="mla_q_up",
    )(proj, q_norm.reshape(1, B_Q_LORA), w_uq_p, cos_b, sin_b)


def _mla_kv_kernel(ckv_ref, nw_ref, w_ref, kpe_ref, cos_ref, sin_ref, k_ref, v_ref):
    n = _rms(ckv_ref[...], nw_ref[...]).astype(BF16)
    kpe = _rope_lanes(kpe_ref[...], cos_ref[...], sin_ref[...]).astype(k_ref.dtype)
    nk = B_HEADS * B_NOPE
    kn = _dot(n, w_ref[:, :nk])
    for h in range(B_HEADS):
        k_ref[:, h * 256:h * 256 + LANES] = kn[:, h * LANES:(h + 1) * LANES].astype(k_ref.dtype)
        k_ref[:, h * 256 + LANES:(h + 1) * 256] = kpe
    v_ref[...] = _dot(n, w_ref[:, nk:]).astype(v_ref.dtype)


def _mla_kv(proj, col_block, kv_norm, w_ukv_p, kpe_block, cos_b, sin_b):
    t = proj.shape[0]
    tm = min(512, t)
    n = w_ukv_p.shape[1]
    tab = pl.BlockSpec((tm, LANES), lambda i: (i, 0))
    kpe_spec = pl.BlockSpec((tm, LANES), lambda i: (i, kpe_block))
    return pl.pallas_call(
        _mla_kv_kernel,
        grid=(t // tm,),
        in_specs=[pl.BlockSpec((tm, B_KV_LORA), lambda i: (i, col_block)),
                  pl.BlockSpec((1, B_KV_LORA), lambda i: (0, 0)),
                  pl.BlockSpec((B_KV_LORA, n), lambda i: (0, 0)),
                  kpe_spec, tab, tab],
        out_specs=[pl.BlockSpec((tm, B_HEADS * 256), lambda i: (i, 0)),
                   pl.BlockSpec((tm, B_HEADS * B_DV), lambda i: (i, 0))],
        out_shape=[jax.ShapeDtypeStruct((t, B_HEADS * 256), BF16),
                   jax.ShapeDtypeStruct((t, B_HEADS * B_DV), BF16)],
        compiler_params=_cparams(1),
        name="mla_kv_up",
    )(proj, kv_norm.reshape(1, B_KV_LORA), w_ukv_p, proj, cos_b, sin_b)


ATTN_T = 512


def _attn_kernel(*refs, tq, nq, dk, dv, hg, mask_gran, has_bias):
    if has_bias:
        q_ref, k_ref, v_ref, cq_ref, ck_ref, o_ref, m_ref, acc_ref, vt_ref = refs
    else:
        q_ref, k_ref, v_ref, o_ref, m_ref, acc_ref, vt_ref = refs
    qi = pl.program_id(2)

    @pl.when(qi == 0)
    def _():
        for hh in range(hg):
            for jb in range(nq):
                vblk = v_ref[jb * tq:(jb + 1) * tq, hh * dv:(hh + 1) * dv]
                vt_ref[hh, jb, 0:dv, :] = vblk.astype(F32).T.astype(BF16)
                vt_ref[hh, jb, dv:, :] = jnp.ones((ATTN_ONES, tq), BF16)

    m_ref[...] = jnp.full(m_ref.shape, NEG_INF, F32)
    acc_ref[...] = jnp.zeros_like(acc_ref)
    heads = range(hg)

    def step(jb, masked):
        start = pl.multiple_of(jb * tq, tq)
        s = [_dot_nt(k_ref[pl.ds(start, tq), hh * dk:(hh + 1) * dk], q_ref[:, hh * dk:(hh + 1) * dk])
             for hh in heads]
        if has_bias:
            s = [s[hh] - ck_ref[hh, jb] for hh in heads]
        if masked:
            r = lax.broadcasted_iota(jnp.int32, (tq, tq), 0)
            c = lax.broadcasted_iota(jnp.int32, (tq, tq), 1)
            if mask_gran > 1:
                shift = int(np.log2(mask_gran))
                r = lax.shift_right_logical(r, shift)
                c = lax.shift_right_logical(c, shift)
            keep = r <= c
            s = [jnp.where(keep, s[hh], NEG_INF) for hh in heads]
        m_prev = [m_ref[hh] for hh in heads]
        m_cur = [jnp.max(s[hh], axis=0, keepdims=True) for hh in heads]
        if has_bias:
            m_cur = [m_cur[hh] + cq_ref[hh] for hh in heads]
        m_new = [jnp.maximum(m_prev[hh], m_cur[hh]) for hh in heads]
        sub = [m_new[hh] - cq_ref[hh] for hh in heads] if has_bias else m_new
        p = [jnp.exp2(s[hh] - sub[hh]).astype(BF16) for hh in heads]
        alpha = [jnp.exp2(m_prev[hh] - m_new[hh]) for hh in heads]
        for hh in heads:
            m_ref[hh] = m_new[hh]
        pv = [_dot(vt_ref[hh, jb], p[hh]) for hh in heads]
        for hh in heads:
            acc_ref[hh] = alpha[hh] * acc_ref[hh] + pv[hh]

    def body(jb, carry):
        step(jb, False)
        return carry

    lax.fori_loop(0, qi, body, 0)
    step(qi, True)
    for hh in heads:
        acc = acc_ref[hh]
        o_ref[:, hh * dv:(hh + 1) * dv] = (acc[:dv] / acc[dv:dv + 1]).T.astype(o_ref.dtype)


ATTN_HG = 4
ATTN_ONES = 16
LOG2E = 1.4426950408889634


def _attention(q, k, v, dk, dv, heads, batch, seq, mask_gran, q_off=0, k_off=0, v_off=0, bias=None):
    t = q.shape[0]
    tq = min(ATTN_T, seq)
    nq = seq // tq
    hg = ATTN_HG
    qb, kb, vb = q_off // (hg * dk), k_off // (hg * dk), v_off // (hg * dv)
    dvx = dv + ATTN_ONES
    in_specs = [pl.BlockSpec((tq, hg * dk), lambda b, g, i: (b * nq + i, qb + g)),
                pl.BlockSpec((seq, hg * dk), lambda b, g, i: (b, kb + g)),
                pl.BlockSpec((seq, hg * dv), lambda b, g, i: (b, vb + g))]
    args = [q, k, v]
    if bias is not None:
        cq, ck = bias
        in_specs += [pl.BlockSpec((None, hg, 1, tq), lambda b, g, i: (b, g, 0, i)),
                     pl.BlockSpec((None, hg, nq, tq, 1), lambda b, g, i: (b, g, 0, 0, 0))]
        args += [cq, ck]
    return pl.pallas_call(
        functools.partial(_attn_kernel, tq=tq, nq=nq, dk=dk, dv=dv, hg=hg, mask_gran=mask_gran,
                          has_bias=bias is not None),
        grid=(batch, heads // hg, nq),
        in_specs=in_specs,
        out_specs=pl.BlockSpec((tq, hg * dv), lambda b, g, i: (b * nq + i, g)),
        out_shape=jax.ShapeDtypeStruct((t, heads * dv), BF16),
        scratch_shapes=[pltpu.VMEM((hg, 1, tq), F32), pltpu.VMEM((hg, dvx, tq), F32),
                        pltpu.VMEM((hg, nq, dvx, tq), BF16)],
        compiler_params=_cparams(3),
        name="flash_attention",
    )(*args)


RET_HG = 2
RET_TB = 512


def _ret_kernel(q_ref, k_ref, v_ref, g_ref, cos_ref, sin_ref, dec_ref, xi_ref, zeta_ref, gch_ref, gnw_ref,
                o_ref, st_ref, *, tb, hg, c):
    j = pl.program_id(2)

    @pl.when(j == 0)
    def _():
        st_ref[...] = jnp.zeros_like(st_ref)

    cos = cos_ref[...]
    sin = sin_ref[...]
    for hh in range(hg):
        lanes = slice(hh * C_DK, (hh + 1) * C_DK)
        qh = _rope_lanes(q_ref[:, lanes], cos, sin)
        kh = _rope_lanes(k_ref[:, lanes], cos, sin)
        state = st_ref[hh]
        for ci in range(tb // c):
            rows = slice(ci * c, (ci + 1) * c)
            qb = qh[rows].astype(BF16)
            kc = kh[rows]
            vb = v_ref[rows, lanes].astype(BF16)
            scores = _dot_nt(qb, kc.astype(BF16)) * dec_ref[hh]
            inner = _dot(scores.astype(BF16), vb)
            cross = _dot(qb, state.astype(BF16)) * xi_ref[hh]
            state = state * gch_ref[hh] + _dot_tn((kc * zeta_ref[hh]).astype(BF16), vb)
            o_c = inner + cross
            o_ref[rows, lanes] = (_rms(o_c, gnw_ref[hh]) * _silu(g_ref[rows, lanes])).astype(o_ref.dtype)
        st_ref[hh] = state


def _retention(proj, cos_c, sin_c, group_norm, batch, seq):
    t = proj.shape[0]
    tb = min(RET_TB, seq)
    c = min(RET_CHUNK, seq)
    hg = RET_HG
    w = hg * C_DK
    nj = seq // tb
    ng = C_HEADS // hg
    nh = C_HEADS * C_DK // w

    log_gamma = jnp.log1p(-jnp.exp2(-5.0 - jnp.arange(C_HEADS, dtype=F32)))
    idx = jnp.arange(c, dtype=F32)
    rel = idx[:, None] - idx[None, :]
    dec = jnp.where(rel >= 0, jnp.exp(log_gamma[:, None, None] * jnp.maximum(rel, 0.0)), 0.0)
    xi = jnp.exp(log_gamma[:, None] * (idx + 1.0))
    zeta = jnp.exp(log_gamma[:, None] * (c - 1.0 - idx))
    gch = jnp.exp(log_gamma * c)
    xi_b = jnp.broadcast_to(xi[:, :, None], (C_HEADS, c, C_DV))
    zeta_b = jnp.broadcast_to(zeta[:, :, None], (C_HEADS, c, C_DK))
    gch_b = jnp.broadcast_to(gch[:, None, None], (C_HEADS, 1, C_DV))

    def colspec(off):
        return pl.BlockSpec((tb, w), lambda b, g, j: (b * nj + j, off + g))

    tab = pl.BlockSpec((tb, LANES), lambda b, g, j: (b * nj + j, 0))

    def hspec(r, cdim):
        return pl.BlockSpec((hg, r, cdim), lambda b, g, j: (g, 0, 0))

    return pl.pallas_call(
        functools.partial(_ret_kernel, tb=tb, hg=hg, c=c),
        grid=(batch, ng, nj),
        in_specs=[colspec(0), colspec(nh), colspec(2 * nh), colspec(3 * nh), tab, tab,
                  hspec(c, c), hspec(c, C_DV), hspec(c, C_DK), hspec(1, C_DV), hspec(1, C_DV)],
        out_specs=pl.BlockSpec((tb, w), lambda b, g, j: (b * nj + j, g)),
        out_shape=jax.ShapeDtypeStruct((t, C_HEADS * C_DV), BF16),
        scratch_shapes=[pltpu.VMEM((hg, C_DK, C_DV), F32)],
        compiler_params=_cparams(3),
        name="retention",
    )(proj, proj, proj, proj, cos_c, sin_c, dec, xi_b, zeta_b, gch_b, group_norm.reshape(C_HEADS, 1, C_DV))


def _fox_cum_kernel(s_ref, b_ref, o_ref, *, seq):
    x = s_ref[...] + b_ref[...]
    acc = -_softplus(-x)
    row = lax.broadcasted_iota(jnp.int32, acc.shape, 0)
    shift = 1
    while shift < seq:
        acc = acc + jnp.where(row >= shift, pltpu.roll(acc, shift, 0), 0.0)
        shift *= 2
    o_ref[...] = acc * LOG2E


def _fox_cum(slab, forget_bias, batch, seq):
    t = slab.shape[0]
    pad = jnp.zeros((LANES - D_HEADS,), F32)
    b_row = jnp.concatenate([forget_bias.astype(F32), pad])[None, :]
    return pl.pallas_call(
        functools.partial(_fox_cum_kernel, seq=seq),
        grid=(batch,),
        in_specs=[pl.BlockSpec((seq, LANES), lambda b: (b, 0)),
                  pl.BlockSpec((1, LANES), lambda b: (0, 0))],
        out_specs=pl.BlockSpec((seq, LANES), lambda b: (b, 0)),
        out_shape=jax.ShapeDtypeStruct((t, LANES), F32),
        compiler_params=_cparams(1),
        name="fox_cumsum",
    )(slab, b_row)


def _pe_lanes(w_pe):
    k = w_pe.shape[0]
    z = jnp.zeros((k, B_ROPE // 2), w_pe.dtype)
    return jnp.concatenate([w_pe[:, :B_ROPE // 2], z, w_pe[:, B_ROPE // 2:], z], axis=1)


AB_MAIN = A_HEADS * (2 * A_DK + A_DV) + A_HEADS * A_DV


def _ab_weights(w_in, w_uq, w_ukv):
    o = AB_MAIN
    w_a = w_in[:, o:o + A_HEADS]; o += A_HEADS
    w_b = w_in[:, o:o + A_HEADS]; o += A_HEADS
    w_cq = w_in[:, o:o + B_Q_LORA]; o += B_Q_LORA
    w_ckv = w_in[:, o:o + B_KV_LORA]; o += B_KV_LORA
    w_kpe = w_in[:, o:o + B_ROPE]
    d = w_in.shape[0]
    w_tail = jnp.concatenate([w_cq, w_ckv, _pe_lanes(w_kpe), w_a, w_b,
                              jnp.zeros((d, LANES - 2 * A_HEADS), w_in.dtype)], axis=1).astype(BF16)
    scale = (B_NOPE + B_ROPE) ** -0.5 * LOG2E
    wq = (w_uq * scale).reshape(B_Q_LORA, B_HEADS, B_NOPE + B_ROPE)
    wq_p = jnp.concatenate(
        [wq[:, :, :B_NOPE],
         _pe_lanes(wq[:, :, B_NOPE:].reshape(B_Q_LORA * B_HEADS, B_ROPE)).reshape(B_Q_LORA, B_HEADS, LANES)],
        axis=2).reshape(B_Q_LORA, B_HEADS * 256).astype(BF16)
    wkv = w_ukv.reshape(B_KV_LORA, B_HEADS, B_NOPE + B_DV)
    wkv_p = jnp.concatenate([wkv[:, :, :B_NOPE].reshape(B_KV_LORA, B_HEADS * B_NOPE),
                             wkv[:, :, B_NOPE:].reshape(B_KV_LORA, B_HEADS * B_DV)], axis=1).astype(BF16)
    return w_tail, wq_p, wkv_p


def _layer_ab(h, rope, w_in_all, j, conv_w, a_log, dt_bias, out_norm, q_norm, w_uq, kv_norm, w_ukv, batch, seq):
    cos_b, sin_b, _, _ = rope
    w_tail, wq_p, wkv_p = _ab_weights(w_in_all[j], w_uq, w_ukv)
    proj = _mm(h, w_in_all, F32, layer=j, ncols=AB_MAIN)
    tail = _mm(h, w_tail, F32, tn=w_tail.shape[1])
    cq_block, ckv_block = 0, 1
    kpe_block = (B_Q_LORA + B_KV_LORA) // LANES
    gates = _gdn_gates(tail, kpe_block + 1, a_log, dt_bias)
    gt = gates[:, :2 * A_HEADS].T
    t = h.shape[0]
    g_col = gt[:A_HEADS].reshape(A_HEADS, t, 1)
    b_col = gt[A_HEADS:].reshape(A_HEADS, t, 1)
    g_row = gt[:A_HEADS].reshape(A_HEADS, 1, t)
    o_a = _gdn(proj, conv_w, g_col, b_col, g_row, out_norm, batch, seq)
    q_b = _mla_q(tail, cq_block, q_norm, wq_p, cos_b, sin_b)
    k_b, v_b = _mla_kv(tail, ckv_block, kv_norm, wkv_p, kpe_block, cos_b, sin_b)
    o_b = _attention(q_b, k_b, v_b, 256, B_DV, B_HEADS, batch, seq, CHUNK)
    return [o_a, o_b]


def _layer_cd(h, rope, w_in_all, j, group_norm, forget_bias, batch, seq):
    _, _, cos_c, sin_c = rope
    n = C_HEADS * C_DK
    ones = jnp.ones((n,), F32)
    scale_c = jnp.concatenate([ones, ones * (C_DK ** -0.5), ones, ones])
    scale_d = jnp.concatenate([ones * (D_DH ** -0.5 * LOG2E), ones, ones])
    proj_c = _mm(h, w_in_all, F32, layer=j, ncols=4 * n, colscale=scale_c)
    proj_d = _mm(h, w_in_all, BF16, layer=j, col0=4 * n, ncols=3 * n, colscale=scale_d)
    w_f = jnp.concatenate([w_in_all[j][:, 7 * n:7 * n + D_HEADS],
                           jnp.zeros((w_in_all.shape[1], LANES - D_HEADS), w_in_all.dtype)], axis=1).astype(BF16)
    slab = _mm(h, w_f, F32, tn=LANES)
    o_c = _retention(proj_c, cos_c, sin_c, group_norm, batch, seq)
    cum = _fox_cum(slab, forget_bias, batch, seq)
    tq = min(ATTN_T, seq)
    cum_t = cum[:, :D_HEADS].reshape(batch, seq, D_HEADS).transpose(0, 2, 1)
    cq = cum_t.reshape(batch, D_HEADS, 1, seq)
    ck = cum_t.reshape(batch, D_HEADS, seq // tq, tq, 1)
    o_d = _attention(proj_d, proj_d, proj_d, D_DH, D_DH, D_HEADS, batch, seq, 1,
                     q_off=0, k_off=D_HEADS * D_DH, v_off=2 * D_HEADS * D_DH, bias=(cq, ck))
    return [o_c, o_d]


def kernel(x, c, positions, ada_w, ada_b, mix_pre_norm, mix_post_norm, ffn_pre_norm, ffn_post_norm, ab_w_in, ab_conv_w, ab_a_log, ab_dt_bias, ab_out_norm, ab_q_norm, ab_w_uq, ab_kv_norm, ab_w_ukv, ab_w_out, cd_w_in, cd_group_norm, cd_forget_bias, cd_w_out, ffn_w_gate, ffn_w_up, ffn_w_down):
    batch, seq, d = x.shape
    depth = ada_w.shape[0]
    t = batch * seq
    rows = 8
    c_pad = jnp.concatenate([c, jnp.zeros((rows - batch, d), c.dtype)], axis=0)
    mod = _ada(c_pad, ada_w, ada_b).reshape(depth, rows, 6, d)
    rope = _rope_tables(positions.astype(F32).reshape(t, 1))
    x2 = x.reshape(t, d)
    h = _prenorm(x2, mod[0], mix_pre_norm[0], seq, 0, 1)
    for layer in range(depth):
        j = layer // 2
        if layer % 2 == 0:
            y_in = _layer_ab(h, rope, ab_w_in, j, ab_conv_w[j], ab_a_log[j], ab_dt_bias[j], ab_out_norm[j],
                             ab_q_norm[j], ab_w_uq[j], ab_kv_norm[j], ab_w_ukv[j], batch, seq)
            w_out = ab_w_out[j]
        else:
            y_in = _layer_cd(h, rope, cd_w_in, j, cd_group_norm[j], cd_forget_bias[j], batch, seq)
            w_out = cd_w_out[j]
        ka = y_in[0].shape[1]
        w_out = w_out.astype(BF16)
        x2, h = _mm_post(y_in, [w_out[:ka], w_out[ka:]], x2, mod[layer], mod[layer], mix_post_norm[layer],
                         ffn_pre_norm[layer], seq, 2, 3, 4, True, tm=512)
        hid = _ffn_up(h, ffn_w_gate, ffn_w_up, layer)
        last = layer == depth - 1
        nxt = layer if last else layer + 1
        x2, h = _mm_post([hid], [ffn_w_down[layer].astype(BF16)], x2, mod[layer], mod[nxt],
                         ffn_post_norm[layer], mix_pre_norm[nxt], seq, 5, 0, 1, not last, tm=256)
    return x2.reshape(batch, seq, d)
```

```python
import functools

import jax
import jax.numpy as jnp
import numpy as np
from jax import lax
from jax.experimental import pallas as pl
from jax.experimental.pallas import tpu as pltpu

F32 = jnp.float32
BF16 = jnp.bfloat16

D_MODEL = 2048
CHUNK = 64
EPS = 1e-6
ROPE_BASE = 10000.0
NEG_INF = -1e30

A_HEADS, A_DK, A_DV, A_CONV = 8, 128, 128, 4
B_HEADS, B_NOPE, B_ROPE, B_DV, B_Q_LORA, B_KV_LORA = 8, 128, 64, 128, 512, 512
C_HEADS, C_DK, C_DV = 8, 128, 128
D_HEADS, D_DH = 8, 128
LANES = 128
SUBLANES = 8

VMEM_LIMIT_BYTES = 56 * 1024 * 1024

RET_CHUNK = 128


def _cparams(n_axes):
    return pltpu.CompilerParams(dimension_semantics=("arbitrary",) * n_axes,
                                vmem_limit_bytes=VMEM_LIMIT_BYTES)


def _dot(a, b):
    return jnp.dot(a, b, preferred_element_type=F32)


def _dot_nt(a, b):
    return lax.dot_general(a, b, (((1,), (1,)), ((), ())), preferred_element_type=F32)


def _dot_tn(a, b):
    return lax.dot_general(a, b, (((0,), (0,)), ((), ())), preferred_element_type=F32)


def _split_bf16(a):
    hi = a.astype(BF16)
    lo = (a - hi.astype(F32)).astype(BF16)
    return hi, lo


def _dot3(a, b):
    ah, al = _split_bf16(a)
    bh, bl = _split_bf16(b)
    return _dot(ah, bh) + _dot(ah, bl) + _dot(al, bh)


def _sigmoid(x):
    return 1.0 / (1.0 + jnp.exp(-x))


def _silu(x):
    return x * _sigmoid(x)


def _softplus(x):
    return jnp.maximum(x, 0.0) + jnp.log(1.0 + jnp.exp(-jnp.abs(x)))


def _rms(x, w):
    return x * lax.rsqrt(jnp.mean(x * x, axis=-1, keepdims=True) + EPS) * w


def _ada_kernel(c_ref, w_ref, b_ref, o_ref):
    cond = _silu(c_ref[...]).astype(BF16)
    o_ref[...] = _dot(cond, w_ref[...].astype(BF16)) + b_ref[...]


def _ada(c_pad, ada_w, ada_b):
    depth, d, n = ada_w.shape
    rows = c_pad.shape[0]
    tn = 1024
    return pl.pallas_call(
        _ada_kernel,
        grid=(depth, n // tn),
        in_specs=[pl.BlockSpec((rows, d), lambda l, j: (0, 0)),
                  pl.BlockSpec((None, d, tn), lambda l, j: (l, 0, j)),
                  pl.BlockSpec((None, 1, tn), lambda l, j: (l, 0, j))],
        out_specs=pl.BlockSpec((None, rows, tn), lambda l, j: (l, 0, j)),
        out_shape=jax.ShapeDtypeStruct((depth, rows, n), F32),
        compiler_params=_cparams(2),
        name="ada_mod",
    )(c_pad, ada_w, ada_b.reshape(depth, 1, n))


def _rope_kernel(pos_ref, fb_ref, mb_ref, sb_ref, fc_ref, sc_ref, cb_ref, snb_ref, cc_ref, snc_ref):
    pos = pos_ref[...]
    ang_b = pos * fb_ref[...]
    cb_ref[...] = jnp.cos(ang_b) * mb_ref[...]
    snb_ref[...] = jnp.sin(ang_b) * sb_ref[...]
    ang_c = pos * fc_ref[...]
    cc_ref[...] = jnp.cos(ang_c)
    snc_ref[...] = jnp.sin(ang_c) * sc_ref[...]


def _rope_tables(pos_col):
    t = pos_col.shape[0]
    tb = min(1024, t)
    inv_b = ROPE_BASE ** (-jnp.arange(0, B_ROPE, 2, dtype=F32) / B_ROPE)
    inv_c = ROPE_BASE ** (-jnp.arange(0, C_DK, 2, dtype=F32) / C_DK)
    z32 = jnp.zeros((B_ROPE // 2,), F32)
    o32 = jnp.ones((B_ROPE // 2,), F32)
    o64 = jnp.ones((C_DK // 2,), F32)
    fb = jnp.concatenate([inv_b, z32, inv_b, z32])[None, :]
    mb = jnp.concatenate([o32, z32, o32, z32])[None, :]
    sb = jnp.concatenate([-o32, z32, o32, z32])[None, :]
    fc = jnp.concatenate([inv_c, inv_c])[None, :]
    sc = jnp.concatenate([-o64, o64])[None, :]
    row = pl.BlockSpec((1, LANES), lambda i: (0, 0))
    tab = pl.BlockSpec((tb, LANES), lambda i: (i, 0))
    shp = jax.ShapeDtypeStruct((t, LANES), F32)
    return pl.pallas_call(
        _rope_kernel,
        grid=(t // tb,),
        in_specs=[pl.BlockSpec((tb, 1), lambda i: (i, 0)), row, row, row, row, row],
        out_specs=[tab, tab, tab, tab],
        out_shape=[shp, shp, shp, shp],
        compiler_params=_cparams(1),
        name="rope_tables",
    )(pos_col, fb, mb, sb, fc, sc)


def _prenorm_kernel(x_ref, mod_ref, w_ref, h_ref, *, shift_idx, scale_idx):
    y = _rms(x_ref[...], w_ref[...])
    h = y * (1.0 + mod_ref[scale_idx:scale_idx + 1, :]) + mod_ref[shift_idx:shift_idx + 1, :]
    h_ref[...] = h.astype(h_ref.dtype)


def _prenorm(x2, mod_l, w, seq, shift_idx, scale_idx):
    t, d = x2.shape
    tm = min(512, seq)
    bpb = seq // tm
    return pl.pallas_call(
        functools.partial(_prenorm_kernel, shift_idx=shift_idx, scale_idx=scale_idx),
        grid=(t // tm,),
        in_specs=[pl.BlockSpec((tm, d), lambda i: (i, 0)),
                  pl.BlockSpec((None, 6, d), lambda i: (i // bpb, 0, 0)),
                  pl.BlockSpec((1, d), lambda i: (0, 0))],
        out_specs=pl.BlockSpec((tm, d), lambda i: (i, 0)),
        out_shape=jax.ShapeDtypeStruct((t, d), BF16),
        compiler_params=_cparams(1),
        name="prenorm",
    )(x2, mod_l, w.reshape(1, d))


def _mm_kernel(*refs, cast_w, scaled):
    a_ref, w_ref = refs[0], refs[1]
    s_ref = refs[2] if scaled else None
    o_ref = refs[3] if scaled else refs[2]
    if cast_w:
        wb_ref = refs[-1]

        @pl.when(pl.program_id(1) == 0)
        def _():
            wb_ref[...] = w_ref[...].astype(BF16)

        w = wb_ref[...]
    else:
        w = w_ref[...]
    y = _dot(a_ref[...], w)
    if scaled:
        y = y * s_ref[...]
    o_ref[...] = y.astype(o_ref.dtype)


def _mm(a, w, out_dtype, layer=None, col0=0, ncols=None, colscale=None, tm=512, tn=1024):
    m, k = a.shape
    n = w.shape[-1] if ncols is None else ncols
    tm = min(tm, m)
    tn = min(tn, n)
    jb0 = col0 // tn
    cast_w = w.dtype != BF16
    if layer is None:
        wspec = pl.BlockSpec((k, tn), lambda j, i: (0, jb0 + j))
    else:
        wspec = pl.BlockSpec((None, k, tn), lambda j, i: (layer, 0, jb0 + j))
    in_specs = [pl.BlockSpec((tm, k), lambda j, i: (i, 0)), wspec]
    args = [a, w]
    if colscale is not None:
        in_specs.append(pl.BlockSpec((1, tn), lambda j, i: (0, j)))
        args.append(colscale.reshape(1, n).astype(F32))
    return pl.pallas_call(
        functools.partial(_mm_kernel, cast_w=cast_w, scaled=colscale is not None),
        grid=(n // tn, m // tm),
        in_specs=in_specs,
        out_specs=pl.BlockSpec((tm, tn), lambda j, i: (i, j)),
        out_shape=jax.ShapeDtypeStruct((m, n), out_dtype),
        scratch_shapes=[pltpu.VMEM((k, tn), BF16)] if cast_w else [],
        compiler_params=_cparams(2),
        name="matmul",
    )(*args)


MM_POST_SPLIT = 2


def _mm_post_kernel(*refs, n_in, gate_idx, nshift_idx, nscale_idx, emit_h):
    a_refs = refs[:n_in]
    w_refs = refs[n_in:2 * n_in]
    x_ref, mod_ref, modn_ref, pw_ref, nw_ref = refs[2 * n_in:2 * n_in + 5]
    outs = refs[2 * n_in + 5:]
    tm = x_ref.shape[0]
    for r0 in range(0, tm, tm // MM_POST_SPLIT):
        rows = slice(r0, r0 + tm // MM_POST_SPLIT)
        y = _dot(a_refs[0][rows, :], w_refs[0][...])
        for a_ref, w_ref in zip(a_refs[1:], w_refs[1:]):
            y = y + _dot(a_ref[rows, :], w_ref[...])
        xn = x_ref[rows, :] + mod_ref[gate_idx:gate_idx + 1, :] * _rms(y, pw_ref[...])
        outs[0][rows, :] = xn
        if emit_h:
            hn = _rms(xn, nw_ref[...])
            hn = hn * (1.0 + modn_ref[nscale_idx:nscale_idx + 1, :]) + modn_ref[nshift_idx:nshift_idx + 1, :]
            outs[1][rows, :] = hn.astype(outs[1].dtype)


def _mm_post(a_list, w_list, x2, mod_l, modn_l, post_w, next_w, seq, gate_idx, nshift_idx, nscale_idx, emit_h,
             tm):
    m = a_list[0].shape[0]
    d = w_list[0].shape[1]
    tm = min(tm, seq)
    bpb = seq // tm
    row = pl.BlockSpec((1, d), lambda i: (0, 0))
    modspec = pl.BlockSpec((None, 6, d), lambda i: (i // bpb, 0, 0))
    xspec = pl.BlockSpec((tm, d), lambda i: (i, 0))
    a_specs = [pl.BlockSpec((tm, a.shape[1]), lambda i: (i, 0)) for a in a_list]
    w_specs = [pl.BlockSpec(w.shape, lambda i: (0, 0), pipeline_mode=pl.Buffered(1)) for w in w_list]
    out_specs = [xspec]
    out_shape = [jax.ShapeDtypeStruct((m, d), F32)]
    if emit_h:
        out_specs.append(xspec)
        out_shape.append(jax.ShapeDtypeStruct((m, d), BF16))
    res = pl.pallas_call(
        functools.partial(_mm_post_kernel, n_in=len(a_list), gate_idx=gate_idx, nshift_idx=nshift_idx,
                          nscale_idx=nscale_idx, emit_h=emit_h),
        grid=(m // tm,),
        in_specs=a_specs + w_specs + [xspec, modspec, modspec, row, row],
        out_specs=out_specs,
        out_shape=out_shape,
        compiler_params=_cparams(1),
        name="matmul_post",
    )(*a_list, *w_list, x2, mod_l, modn_l, post_w.reshape(1, d), next_w.reshape(1, d))
    return res if emit_h else (res[0], None)


def _ffn_up_kernel(a_ref, wg_ref, wu_ref, o_ref, wgb_ref, wub_ref):
    @pl.when(pl.program_id(1) == 0)
    def _():
        wgb_ref[...] = wg_ref[...].astype(BF16)
        wub_ref[...] = wu_ref[...].astype(BF16)

    a = a_ref[...]
    g = _dot(a, wgb_ref[...])
    u = _dot(a, wub_ref[...])
    o_ref[...] = (_silu(g) * u).astype(o_ref.dtype)


def _ffn_up(a, wg, wu, layer, tm=512, tn=512):
    m, k = a.shape
    n = wg.shape[-1]
    tm = min(tm, m)
    wspec = pl.BlockSpec((None, k, tn), lambda j, i: (layer, 0, j))
    return pl.pallas_call(
        _ffn_up_kernel,
        grid=(n // tn, m // tm),
        in_specs=[pl.BlockSpec((tm, k), lambda j, i: (i, 0)), wspec, wspec],
        out_specs=pl.BlockSpec((tm, tn), lambda j, i: (i, j)),
        out_shape=jax.ShapeDtypeStruct((m, n), BF16),
        scratch_shapes=[pltpu.VMEM((k, tn), BF16), pltpu.VMEM((k, tn), BF16)],
        compiler_params=_cparams(2),
        name="ffn_up",
    )(a, wg, wu)


def _group_slabs(x, o_ref, heads_per_group):
    for g in range(o_ref.shape[0]):
        shift = (LANES - g * heads_per_group) % LANES
        o_ref[g] = x if shift == 0 else pltpu.roll(x, shift, 1)


def _gdn_gate_kernel(s_ref, alog_ref, dt_ref, o_ref, *, hg):
    s = s_ref[...]
    g = -jnp.exp(alog_ref[...]) * _softplus(s + dt_ref[...])
    beta = _sigmoid(s)
    lane = lax.broadcasted_iota(jnp.int32, s.shape, 1)
    _group_slabs(jnp.where(lane < A_HEADS, g, beta), o_ref, hg)


def _gdn_gates(slab_src, col_block, a_log, dt_bias, hg):
    t = slab_src.shape[0]
    tb = min(1024, t)
    ng = A_HEADS // hg
    pad = jnp.zeros((LANES - A_HEADS,), F32)
    alog_row = jnp.concatenate([a_log.astype(F32), pad])[None, :]
    dt_row = jnp.concatenate([dt_bias.astype(F32), pad])[None, :]
    row = pl.BlockSpec((1, LANES), lambda i: (0, 0))
    return pl.pallas_call(
        functools.partial(_gdn_gate_kernel, hg=hg),
        grid=(t // tb,),
        in_specs=[pl.BlockSpec((tb, LANES), lambda i: (i, col_block)), row, row],
        out_specs=pl.BlockSpec((ng, tb, LANES), lambda i: (0, i, 0)),
        out_shape=jax.ShapeDtypeStruct((ng, t, LANES), F32),
        compiler_params=_cparams(1),
        name="gdn_gates",
    )(slab_src, alog_row, dt_row)


GDN_HG = 4
GDN_TB = 256


def _unit_lower_inverses(n_mats):
    c = n_mats[0].shape[0]
    r = lax.broadcasted_iota(jnp.int32, (c, c), 0)
    col = lax.broadcasted_iota(jnp.int32, (c, c), 1)
    eye = jnp.where(r == col, 1.0, 0.0)
    ps = [eye - n for n in n_mats]
    ms = [_dot3(n, n) for n in n_mats]
    power = 2
    while True:
        ps = [p + _dot3(p, m) for p, m in zip(ps, ms)]
        power *= 2
        if power >= c:
            break
        ms = [_dot3(m, m) for m in ms]
    return ps


def _gdn_kernel(q_ref, k_ref, v_ref, z_ref, cwq_ref, cwk_ref, cwv_ref, gates_ref, grow_ref, onw_ref,
                o_ref, xq_ref, xk_ref, xv_ref, st_ref, *, tb, hg):
    j = pl.program_id(2)

    @pl.when(j == 0)
    def _():
        st_ref[...] = jnp.zeros_like(st_ref)
        for h_ref in (xq_ref, xk_ref, xv_ref):
            h_ref[...] = jnp.zeros_like(h_ref)

    def conv_silu(src_ref, hist_ref, cw_ref):
        x = src_ref[...]
        prev = hist_ref[...]
        hrows = prev.shape[0]
        row = lax.broadcasted_iota(jnp.int32, prev.shape, 0)
        acc = x * cw_ref[A_CONV - 1:A_CONV, :]
        for s in range(1, A_CONV):
            xs = pltpu.roll(x, s, 0)
            top = jnp.where(row < s, pltpu.roll(prev, s, 0), xs[0:hrows])
            shifted = jnp.concatenate([top, xs[hrows:]], axis=0)
            acc = acc + shifted * cw_ref[A_CONV - 1 - s:A_CONV - s, :]
        hist_ref[...] = x[tb - hrows:tb]
        return _silu(acc)

    q_all = conv_silu(q_ref, xq_ref, cwq_ref)
    k_all = conv_silu(k_ref, xk_ref, cwk_ref)
    v_all = conv_silu(v_ref, xv_ref, cwv_ref)

    c = CHUNK
    r = lax.broadcasted_iota(jnp.int32, (c, c), 0)
    col = lax.broadcasted_iota(jnp.int32, (c, c), 1)
    tri = r >= col
    strict = r > col

    nch = tb // c
    items = [(hh, ci) for hh in range(hg) for ci in range(nch)]
    qn, kn, vn = [], [], []
    for hh in range(hg):
        lanes = slice(hh * A_DK, (hh + 1) * A_DK)
        qh = q_all[:, lanes]
        kh = k_all[:, lanes]
        qn.append(qh * lax.rsqrt(jnp.sum(qh * qh, axis=-1, keepdims=True) + EPS) * (A_DK ** -0.5))
        kn.append(kh * lax.rsqrt(jnp.sum(kh * kh, axis=-1, keepdims=True) + EPS))
        vn.append(v_all[:, lanes])

    def rows_of(ci):
        return slice(ci * c, (ci + 1) * c)

    qc = [qn[hh][rows_of(ci)] for hh, ci in items]
    kc = [kn[hh][rows_of(ci)] for hh, ci in items]
    vc = [vn[hh][rows_of(ci)] for hh, ci in items]
    b_col = [gates_ref[rows_of(ci), A_HEADS + hh:A_HEADS + hh + 1] for hh, ci in items]
    gc_col = [jnp.sum(jnp.where(tri, grow_ref[hh][:, rows_of(ci)], 0.0), axis=1, keepdims=True)
              for hh, ci in items]
    gc_row = [jnp.sum(jnp.where(r <= col, gates_ref[rows_of(ci), hh:hh + 1], 0.0), axis=0, keepdims=True)
              for hh, ci in items]
    decay = [jnp.where(tri, jnp.exp(jnp.minimum(a - b, 0.0)), 0.0) for a, b in zip(gc_col, gc_row)]
    e_gc = [jnp.exp(a) for a in gc_col]
    g_last = [a[c - 1:c, :] for a in gc_col]
    kb = [x.astype(BF16) for x in kc]
    kk = [_dot_nt(x, x) for x in kb]
    qk = [_dot_nt(x.astype(BF16), y) for x, y in zip(qc, kb)]
    n_mat = [jnp.where(strict, b * x * d, 0.0) for b, x, d in zip(b_col, kk, decay)]
    qk = [jnp.where(tri, x * d, 0.0).astype(BF16) for x, d in zip(qk, decay)]
    t_inv = _unit_lower_inverses(n_mat)
    wu = [_dot3(t, jnp.concatenate([k * (b * e), v * b], axis=1)).astype(BF16)
          for t, k, v, b, e in zip(t_inv, kc, vc, b_col, e_gc)]
    kdb = [(k * jnp.exp(gl - g)).astype(BF16) for k, gl, g in zip(kc, g_last, gc_col)]
    mb = [_dot_tn(kd, x) for kd, x in zip(kdb, wu)]
    m_c = [x[:, :A_DK].astype(BF16) for x in mb]
    b_c = [x[:, A_DK:] for x in mb]
    qwu = [_dot(a, x) for a, x in zip(qk, wu)]
    q_eff = [(q * e - x[:, :A_DK]).astype(BF16) for q, e, x in zip(qc, e_gc, qwu)]
    qku = [x[:, A_DK:] for x in qwu]
    e_gl = [jnp.exp(x) for x in g_last]
    states = [st_ref[hh] for hh in range(hg)]
    o_c = [None] * len(items)
    for ci in range(nch):
        idx = [hh * nch + ci for hh in range(hg)]
        sb = [s.astype(BF16) for s in states]
        for hh in range(hg):
            o_c[idx[hh]] = _dot(q_eff[idx[hh]], sb[hh]) + qku[idx[hh]]
        states = [states[hh] * e_gl[idx[hh]] + b_c[idx[hh]] - _dot(m_c[idx[hh]], sb[hh]) for hh in range(hg)]
    for hh in range(hg):
        st_ref[hh] = states[hh]
    for i, (hh, ci) in enumerate(items):
        lanes = slice(hh * A_DK, (hh + 1) * A_DK)
        zc = z_ref[rows_of(ci), lanes]
        o_ref[rows_of(ci), lanes] = (_rms(o_c[i], onw_ref[...]) * _silu(zc)).astype(o_ref.dtype)


def _gdn(proj, conv_w, gates, g_row, out_norm, batch, seq):
    t = proj.shape[0]
    tb = min(GDN_TB, seq)
    hg = GDN_HG
    w = hg * A_DK
    nj = seq // tb
    ng = A_HEADS // hg
    kq = A_HEADS * A_DK
    off_k = kq // w
    off_v = 2 * kq // w
    off_z = 3 * kq // w

    def colspec(off):
        return pl.BlockSpec((tb, w), lambda b, g, j: (b * nj + j, off + g))

    def cwspec(off):
        return pl.BlockSpec((A_CONV, w), lambda b, g, j: (0, off + g))

    colv = pl.BlockSpec((None, tb, LANES), lambda b, g, j: (g, b * nj + j, 0))
    rowv = pl.BlockSpec((hg, 1, tb), lambda b, g, j: (g, 0, b * nj + j))
    return pl.pallas_call(
        functools.partial(_gdn_kernel, tb=tb, hg=hg),
        grid=(batch, ng, nj),
        in_specs=[colspec(0), colspec(off_k), colspec(off_v), colspec(off_z),
                  cwspec(0), cwspec(off_k), cwspec(off_v), colv, rowv,
                  pl.BlockSpec((1, A_DV), lambda b, g, j: (0, 0))],
        out_specs=pl.BlockSpec((tb, w), lambda b, g, j: (b * nj + j, g)),
        out_shape=jax.ShapeDtypeStruct((t, A_HEADS * A_DV), BF16),
        scratch_shapes=[pltpu.VMEM((SUBLANES, w), F32), pltpu.VMEM((SUBLANES, w), F32),
                        pltpu.VMEM((SUBLANES, w), F32), pltpu.VMEM((hg, A_DK, A_DV), F32)],
        compiler_params=_cparams(3),
        name="gated_deltanet",
    )(proj, proj, proj, proj, conv_w, conv_w, conv_w, gates, g_row, out_norm.reshape(1, A_DV))


def _rope_lanes(x, cos, sin_signed):
    return x * cos + pltpu.roll(x, LANES // 2, 1) * sin_signed


def _mla_q_kernel(cq_ref, nw_ref, w_ref, cos_ref, sin_ref, o_ref):
    n = _rms(cq_ref[...], nw_ref[...]).astype(BF16)
    cos = cos_ref[...]
    sin = sin_ref[...]
    for h in range(B_HEADS):
        qh = _dot(n, w_ref[:, h * 256:(h + 1) * 256])
        o_ref[:, h * 256:h * 256 + LANES] = qh[:, :LANES].astype(o_ref.dtype)
        o_ref[:, h * 256 + LANES:(h + 1) * 256] = _rope_lanes(qh[:, LANES:], cos, sin).astype(o_ref.dtype)


def _mla_q(proj, col_block, q_norm, w_uq_p, cos_b, sin_b):
    t = proj.shape[0]
    tm = min(512, t)
    n = w_uq_p.shape[1]
    return pl.pallas_call(
        _mla_q_kernel,
        grid=(t // tm,),
        in_specs=[pl.BlockSpec((tm, B_Q_LORA), lambda i: (i, col_block)),
                  pl.BlockSpec((1, B_Q_LORA), lambda i: (0, 0)),
                  pl.BlockSpec((B_Q_LORA, n), lambda i: (0, 0)),
                  pl.BlockSpec((tm, LANES), lambda i: (i, 0)),
                  pl.BlockSpec((tm, LANES), lambda i: (i, 0))],
        out_specs=pl.BlockSpec((tm, n), lambda i: (i, 0)),
        out_shape=jax.ShapeDtypeStruct((t, n), BF16),
        compiler_params=_cparams(1),
        name="mla_q_up",
    )(proj, q_norm.reshape(1, B_Q_LORA), w_uq_p, cos_b, sin_b)


def _mla_kv_kernel(ckv_ref, nw_ref, w_ref, kpe_ref, cos_ref, sin_ref, k_ref, v_ref):
    n = _rms(ckv_ref[...], nw_ref[...]).astype(BF16)
    kpe = _rope_lanes(kpe_ref[...], cos_ref[...], sin_ref[...]).astype(k_ref.dtype)
    nk = B_HEADS * B_NOPE
    kn = _dot(n, w_ref[:, :nk])
    for h in range(B_HEADS):
        k_ref[:, h * 256:h * 256 + LANES] = kn[:, h * LANES:(h + 1) * LANES].astype(k_ref.dtype)
        k_ref[:, h * 256 + LANES:(h + 1) * 256] = kpe
    v_ref[...] = _dot(n, w_ref[:, nk:]).astype(v_ref.dtype)


def _mla_kv(proj, col_block, kv_norm, w_ukv_p, kpe_block, cos_b, sin_b):
    t = proj.shape[0]
    tm = min(512, t)
    n = w_ukv_p.shape[1]
    tab = pl.BlockSpec((tm, LANES), lambda i: (i, 0))
    kpe_spec = pl.BlockSpec((tm, LANES), lambda i: (i, kpe_block))
    return pl.pallas_call(
        _mla_kv_kernel,
        grid=(t // tm,),
        in_specs=[pl.BlockSpec((tm, B_KV_LORA), lambda i: (i, col_block)),
                  pl.BlockSpec((1, B_KV_LORA), lambda i: (0, 0)),
                  pl.BlockSpec((B_KV_LORA, n), lambda i: (0, 0)),
                  kpe_spec, tab, tab],
        out_specs=[pl.BlockSpec((tm, B_HEADS * 256), lambda i: (i, 0)),
                   pl.BlockSpec((tm, B_HEADS * B_DV), lambda i: (i, 0))],
        out_shape=[jax.ShapeDtypeStruct((t, B_HEADS * 256), BF16),
                   jax.ShapeDtypeStruct((t, B_HEADS * B_DV), BF16)],
        compiler_params=_cparams(1),
        name="mla_kv_up",
    )(proj, kv_norm.reshape(1, B_KV_LORA), w_ukv_p, proj, cos_b, sin_b)


ATTN_T = 512


def _attn_kernel(*refs, tq, nq, dk, dv, hg, mask_gran, has_bias):
    if has_bias:
        q_ref, k_ref, v_ref, cq_ref, ck_ref, o_ref, m_ref, acc_ref, vt_ref = refs
    else:
        q_ref, k_ref, v_ref, o_ref, m_ref, acc_ref, vt_ref = refs
    qi = pl.program_id(2)

    @pl.when(qi == 0)
    def _():
        for hh in range(hg):
            for jb in range(nq):
                vblk = v_ref[jb * tq:(jb + 1) * tq, hh * dv:(hh + 1) * dv]
                vt_ref[hh, jb, 0:dv, :] = vblk.astype(F32).T.astype(BF16)
                vt_ref[hh, jb, dv:, :] = jnp.ones((ATTN_ONES, tq), BF16)

    m_ref[...] = jnp.full(m_ref.shape, NEG_INF, F32)
    acc_ref[...] = jnp.zeros_like(acc_ref)
    heads = range(hg)

    def step(jb, masked):
        start = pl.multiple_of(jb * tq, tq)
        s = [_dot_nt(k_ref[pl.ds(start, tq), hh * dk:(hh + 1) * dk], q_ref[:, hh * dk:(hh + 1) * dk])
             for hh in heads]
        if has_bias:
            s = [s[hh] - ck_ref[pl.ds(start, tq), hh:hh + 1] for hh in heads]
        if masked:
            r = lax.broadcasted_iota(jnp.int32, (tq, tq), 0)
            c = lax.broadcasted_iota(jnp.int32, (tq, tq), 1)
            if mask_gran > 1:
                shift = int(np.log2(mask_gran))
                r = lax.shift_right_logical(r, shift)
                c = lax.shift_right_logical(c, shift)
            keep = r <= c
            s = [jnp.where(keep, s[hh], NEG_INF) for hh in heads]
        m_prev = [m_ref[hh] for hh in heads]
        m_cur = [jnp.max(s[hh], axis=0, keepdims=True) for hh in heads]
        if has_bias:
            m_cur = [m_cur[hh] + cq_ref[hh] for hh in heads]
        m_new = [jnp.maximum(m_prev[hh], m_cur[hh]) for hh in heads]
        sub = [m_new[hh] - cq_ref[hh] for hh in heads] if has_bias else m_new
        p = [jnp.exp2(s[hh] - sub[hh]).astype(BF16) for hh in heads]
        alpha = [jnp.exp2(m_prev[hh] - m_new[hh]) for hh in heads]
        for hh in heads:
            m_ref[hh] = m_new[hh]
        pv = [_dot(vt_ref[hh, jb], p[hh]) for hh in heads]
        for hh in heads:
            acc_ref[hh] = alpha[hh] * acc_ref[hh] + pv[hh]

    def body(jb, carry):
        step(jb, False)
        return carry

    lax.fori_loop(0, qi, body, 0)
    step(qi, True)
    for hh in heads:
        acc = acc_ref[hh]
        o_ref[:, hh * dv:(hh + 1) * dv] = (acc[:dv] / acc[dv:dv + 1]).T.astype(o_ref.dtype)


ATTN_HG = 4
ATTN_ONES = 16
LOG2E = 1.4426950408889634


def _attention(q, k, v, dk, dv, heads, batch, seq, mask_gran, q_off=0, k_off=0, v_off=0, bias=None):
    t = q.shape[0]
    tq = min(ATTN_T, seq)
    nq = seq // tq
    hg = ATTN_HG
    qb, kb, vb = q_off // (hg * dk), k_off // (hg * dk), v_off // (hg * dv)
    dvx = dv + ATTN_ONES
    in_specs = [pl.BlockSpec((tq, hg * dk), lambda b, g, i: (b * nq + i, qb + g)),
                pl.BlockSpec((seq, hg * dk), lambda b, g, i: (b, kb + g)),
                pl.BlockSpec((seq, hg * dv), lambda b, g, i: (b, vb + g))]
    args = [q, k, v]
    if bias is not None:
        cq, ck = bias
        in_specs += [pl.BlockSpec((None, hg, 1, tq), lambda b, g, i: (b, g, 0, i)),
                     pl.BlockSpec((None, seq, LANES), lambda b, g, i: (g, b, 0))]
        args += [cq, ck]
    return pl.pallas_call(
        functools.partial(_attn_kernel, tq=tq, nq=nq, dk=dk, dv=dv, hg=hg, mask_gran=mask_gran,
                          has_bias=bias is not None),
        grid=(batch, heads // hg, nq),
        in_specs=in_specs,
        out_specs=pl.BlockSpec((tq, hg * dv), lambda b, g, i: (b * nq + i, g)),
        out_shape=jax.ShapeDtypeStruct((t, heads * dv), BF16),
        scratch_shapes=[pltpu.VMEM((hg, 1, tq), F32), pltpu.VMEM((hg, dvx, tq), F32),
                        pltpu.VMEM((hg, nq, dvx, tq), BF16)],
        compiler_params=_cparams(3),
        name="flash_attention",
    )(*args)


RET_HG = 2
RET_TB = 512


def _ret_kernel(q_ref, k_ref, v_ref, g_ref, cos_ref, sin_ref, dec_ref, xi_ref, zeta_ref, gch_ref, gnw_ref,
                o_ref, st_ref, *, tb, hg, c):
    j = pl.program_id(2)

    @pl.when(j == 0)
    def _():
        st_ref[...] = jnp.zeros_like(st_ref)

    cos = cos_ref[...]
    sin = sin_ref[...]
    nch = tb // c
    items = [(hh, ci) for hh in range(hg) for ci in range(nch)]

    def lanes_of(hh):
        return slice(hh * C_DK, (hh + 1) * C_DK)

    def rows_of(ci):
        return slice(ci * c, (ci + 1) * c)

    qh = [_rope_lanes(q_ref[:, lanes_of(hh)], cos, sin) for hh in range(hg)]
    kh = [_rope_lanes(k_ref[:, lanes_of(hh)], cos, sin) for hh in range(hg)]
    qb = [qh[hh][rows_of(ci)].astype(BF16) for hh, ci in items]
    kc = [kh[hh][rows_of(ci)] for hh, ci in items]
    vb = [v_ref[rows_of(ci), lanes_of(hh)].astype(BF16) for hh, ci in items]
    scores = [(_dot_nt(q, k.astype(BF16)) * dec_ref[hh]).astype(BF16) for q, k, (hh, _) in zip(qb, kc, items)]
    inner = [_dot(s, v) for s, v in zip(scores, vb)]
    kzv = [_dot_tn((k * zeta_ref[hh]).astype(BF16), v) for k, v, (hh, _) in zip(kc, vb, items)]
    states = []
    for hh in range(hg):
        state = st_ref[hh]
        for ci in range(nch):
            states.append(state)
            state = state * gch_ref[hh] + kzv[hh * nch + ci]
        st_ref[hh] = state
    cross = [_dot(q, s.astype(BF16)) * xi_ref[hh] for q, s, (hh, _) in zip(qb, states, items)]
    for i, (hh, ci) in enumerate(items):
        o_c = inner[i] + cross[i]
        gate = _silu(g_ref[rows_of(ci), lanes_of(hh)])
        o_ref[rows_of(ci), lanes_of(hh)] = (_rms(o_c, gnw_ref[hh]) * gate).astype(o_ref.dtype)


def _retention(proj, cos_c, sin_c, group_norm, batch, seq):
    t = proj.shape[0]
    tb = min(RET_TB, seq)
    c = min(RET_CHUNK, seq)
    hg = RET_HG
    w = hg * C_DK
    nj = seq // tb
    ng = C_HEADS // hg
    nh = C_HEADS * C_DK // w

    log_gamma = jnp.log1p(-jnp.exp2(-5.0 - jnp.arange(C_HEADS, dtype=F32)))
    idx = jnp.arange(c, dtype=F32)
    rel = idx[:, None] - idx[None, :]
    dec = jnp.where(rel >= 0, jnp.exp(log_gamma[:, None, None] * jnp.maximum(rel, 0.0)), 0.0)
    xi = jnp.exp(log_gamma[:, None] * (idx + 1.0))
    zeta = jnp.exp(log_gamma[:, None] * (c - 1.0 - idx))
    gch = jnp.exp(log_gamma * c)
    xi_b = jnp.broadcast_to(xi[:, :, None], (C_HEADS, c, C_DV))
    zeta_b = jnp.broadcast_to(zeta[:, :, None], (C_HEADS, c, C_DK))
    gch_b = jnp.broadcast_to(gch[:, None, None], (C_HEADS, 1, C_DV))

    def colspec(off):
        return pl.BlockSpec((tb, w), lambda b, g, j: (b * nj + j, off + g))

    tab = pl.BlockSpec((tb, LANES), lambda b, g, j: (b * nj + j, 0))

    def hspec(r, cdim):
        return pl.BlockSpec((hg, r, cdim), lambda b, g, j: (g, 0, 0))

    return pl.pallas_call(
        functools.partial(_ret_kernel, tb=tb, hg=hg, c=c),
        grid=(batch, ng, nj),
        in_specs=[colspec(0), colspec(nh), colspec(2 * nh), colspec(3 * nh), tab, tab,
                  hspec(c, c), hspec(c, C_DV), hspec(c, C_DK), hspec(1, C_DV), hspec(1, C_DV)],
        out_specs=pl.BlockSpec((tb, w), lambda b, g, j: (b * nj + j, g)),
        out_shape=jax.ShapeDtypeStruct((t, C_HEADS * C_DV), BF16),
        scratch_shapes=[pltpu.VMEM((hg, C_DK, C_DV), F32)],
        compiler_params=_cparams(3),
        name="retention",
    )(proj, proj, proj, proj, cos_c, sin_c, dec, xi_b, zeta_b, gch_b, group_norm.reshape(C_HEADS, 1, C_DV))


def _fox_cum_kernel(s_ref, b_ref, o_ref, *, seq, hg):
    x = s_ref[...] + b_ref[...]
    acc = -_softplus(-x)
    row = lax.broadcasted_iota(jnp.int32, acc.shape, 0)
    shift = 1
    while shift < seq:
        acc = acc + jnp.where(row >= shift, pltpu.roll(acc, shift, 0), 0.0)
        shift *= 2
    _group_slabs(acc * LOG2E, o_ref, hg)


def _fox_cum(slab, forget_bias, batch, seq, hg):
    t = slab.shape[0]
    ng = D_HEADS // hg
    pad = jnp.zeros((LANES - D_HEADS,), F32)
    b_row = jnp.concatenate([forget_bias.astype(F32), pad])[None, :]
    return pl.pallas_call(
        functools.partial(_fox_cum_kernel, seq=seq, hg=hg),
        grid=(batch,),
        in_specs=[pl.BlockSpec((seq, LANES), lambda b: (b, 0)),
                  pl.BlockSpec((1, LANES), lambda b: (0, 0))],
        out_specs=pl.BlockSpec((ng, seq, LANES), lambda b: (0, b, 0)),
        out_shape=jax.ShapeDtypeStruct((ng, t, LANES), F32),
        compiler_params=_cparams(1),
        name="fox_cumsum",
    )(slab, b_row)


def _pe_lanes(w_pe):
    k = w_pe.shape[0]
    z = jnp.zeros((k, B_ROPE // 2), w_pe.dtype)
    return jnp.concatenate([w_pe[:, :B_ROPE // 2], z, w_pe[:, B_ROPE // 2:], z], axis=1)


AB_MAIN = A_HEADS * (2 * A_DK + A_DV) + A_HEADS * A_DV


def _ab_weights(w_in, w_uq, w_ukv):
    o = AB_MAIN
    w_a = w_in[:, o:o + A_HEADS]; o += A_HEADS
    w_b = w_in[:, o:o + A_HEADS]; o += A_HEADS
    w_cq = w_in[:, o:o + B_Q_LORA]; o += B_Q_LORA
    w_ckv = w_in[:, o:o + B_KV_LORA]; o += B_KV_LORA
    w_kpe = w_in[:, o:o + B_ROPE]
    d = w_in.shape[0]
    w_tail = jnp.concatenate([w_cq, w_ckv, _pe_lanes(w_kpe), w_a, w_b,
                              jnp.zeros((d, LANES - 2 * A_HEADS), w_in.dtype)], axis=1).astype(BF16)
    scale = (B_NOPE + B_ROPE) ** -0.5 * LOG2E
    wq = (w_uq * scale).reshape(B_Q_LORA, B_HEADS, B_NOPE + B_ROPE)
    wq_p = jnp.concatenate(
        [wq[:, :, :B_NOPE],
         _pe_lanes(wq[:, :, B_NOPE:].reshape(B_Q_LORA * B_HEADS, B_ROPE)).reshape(B_Q_LORA, B_HEADS, LANES)],
        axis=2).reshape(B_Q_LORA, B_HEADS * 256).astype(BF16)
    wkv = w_ukv.reshape(B_KV_LORA, B_HEADS, B_NOPE + B_DV)
    wkv_p = jnp.concatenate([wkv[:, :, :B_NOPE].reshape(B_KV_LORA, B_HEADS * B_NOPE),
                             wkv[:, :, B_NOPE:].reshape(B_KV_LORA, B_HEADS * B_DV)], axis=1).astype(BF16)
    return w_tail, wq_p, wkv_p


def _layer_ab(h, rope, w_in_all, j, conv_w, a_log, dt_bias, out_norm, q_norm, w_uq, kv_norm, w_ukv, batch, seq):
    cos_b, sin_b, _, _ = rope
    w_tail, wq_p, wkv_p = _ab_weights(w_in_all[j], w_uq, w_ukv)
    proj = _mm(h, w_in_all, F32, layer=j, ncols=AB_MAIN)
    tail = _mm(h, w_tail, F32, tn=w_tail.shape[1])
    cq_block, ckv_block = 0, 1
    kpe_block = (B_Q_LORA + B_KV_LORA) // LANES
    gates = _gdn_gates(tail, kpe_block + 1, a_log, dt_bias, GDN_HG)
    g_row = gates[0][:, :A_HEADS].T.reshape(A_HEADS, 1, h.shape[0])
    o_a = _gdn(proj, conv_w, gates, g_row, out_norm, batch, seq)
    q_b = _mla_q(tail, cq_block, q_norm, wq_p, cos_b, sin_b)
    k_b, v_b = _mla_kv(tail, ckv_block, kv_norm, wkv_p, kpe_block, cos_b, sin_b)
    o_b = _attention(q_b, k_b, v_b, 256, B_DV, B_HEADS, batch, seq, CHUNK)
    return [o_a, o_b]


def _layer_cd(h, rope, w_in_all, j, group_norm, forget_bias, batch, seq):
    _, _, cos_c, sin_c = rope
    n = C_HEADS * C_DK
    ones = jnp.ones((n,), F32)
    scale_c = jnp.concatenate([ones, ones * (C_DK ** -0.5), ones, ones])
    scale_d = jnp.concatenate([ones * (D_DH ** -0.5 * LOG2E), ones, ones])
    proj_c = _mm(h, w_in_all, F32, layer=j, ncols=4 * n, colscale=scale_c)
    proj_d = _mm(h, w_in_all, BF16, layer=j, col0=4 * n, ncols=3 * n, colscale=scale_d)
    w_f = jnp.concatenate([w_in_all[j][:, 7 * n:7 * n + D_HEADS],
                           jnp.zeros((w_in_all.shape[1], LANES - D_HEADS), w_in_all.dtype)], axis=1).astype(BF16)
    slab = _mm(h, w_f, F32, tn=LANES)
    o_c = _retention(proj_c, cos_c, sin_c, group_norm, batch, seq)
    ck = _fox_cum(slab, forget_bias, batch, seq, ATTN_HG)
    cq = ck[0][:, :D_HEADS].reshape(batch, seq, D_HEADS).transpose(0, 2, 1).reshape(batch, D_HEADS, 1, seq)
    o_d = _attention(proj_d, proj_d, proj_d, D_DH, D_DH, D_HEADS, batch, seq, 1,
                     q_off=0, k_off=D_HEADS * D_DH, v_off=2 * D_HEADS * D_DH, bias=(cq, ck))
    return [o_c, o_d]


def kernel(x, c, positions, ada_w, ada_b, mix_pre_norm, mix_post_norm, ffn_pre_norm, ffn_post_norm, ab_w_in, ab_conv_w, ab_a_log, ab_dt_bias, ab_out_norm, ab_q_norm, ab_w_uq, ab_kv_norm, ab_w_ukv, ab_w_out, cd_w_in, cd_group_norm, cd_forget_bias, cd_w_out, ffn_w_gate, ffn_w_up, ffn_w_down):
    batch, seq, d = x.shape
    depth = ada_w.shape[0]
    t = batch * seq
    rows = 8
    c_pad = jnp.concatenate([c, jnp.zeros((rows - batch, d), c.dtype)], axis=0)
    mod = _ada(c_pad, ada_w, ada_b).reshape(depth, rows, 6, d)
    rope = _rope_tables(positions.astype(F32).reshape(t, 1))
    x2 = x.reshape(t, d)
    h = _prenorm(x2, mod[0], mix_pre_norm[0], seq, 0, 1)
    for layer in range(depth):
        j = layer // 2
        if layer % 2 == 0:
            y_in = _layer_ab(h, rope, ab_w_in, j, ab_conv_w[j], ab_a_log[j], ab_dt_bias[j], ab_out_norm[j],
                             ab_q_norm[j], ab_w_uq[j], ab_kv_norm[j], ab_w_ukv[j], batch, seq)
            w_out = ab_w_out[j]
        else:
            y_in = _layer_cd(h, rope, cd_w_in, j, cd_group_norm[j], cd_forget_bias[j], batch, seq)
            w_out = cd_w_out[j]
        ka = y_in[0].shape[1]
        w_out = w_out.astype(BF16)
        x2, h = _mm_post(y_in, [w_out[:ka], w_out[ka:]], x2, mod[layer], mod[layer], mix_post_norm[layer],
                         ffn_pre_norm[layer], seq, 2, 3, 4, True, tm=512)
        hid = _ffn_up(h, ffn_w_gate, ffn_w_up, layer)
        last = layer == depth - 1
        nxt = layer if last else layer + 1
        x2, h = _mm_post([hid], [ffn_w_down[layer].astype(BF16)], x2, mod[layer], mod[nxt],
                         ffn_post_norm[layer], mix_pre_norm[nxt], seq, 5, 0, 1, not last, tm=256)
    return x2.reshape(batch, seq, d)
```

```python
import functools

import jax
import jax.numpy as jnp
import numpy as np
from jax import lax
from jax.experimental import pallas as pl
from jax.experimental.pallas import tpu as pltpu

F32 = jnp.float32
BF16 = jnp.bfloat16

D_MODEL = 2048
CHUNK = 64
EPS = 1e-6
ROPE_BASE = 10000.0
NEG_INF = -1e30

A_HEADS, A_DK, A_DV, A_CONV = 8, 128, 128, 4
B_HEADS, B_NOPE, B_ROPE, B_DV, B_Q_LORA, B_KV_LORA = 8, 128, 64, 128, 512, 512
C_HEADS, C_DK, C_DV = 8, 128, 128
D_HEADS, D_DH = 8, 128
LANES = 128
SUBLANES = 8
B_QK = B_NOPE + LANES

VMEM_LIMIT_BYTES = 56 * 1024 * 1024

RET_CHUNK = 128


def _cparams(n_axes):
    return pltpu.CompilerParams(dimension_semantics=("arbitrary",) * n_axes,
                                vmem_limit_bytes=VMEM_LIMIT_BYTES)


def _dot(a, b):
    return jnp.dot(a, b, preferred_element_type=F32)


def _dot_nt(a, b):
    return lax.dot_general(a, b, (((1,), (1,)), ((), ())), preferred_element_type=F32)


def _dot_tn(a, b):
    return lax.dot_general(a, b, (((0,), (0,)), ((), ())), preferred_element_type=F32)


def _split_bf16(a):
    hi = a.astype(BF16)
    lo = (a - hi.astype(F32)).astype(BF16)
    return hi, lo


def _dot3(a, b):
    ah, al = _split_bf16(a)
    bh, bl = _split_bf16(b)
    return _dot(ah, bh) + _dot(ah, bl) + _dot(al, bh)


def _sigmoid(x):
    return 1.0 / (1.0 + jnp.exp(-x))


def _silu(x):
    return x * _sigmoid(x)


def _softplus(x):
    return jnp.maximum(x, 0.0) + jnp.log(1.0 + jnp.exp(-jnp.abs(x)))


def _rms(x, w):
    return x * lax.rsqrt(jnp.mean(x * x, axis=-1, keepdims=True) + EPS) * w


def _ada_kernel(c_ref, w_ref, b_ref, o_ref):
    cond = _silu(c_ref[...]).astype(BF16)
    o_ref[...] = _dot(cond, w_ref[...].astype(BF16)) + b_ref[...]


def _ada(c_pad, ada_w, ada_b):
    depth, d, n = ada_w.shape
    rows = c_pad.shape[0]
    tn = 1024
    return pl.pallas_call(
        _ada_kernel,
        grid=(depth, n // tn),
        in_specs=[pl.BlockSpec((rows, d), lambda l, j: (0, 0)),
                  pl.BlockSpec((None, d, tn), lambda l, j: (l, 0, j)),
                  pl.BlockSpec((None, 1, tn), lambda l, j: (l, 0, j))],
        out_specs=pl.BlockSpec((None, rows, tn), lambda l, j: (l, 0, j)),
        out_shape=jax.ShapeDtypeStruct((depth, rows, n), F32),
        compiler_params=_cparams(2),
        name="ada_mod",
    )(c_pad, ada_w, ada_b.reshape(depth, 1, n))


def _rope_kernel(pos_ref, f_ref, sb_ref, sc_ref, cb_ref, snb_ref, cc_ref, snc_ref):
    ang = pos_ref[...] * f_ref[...]
    cos = jnp.cos(ang)
    sin = jnp.sin(ang)
    lane = lax.broadcasted_iota(jnp.int32, ang.shape, 1)
    nb, nc = B_ROPE // 2, C_DK // 2
    in_b = (lane < nb) | ((lane >= 2 * nb) & (lane < 3 * nb))

    def lay_b(x):
        return jnp.where(in_b, jnp.where(lane < nb, x, pltpu.roll(x, 2 * nb, 1)), 0.0)

    def lay_c(x):
        return jnp.where(lane < nc, pltpu.roll(x, LANES - nb, 1), pltpu.roll(x, nc - nb, 1))

    cb_ref[...] = lay_b(cos)
    snb_ref[...] = lay_b(sin) * sb_ref[...]
    cc_ref[...] = lay_c(cos)
    snc_ref[...] = lay_c(sin) * sc_ref[...]


def _rope_tables(pos_col):
    t = pos_col.shape[0]
    tb = min(1024, t)
    inv_b = ROPE_BASE ** (-jnp.arange(0, B_ROPE, 2, dtype=F32) / B_ROPE)
    inv_c = ROPE_BASE ** (-jnp.arange(0, C_DK, 2, dtype=F32) / C_DK)
    z32 = jnp.zeros((B_ROPE // 2,), F32)
    o32 = jnp.ones((B_ROPE // 2,), F32)
    o64 = jnp.ones((C_DK // 2,), F32)
    freqs = jnp.concatenate([inv_b, inv_c, z32])[None, :]
    sb = jnp.concatenate([-o32, z32, o32, z32])[None, :]
    sc = jnp.concatenate([-o64, o64])[None, :]
    row = pl.BlockSpec((1, LANES), lambda i: (0, 0))
    tab = pl.BlockSpec((tb, LANES), lambda i: (i, 0))
    shp = jax.ShapeDtypeStruct((t, LANES), F32)
    return pl.pallas_call(
        _rope_kernel,
        grid=(t // tb,),
        in_specs=[pl.BlockSpec((tb, 1), lambda i: (i, 0)), row, row, row],
        out_specs=[tab, tab, tab, tab],
        out_shape=[shp, shp, shp, shp],
        compiler_params=_cparams(1),
        name="rope_tables",
    )(pos_col, freqs, sb, sc)


def _prenorm_kernel(x_ref, mod_ref, w_ref, h_ref, *, shift_idx, scale_idx):
    y = _rms(x_ref[...], w_ref[...])
    h = y * (1.0 + mod_ref[scale_idx:scale_idx + 1, :]) + mod_ref[shift_idx:shift_idx + 1, :]
    h_ref[...] = h.astype(h_ref.dtype)


def _prenorm(x2, mod_l, w, seq, shift_idx, scale_idx):
    t, d = x2.shape
    tm = min(512, seq)
    bpb = seq // tm
    return pl.pallas_call(
        functools.partial(_prenorm_kernel, shift_idx=shift_idx, scale_idx=scale_idx),
        grid=(t // tm,),
        in_specs=[pl.BlockSpec((tm, d), lambda i: (i, 0)),
                  pl.BlockSpec((None, 6, d), lambda i: (i // bpb, 0, 0)),
                  pl.BlockSpec((1, d), lambda i: (0, 0))],
        out_specs=pl.BlockSpec((tm, d), lambda i: (i, 0)),
        out_shape=jax.ShapeDtypeStruct((t, d), BF16),
        compiler_params=_cparams(1),
        name="prenorm",
    )(x2, mod_l, w.reshape(1, d))


def _mm_kernel(*refs, cast_w, scaled):
    a_ref, w_ref = refs[0], refs[1]
    s_ref = refs[2] if scaled else None
    o_ref = refs[3] if scaled else refs[2]
    if cast_w:
        wb_ref = refs[-1]

        @pl.when(pl.program_id(1) == 0)
        def _():
            wb_ref[...] = w_ref[...].astype(BF16)

        w = wb_ref[...]
    else:
        w = w_ref[...]
    y = _dot(a_ref[...], w)
    if scaled:
        y = y * s_ref[...]
    o_ref[...] = y.astype(o_ref.dtype)


def _mm(a, w, out_dtype, layer=None, col0=0, ncols=None, colscale=None, tm=512, tn=1024):
    m, k = a.shape
    n = w.shape[-1] if ncols is None else ncols
    tm = min(tm, m)
    tn = min(tn, n)
    jb0 = col0 // tn
    cast_w = w.dtype != BF16
    if layer is None:
        wspec = pl.BlockSpec((k, tn), lambda j, i: (0, jb0 + j))
    else:
        wspec = pl.BlockSpec((None, k, tn), lambda j, i: (layer, 0, jb0 + j))
    in_specs = [pl.BlockSpec((tm, k), lambda j, i: (i, 0)), wspec]
    args = [a, w]
    if colscale is not None:
        in_specs.append(pl.BlockSpec((1, tn), lambda j, i: (0, j)))
        args.append(colscale.reshape(1, n).astype(F32))
    return pl.pallas_call(
        functools.partial(_mm_kernel, cast_w=cast_w, scaled=colscale is not None),
        grid=(n // tn, m // tm),
        in_specs=in_specs,
        out_specs=pl.BlockSpec((tm, tn), lambda j, i: (i, j)),
        out_shape=jax.ShapeDtypeStruct((m, n), out_dtype),
        scratch_shapes=[pltpu.VMEM((k, tn), BF16)] if cast_w else [],
        compiler_params=_cparams(2),
        name="matmul",
    )(*args)


MM_POST_SPLIT = 2


def _mm_post_kernel(*refs, n_in, cast_w, gate_idx, nshift_idx, nscale_idx, emit_h):
    a_refs = refs[:n_in]
    w_refs = refs[n_in:2 * n_in]
    x_ref, mod_ref, modn_ref, pw_ref, nw_ref = refs[2 * n_in:2 * n_in + 5]
    n_out = 2 if emit_h else 1
    outs = refs[2 * n_in + 5:2 * n_in + 5 + n_out]
    if cast_w:
        wb_refs = refs[2 * n_in + 5 + n_out:]

        @pl.when(pl.program_id(0) == 0)
        def _():
            for w_ref, wb_ref in zip(w_refs, wb_refs):
                wb_ref[...] = w_ref[...].astype(BF16)

        w_refs = wb_refs
    tm = x_ref.shape[0]
    for r0 in range(0, tm, tm // MM_POST_SPLIT):
        rows = slice(r0, r0 + tm // MM_POST_SPLIT)
        y = _dot(a_refs[0][rows, :], w_refs[0][...])
        for a_ref, w_ref in zip(a_refs[1:], w_refs[1:]):
            y = y + _dot(a_ref[rows, :], w_ref[...])
        xn = x_ref[rows, :] + mod_ref[gate_idx:gate_idx + 1, :] * _rms(y, pw_ref[...])
        outs[0][rows, :] = xn
        if emit_h:
            hn = _rms(xn, nw_ref[...])
            hn = hn * (1.0 + modn_ref[nscale_idx:nscale_idx + 1, :]) + modn_ref[nshift_idx:nshift_idx + 1, :]
            outs[1][rows, :] = hn.astype(outs[1].dtype)


def _mm_post(a_list, w, w_layer, x2, mod_l, modn_l, post_w, next_w, seq, gate_idx, nshift_idx, nscale_idx, emit_h,
             tm):
    m = a_list[0].shape[0]
    d = w.shape[-1]
    ka = a_list[0].shape[1]
    tm = min(tm, seq)
    bpb = seq // tm
    cast_w = w_layer is not None
    row = pl.BlockSpec((1, d), lambda i: (0, 0))
    modspec = pl.BlockSpec((None, 6, d), lambda i: (i // bpb, 0, 0))
    xspec = pl.BlockSpec((tm, d), lambda i: (i, 0))
    a_specs = [pl.BlockSpec((tm, ka), lambda i: (i, 0)) for _ in a_list]
    if cast_w:
        w_specs = [pl.BlockSpec((None, ka, d), lambda i, r=r: (w_layer, r, 0), pipeline_mode=pl.Buffered(1))
                   for r in range(len(a_list))]
    else:
        w_specs = [pl.BlockSpec((ka, d), lambda i, r=r: (r, 0), pipeline_mode=pl.Buffered(1))
                   for r in range(len(a_list))]
    out_specs = [xspec]
    out_shape = [jax.ShapeDtypeStruct((m, d), F32)]
    if emit_h:
        out_specs.append(xspec)
        out_shape.append(jax.ShapeDtypeStruct((m, d), BF16))
    res = pl.pallas_call(
        functools.partial(_mm_post_kernel, n_in=len(a_list), cast_w=cast_w, gate_idx=gate_idx,
                          nshift_idx=nshift_idx, nscale_idx=nscale_idx, emit_h=emit_h),
        grid=(m // tm,),
        in_specs=a_specs + w_specs + [xspec, modspec, modspec, row, row],
        out_specs=out_specs,
        out_shape=out_shape,
        scratch_shapes=[pltpu.VMEM((ka, d), BF16) for _ in a_list] if cast_w else [],
        compiler_params=_cparams(1),
        name="matmul_post",
    )(*a_list, *([w] * len(a_list)), x2, mod_l, modn_l, post_w.reshape(1, d), next_w.reshape(1, d))
    return res if emit_h else (res[0], None)


def _ffn_up_kernel(a_ref, wg_ref, wu_ref, o_ref, wgb_ref, wub_ref):
    @pl.when(pl.program_id(1) == 0)
    def _():
        wgb_ref[...] = wg_ref[...].astype(BF16)
        wub_ref[...] = wu_ref[...].astype(BF16)

    a = a_ref[...]
    g = _dot(a, wgb_ref[...])
    u = _dot(a, wub_ref[...])
    o_ref[...] = (_silu(g) * u).astype(o_ref.dtype)


def _ffn_up(a, wg, wu, layer, tm=512, tn=512):
    m, k = a.shape
    n = wg.shape[-1]
    tm = min(tm, m)
    wspec = pl.BlockSpec((None, k, tn), lambda j, i: (layer, 0, j))
    return pl.pallas_call(
        _ffn_up_kernel,
        grid=(n // tn, m // tm),
        in_specs=[pl.BlockSpec((tm, k), lambda j, i: (i, 0)), wspec, wspec],
        out_specs=pl.BlockSpec((tm, tn), lambda j, i: (i, j)),
        out_shape=jax.ShapeDtypeStruct((m, n), BF16),
        scratch_shapes=[pltpu.VMEM((k, tn), BF16), pltpu.VMEM((k, tn), BF16)],
        compiler_params=_cparams(2),
        name="ffn_up",
    )(a, wg, wu)


def _group_slabs(x, o_ref, heads_per_group):
    for g in range(o_ref.shape[0]):
        shift = (LANES - g * heads_per_group) % LANES
        o_ref[g] = x if shift == 0 else pltpu.roll(x, shift, 1)


def _gdn_gate_kernel(s_ref, alog_ref, dt_ref, o_ref, *, hg):
    s = s_ref[...]
    g = -jnp.exp(alog_ref[...]) * _softplus(s + dt_ref[...])
    beta = _sigmoid(s)
    lane = lax.broadcasted_iota(jnp.int32, s.shape, 1)
    _group_slabs(jnp.where(lane < A_HEADS, g, beta), o_ref, hg)


def _gdn_gates(slab_src, col_block, a_log, dt_bias, hg):
    t = slab_src.shape[0]
    tb = min(1024, t)
    ng = A_HEADS // hg
    pad = jnp.zeros((LANES - A_HEADS,), F32)
    alog_row = jnp.concatenate([a_log.astype(F32), pad])[None, :]
    dt_row = jnp.concatenate([dt_bias.astype(F32), pad])[None, :]
    row = pl.BlockSpec((1, LANES), lambda i: (0, 0))
    return pl.pallas_call(
        functools.partial(_gdn_gate_kernel, hg=hg),
        grid=(t // tb,),
        in_specs=[pl.BlockSpec((tb, LANES), lambda i: (i, col_block)), row, row],
        out_specs=pl.BlockSpec((ng, tb, LANES), lambda i: (0, i, 0)),
        out_shape=jax.ShapeDtypeStruct((ng, t, LANES), F32),
        compiler_params=_cparams(1),
        name="gdn_gates",
    )(slab_src, alog_row, dt_row)


GDN_HG = 4
GDN_TB = 256


def _unit_lower_inverses(n_mats):
    c = n_mats[0].shape[0]
    r = lax.broadcasted_iota(jnp.int32, (c, c), 0)
    col = lax.broadcasted_iota(jnp.int32, (c, c), 1)
    eye = jnp.where(r == col, 1.0, 0.0)

    def dot1(a, b):
        return _dot(a.astype(BF16), b.astype(BF16))

    ps = [eye - n for n in n_mats]
    ms = [dot1(n, n) for n in n_mats]
    power = 2
    while True:
        ps = [p + dot1(p, m) for p, m in zip(ps, ms)]
        power *= 2
        if power >= c:
            break
        ms = [dot1(m, m) for m in ms]
    res = [eye - p - _dot3(n, p) for n, p in zip(n_mats, ps)]
    return [p + dot1(p, e) for p, e in zip(ps, res)]


def _gdn_kernel(q_ref, k_ref, v_ref, z_ref, cwq_ref, cwk_ref, cwv_ref, gates_ref, grow_ref, onw_ref,
                o_ref, xq_ref, xk_ref, xv_ref, st_ref, *, tb, hg):
    j = pl.program_id(2)

    @pl.when(j == 0)
    def _():
        st_ref[...] = jnp.zeros_like(st_ref)
        for h_ref in (xq_ref, xk_ref, xv_ref):
            h_ref[...] = jnp.zeros_like(h_ref)

    def conv_silu(src_ref, hist_ref, cw_ref):
        x = src_ref[...]
        prev = hist_ref[...]
        hrows = prev.shape[0]
        row = lax.broadcasted_iota(jnp.int32, prev.shape, 0)
        acc = x * cw_ref[A_CONV - 1:A_CONV, :]
        for s in range(1, A_CONV):
            xs = pltpu.roll(x, s, 0)
            top = jnp.where(row < s, pltpu.roll(prev, s, 0), xs[0:hrows])
            shifted = jnp.concatenate([top, xs[hrows:]], axis=0)
            acc = acc + shifted * cw_ref[A_CONV - 1 - s:A_CONV - s, :]
        hist_ref[...] = x[tb - hrows:tb]
        return _silu(acc)

    q_all = conv_silu(q_ref, xq_ref, cwq_ref)
    k_all = conv_silu(k_ref, xk_ref, cwk_ref)
    v_all = conv_silu(v_ref, xv_ref, cwv_ref)

    c = CHUNK
    r = lax.broadcasted_iota(jnp.int32, (c, c), 0)
    col = lax.broadcasted_iota(jnp.int32, (c, c), 1)
    tri = r >= col
    strict = r > col

    nch = tb // c
    items = [(hh, ci) for hh in range(hg) for ci in range(nch)]
    qn, kn, vn = [], [], []
    for hh in range(hg):
        lanes = slice(hh * A_DK, (hh + 1) * A_DK)
        qh = q_all[:, lanes]
        kh = k_all[:, lanes]
        qn.append(qh * lax.rsqrt(jnp.sum(qh * qh, axis=-1, keepdims=True) + EPS) * (A_DK ** -0.5))
        kn.append(kh * lax.rsqrt(jnp.sum(kh * kh, axis=-1, keepdims=True) + EPS))
        vn.append(v_all[:, lanes])

    def rows_of(ci):
        return slice(ci * c, (ci + 1) * c)

    qc = [qn[hh][rows_of(ci)] for hh, ci in items]
    kc = [kn[hh][rows_of(ci)] for hh, ci in items]
    vc = [vn[hh][rows_of(ci)] for hh, ci in items]
    b_col = [gates_ref[rows_of(ci), A_HEADS + hh:A_HEADS + hh + 1] for hh, ci in items]
    gc_col = [jnp.sum(jnp.where(tri, grow_ref[hh][:, rows_of(ci)], 0.0), axis=1, keepdims=True)
              for hh, ci in items]
    gc_row = [jnp.sum(jnp.where(r <= col, gates_ref[rows_of(ci), hh:hh + 1], 0.0), axis=0, keepdims=True)
              for hh, ci in items]
    decay = [jnp.where(tri, jnp.exp(jnp.minimum(a - b, 0.0)), 0.0) for a, b in zip(gc_col, gc_row)]
    e_gc = [jnp.exp(a) for a in gc_col]
    g_last = [a[c - 1:c, :] for a in gc_col]
    kb = [x.astype(BF16) for x in kc]
    kk = [_dot_nt(x, x) for x in kb]
    qk = [_dot_nt(x.astype(BF16), y) for x, y in zip(qc, kb)]
    n_mat = [jnp.where(strict, b * x * d, 0.0) for b, x, d in zip(b_col, kk, decay)]
    qk = [jnp.where(tri, x * d, 0.0).astype(BF16) for x, d in zip(qk, decay)]
    t_inv = _unit_lower_inverses(n_mat)
    wu = [_dot3(t, jnp.concatenate([k * (b * e), v * b], axis=1)).astype(BF16)
          for t, k, v, b, e in zip(t_inv, kc, vc, b_col, e_gc)]
    kdb = [(k * jnp.exp(gl - g)).astype(BF16) for k, gl, g in zip(kc, g_last, gc_col)]
    mb = [_dot_tn(kd, x) for kd, x in zip(kdb, wu)]
    m_c = [x[:, :A_DK].astype(BF16) for x in mb]
    b_c = [x[:, A_DK:] for x in mb]
    qwu = [_dot(a, x) for a, x in zip(qk, wu)]
    q_eff = [(q * e - x[:, :A_DK]).astype(BF16) for q, e, x in zip(qc, e_gc, qwu)]
    qku = [x[:, A_DK:] for x in qwu]
    e_gl = [jnp.exp(x) for x in g_last]
    states = [st_ref[hh] for hh in range(hg)]
    o_c = [None] * len(items)
    for ci in range(nch):
        idx = [hh * nch + ci for hh in range(hg)]
        sb = [s.astype(BF16) for s in states]
        for hh in range(hg):
            o_c[idx[hh]] = _dot(q_eff[idx[hh]], sb[hh]) + qku[idx[hh]]
        states = [states[hh] * e_gl[idx[hh]] + b_c[idx[hh]] - _dot(m_c[idx[hh]], sb[hh]) for hh in range(hg)]
    for hh in range(hg):
        st_ref[hh] = states[hh]
    for i, (hh, ci) in enumerate(items):
        lanes = slice(hh * A_DK, (hh + 1) * A_DK)
        zc = z_ref[rows_of(ci), lanes]
        o_ref[rows_of(ci), lanes] = (_rms(o_c[i], onw_ref[...]) * _silu(zc)).astype(o_ref.dtype)


def _gdn(proj, conv_w, gates, g_row, out_norm, batch, seq):
    t = proj.shape[0]
    tb = min(GDN_TB, seq)
    hg = GDN_HG
    w = hg * A_DK
    nj = seq // tb
    ng = A_HEADS // hg
    kq = A_HEADS * A_DK
    off_k = kq // w
    off_v = 2 * kq // w
    off_z = 3 * kq // w

    def colspec(off):
        return pl.BlockSpec((tb, w), lambda b, g, j: (b * nj + j, off + g))

    def cwspec(off):
        return pl.BlockSpec((A_CONV, w), lambda b, g, j: (0, off + g))

    colv = pl.BlockSpec((None, tb, LANES), lambda b, g, j: (g, b * nj + j, 0))
    rowv = pl.BlockSpec((hg, 1, tb), lambda b, g, j: (g, 0, b * nj + j))
    return pl.pallas_call(
        functools.partial(_gdn_kernel, tb=tb, hg=hg),
        grid=(batch, ng, nj),
        in_specs=[colspec(0), colspec(off_k), colspec(off_v), colspec(off_z),
                  cwspec(0), cwspec(off_k), cwspec(off_v), colv, rowv,
                  pl.BlockSpec((1, A_DV), lambda b, g, j: (0, 0))],
        out_specs=pl.BlockSpec((tb, w), lambda b, g, j: (b * nj + j, g)),
        out_shape=jax.ShapeDtypeStruct((t, A_HEADS * A_DV), BF16),
        scratch_shapes=[pltpu.VMEM((SUBLANES, w), F32), pltpu.VMEM((SUBLANES, w), F32),
                        pltpu.VMEM((SUBLANES, w), F32), pltpu.VMEM((hg, A_DK, A_DV), F32)],
        compiler_params=_cparams(3),
        name="gated_deltanet",
    )(proj, proj, proj, proj, conv_w, conv_w, conv_w, gates, g_row, out_norm.reshape(1, A_DV))


def _rope_lanes(x, cos, sin_signed):
    return x * cos + pltpu.roll(x, LANES // 2, 1) * sin_signed


def _mla_q_kernel(cq_ref, nw_ref, w_ref, cos_ref, sin_ref, o_ref):
    n = _rms(cq_ref[...], nw_ref[...]).astype(BF16)
    cos = cos_ref[...]
    sin = sin_ref[...]
    for h in range(B_HEADS):
        qh = _dot(n, w_ref[:, h * B_QK:(h + 1) * B_QK])
        o_ref[:, h * B_QK:h * B_QK + B_NOPE] = qh[:, :B_NOPE].astype(o_ref.dtype)
        o_ref[:, h * B_QK + B_NOPE:(h + 1) * B_QK] = _rope_lanes(qh[:, B_NOPE:], cos, sin).astype(o_ref.dtype)


def _mla_q(proj, col_block, q_norm, w_uq_p, cos_b, sin_b):
    t = proj.shape[0]
    tm = min(512, t)
    n = w_uq_p.shape[1]
    return pl.pallas_call(
        _mla_q_kernel,
        grid=(t // tm,),
        in_specs=[pl.BlockSpec((tm, B_Q_LORA), lambda i: (i, col_block)),
                  pl.BlockSpec((1, B_Q_LORA), lambda i: (0, 0)),
                  pl.BlockSpec((B_Q_LORA, n), lambda i: (0, 0)),
                  pl.BlockSpec((tm, LANES), lambda i: (i, 0)),
                  pl.BlockSpec((tm, LANES), lambda i: (i, 0))],
        out_specs=pl.BlockSpec((tm, n), lambda i: (i, 0)),
        out_shape=jax.ShapeDtypeStruct((t, n), BF16),
        compiler_params=_cparams(1),
        name="mla_q_up",
    )(proj, q_norm.reshape(1, B_Q_LORA), w_uq_p, cos_b, sin_b)


def _mla_kv_kernel(ckv_ref, nw_ref, w_ref, kpe_ref, cos_ref, sin_ref, k_ref, v_ref):
    n = _rms(ckv_ref[...], nw_ref[...]).astype(BF16)
    kpe = _rope_lanes(kpe_ref[...], cos_ref[...], sin_ref[...]).astype(k_ref.dtype)
    nk = B_HEADS * B_NOPE
    kn = _dot(n, w_ref[:, :nk])
    for h in range(B_HEADS):
        k_ref[:, h * B_QK:h * B_QK + B_NOPE] = kn[:, h * B_NOPE:(h + 1) * B_NOPE].astype(k_ref.dtype)
        k_ref[:, h * B_QK + B_NOPE:(h + 1) * B_QK] = kpe
    v_ref[...] = _dot(n, w_ref[:, nk:]).astype(v_ref.dtype)


def _mla_kv(proj, col_block, kv_norm, w_ukv_p, kpe_block, cos_b, sin_b):
    t = proj.shape[0]
    tm = min(512, t)
    n = w_ukv_p.shape[1]
    tab = pl.BlockSpec((tm, LANES), lambda i: (i, 0))
    kpe_spec = pl.BlockSpec((tm, LANES), lambda i: (i, kpe_block))
    return pl.pallas_call(
        _mla_kv_kernel,
        grid=(t // tm,),
        in_specs=[pl.BlockSpec((tm, B_KV_LORA), lambda i: (i, col_block)),
                  pl.BlockSpec((1, B_KV_LORA), lambda i: (0, 0)),
                  pl.BlockSpec((B_KV_LORA, n), lambda i: (0, 0)),
                  kpe_spec, tab, tab],
        out_specs=[pl.BlockSpec((tm, B_HEADS * B_QK), lambda i: (i, 0)),
                   pl.BlockSpec((tm, B_HEADS * B_DV), lambda i: (i, 0))],
        out_shape=[jax.ShapeDtypeStruct((t, B_HEADS * B_QK), BF16),
                   jax.ShapeDtypeStruct((t, B_HEADS * B_DV), BF16)],
        compiler_params=_cparams(1),
        name="mla_kv_up",
    )(proj, kv_norm.reshape(1, B_KV_LORA), w_ukv_p, proj, cos_b, sin_b)


ATTN_T = 512


def _attn_kernel(*refs, tq, nq, dk, dv, hg, mask_gran, has_bias):
    if has_bias:
        q_ref, k_ref, v_ref, cq_ref, ck_ref, o_ref, m_ref, acc_ref, vt_ref = refs
    else:
        q_ref, k_ref, v_ref, o_ref, m_ref, acc_ref, vt_ref = refs
    qi = pl.program_id(2)

    @pl.when(qi == 0)
    def _():
        for hh in range(hg):
            for jb in range(nq):
                vblk = v_ref[jb * tq:(jb + 1) * tq, hh * dv:(hh + 1) * dv]
                vt_ref[hh, jb, 0:dv, :] = vblk.astype(F32).T.astype(BF16)
                vt_ref[hh, jb, dv:, :] = jnp.ones((ATTN_ONES, tq), BF16)

    m_ref[...] = jnp.full(m_ref.shape, NEG_INF, F32)
    acc_ref[...] = jnp.zeros_like(acc_ref)
    heads = range(hg)

    def step(jb, masked):
        start = pl.multiple_of(jb * tq, tq)
        s = [_dot_nt(k_ref[pl.ds(start, tq), hh * dk:(hh + 1) * dk], q_ref[:, hh * dk:(hh + 1) * dk])
             for hh in heads]
        if has_bias:
            s = [s[hh] - ck_ref[pl.ds(start, tq), hh:hh + 1] for hh in heads]
        if masked:
            r = lax.broadcasted_iota(jnp.int32, (tq, tq), 0)
            c = lax.broadcasted_iota(jnp.int32, (tq, tq), 1)
            if mask_gran > 1:
                shift = int(np.log2(mask_gran))
                r = lax.shift_right_logical(r, shift)
                c = lax.shift_right_logical(c, shift)
            keep = r <= c
            s = [jnp.where(keep, s[hh], NEG_INF) for hh in heads]
        m_prev = [m_ref[hh] for hh in heads]
        m_cur = [jnp.max(s[hh], axis=0, keepdims=True) for hh in heads]
        if has_bias:
            m_cur = [m_cur[hh] + cq_ref[hh] for hh in heads]
        m_new = [jnp.maximum(m_prev[hh], m_cur[hh]) for hh in heads]
        sub = [m_new[hh] - cq_ref[hh] for hh in heads] if has_bias else m_new
        p = [jnp.exp2(s[hh] - sub[hh]).astype(BF16) for hh in heads]
        alpha = [jnp.exp2(m_prev[hh] - m_new[hh]) for hh in heads]
        for hh in heads:
            m_ref[hh] = m_new[hh]
        pv = [_dot(vt_ref[hh, jb], p[hh]) for hh in heads]
        for hh in heads:
            acc_ref[hh] = alpha[hh] * acc_ref[hh] + pv[hh]

    def body(jb, carry):
        step(jb, False)
        return carry

    lax.fori_loop(0, qi, body, 0)
    step(qi, True)
    for hh in heads:
        acc = acc_ref[hh]
        o_ref[:, hh * dv:(hh + 1) * dv] = (acc[:dv] / acc[dv:dv + 1]).T.astype(o_ref.dtype)


ATTN_HG = 4
ATTN_ONES = 16
LOG2E = 1.4426950408889634


def _attention(q, k, v, dk, dv, heads, batch, seq, mask_gran, q_off=0, k_off=0, v_off=0, bias=None):
    t = q.shape[0]
    tq = min(ATTN_T, seq)
    nq = seq // tq
    hg = ATTN_HG
    qb, kb, vb = q_off // (hg * dk), k_off // (hg * dk), v_off // (hg * dv)
    dvx = dv + ATTN_ONES
    in_specs = [pl.BlockSpec((tq, hg * dk), lambda b, g, i: (b * nq + i, qb + g)),
                pl.BlockSpec((seq, hg * dk), lambda b, g, i: (b, kb + g)),
                pl.BlockSpec((seq, hg * dv), lambda b, g, i: (b, vb + g))]
    args = [q, k, v]
    if bias is not None:
        cq, ck = bias
        in_specs += [pl.BlockSpec((None, hg, 1, tq), lambda b, g, i: (b, g, 0, i)),
                     pl.BlockSpec((None, seq, LANES), lambda b, g, i: (g, b, 0))]
        args += [cq, ck]
    return pl.pallas_call(
        functools.partial(_attn_kernel, tq=tq, nq=nq, dk=dk, dv=dv, hg=hg, mask_gran=mask_gran,
                          has_bias=bias is not None),
        grid=(batch, heads // hg, nq),
        in_specs=in_specs,
        out_specs=pl.BlockSpec((tq, hg * dv), lambda b, g, i: (b * nq + i, g)),
        out_shape=jax.ShapeDtypeStruct((t, heads * dv), BF16),
        scratch_shapes=[pltpu.VMEM((hg, 1, tq), F32), pltpu.VMEM((hg, dvx, tq), F32),
                        pltpu.VMEM((hg, nq, dvx, tq), BF16)],
        compiler_params=_cparams(3),
        name="flash_attention",
    )(*args)


RET_HG = 2
RET_TB = 512


def _ret_kernel(q_ref, k_ref, v_ref, g_ref, cos_ref, sin_ref, dec_ref, xi_ref, zeta_ref, gch_ref, gnw_ref,
                o_ref, st_ref, *, tb, hg, c):
    j = pl.program_id(2)

    @pl.when(j == 0)
    def _():
        st_ref[...] = jnp.zeros_like(st_ref)

    cos = cos_ref[...]
    sin = sin_ref[...]
    nch = tb // c
    items = [(hh, ci) for hh in range(hg) for ci in range(nch)]

    def lanes_of(hh):
        return slice(hh * C_DK, (hh + 1) * C_DK)

    def rows_of(ci):
        return slice(ci * c, (ci + 1) * c)

    qh = [_rope_lanes(q_ref[:, lanes_of(hh)], cos, sin) for hh in range(hg)]
    kh = [_rope_lanes(k_ref[:, lanes_of(hh)], cos, sin) for hh in range(hg)]
    qb = [qh[hh][rows_of(ci)].astype(BF16) for hh, ci in items]
    kc = [kh[hh][rows_of(ci)] for hh, ci in items]
    vb = [v_ref[rows_of(ci), lanes_of(hh)].astype(BF16) for hh, ci in items]
    scores = [(_dot_nt(q, k.astype(BF16)) * dec_ref[hh]).astype(BF16) for q, k, (hh, _) in zip(qb, kc, items)]
    inner = [_dot(s, v) for s, v in zip(scores, vb)]
    kzv = [_dot_tn((k * zeta_ref[hh]).astype(BF16), v) for k, v, (hh, _) in zip(kc, vb, items)]
    states = []
    for hh in range(hg):
        state = st_ref[hh]
        for ci in range(nch):
            states.append(state)
            state = state * gch_ref[hh] + kzv[hh * nch + ci]
        st_ref[hh] = state
    cross = [_dot(q, s.astype(BF16)) * xi_ref[hh] for q, s, (hh, _) in zip(qb, states, items)]
    for i, (hh, ci) in enumerate(items):
        o_c = inner[i] + cross[i]
        gate = _silu(g_ref[rows_of(ci), lanes_of(hh)])
        o_ref[rows_of(ci), lanes_of(hh)] = (_rms(o_c, gnw_ref[hh]) * gate).astype(o_ref.dtype)


def _retention(proj, cos_c, sin_c, group_norm, batch, seq):
    t = proj.shape[0]
    tb = min(RET_TB, seq)
    c = min(RET_CHUNK, seq)
    hg = RET_HG
    w = hg * C_DK
    nj = seq // tb
    ng = C_HEADS // hg
    nh = C_HEADS * C_DK // w

    log_gamma = jnp.log1p(-jnp.exp2(-5.0 - jnp.arange(C_HEADS, dtype=F32)))
    idx = jnp.arange(c, dtype=F32)
    rel = idx[:, None] - idx[None, :]
    dec = jnp.where(rel >= 0, jnp.exp(log_gamma[:, None, None] * jnp.maximum(rel, 0.0)), 0.0)
    xi = jnp.exp(log_gamma[:, None] * (idx + 1.0))
    zeta = jnp.exp(log_gamma[:, None] * (c - 1.0 - idx))
    gch = jnp.exp(log_gamma * c)
    xi_b = jnp.broadcast_to(xi[:, :, None], (C_HEADS, c, C_DV))
    zeta_b = jnp.broadcast_to(zeta[:, :, None], (C_HEADS, c, C_DK))
    gch_b = jnp.broadcast_to(gch[:, None, None], (C_HEADS, 1, C_DV))

    def colspec(off):
        return pl.BlockSpec((tb, w), lambda b, g, j: (b * nj + j, off + g))

    tab = pl.BlockSpec((tb, LANES), lambda b, g, j: (b * nj + j, 0))

    def hspec(r, cdim):
        return pl.BlockSpec((hg, r, cdim), lambda b, g, j: (g, 0, 0))

    return pl.pallas_call(
        functools.partial(_ret_kernel, tb=tb, hg=hg, c=c),
        grid=(batch, ng, nj),
        in_specs=[colspec(0), colspec(nh), colspec(2 * nh), colspec(3 * nh), tab, tab,
                  hspec(c, c), hspec(c, C_DV), hspec(c, C_DK), hspec(1, C_DV), hspec(1, C_DV)],
        out_specs=pl.BlockSpec((tb, w), lambda b, g, j: (b * nj + j, g)),
        out_shape=jax.ShapeDtypeStruct((t, C_HEADS * C_DV), BF16),
        scratch_shapes=[pltpu.VMEM((hg, C_DK, C_DV), F32)],
        compiler_params=_cparams(3),
        name="retention",
    )(proj, proj, proj, proj, cos_c, sin_c, dec, xi_b, zeta_b, gch_b, group_norm.reshape(C_HEADS, 1, C_DV))


def _fox_cum_kernel(s_ref, b_ref, o_ref, *, seq, hg):
    x = s_ref[...] + b_ref[...]
    acc = -_softplus(-x)
    row = lax.broadcasted_iota(jnp.int32, acc.shape, 0)
    shift = 1
    while shift < seq:
        acc = acc + jnp.where(row >= shift, pltpu.roll(acc, shift, 0), 0.0)
        shift *= 2
    _group_slabs(acc * LOG2E, o_ref, hg)


def _fox_cum(slab, forget_bias, batch, seq, hg):
    t = slab.shape[0]
    ng = D_HEADS // hg
    pad = jnp.zeros((LANES - D_HEADS,), F32)
    b_row = jnp.concatenate([forget_bias.astype(F32), pad])[None, :]
    return pl.pallas_call(
        functools.partial(_fox_cum_kernel, seq=seq, hg=hg),
        grid=(batch,),
        in_specs=[pl.BlockSpec((seq, LANES), lambda b: (b, 0)),
                  pl.BlockSpec((1, LANES), lambda b: (0, 0))],
        out_specs=pl.BlockSpec((ng, seq, LANES), lambda b: (0, b, 0)),
        out_shape=jax.ShapeDtypeStruct((ng, t, LANES), F32),
        compiler_params=_cparams(1),
        name="fox_cumsum",
    )(slab, b_row)


def _pe_lanes(w_pe):
    k = w_pe.shape[0]
    z = jnp.zeros((k, B_ROPE // 2), w_pe.dtype)
    return jnp.concatenate([w_pe[:, :B_ROPE // 2], z, w_pe[:, B_ROPE // 2:], z], axis=1)


AB_MAIN = A_HEADS * (2 * A_DK + A_DV) + A_HEADS * A_DV


def _ab_weights(w_in, w_uq, w_ukv):
    o = AB_MAIN
    w_a = w_in[:, o:o + A_HEADS]; o += A_HEADS
    w_b = w_in[:, o:o + A_HEADS]; o += A_HEADS
    w_cq = w_in[:, o:o + B_Q_LORA]; o += B_Q_LORA
    w_ckv = w_in[:, o:o + B_KV_LORA]; o += B_KV_LORA
    w_kpe = w_in[:, o:o + B_ROPE]
    d = w_in.shape[0]
    w_tail = jnp.concatenate([w_cq, w_ckv, _pe_lanes(w_kpe), w_a, w_b,
                              jnp.zeros((d, LANES - 2 * A_HEADS), w_in.dtype)], axis=1).astype(BF16)
    scale = (B_NOPE + B_ROPE) ** -0.5 * LOG2E
    wq = (w_uq * scale).reshape(B_Q_LORA, B_HEADS, B_NOPE + B_ROPE)
    wq_p = jnp.concatenate(
        [wq[:, :, :B_NOPE],
         _pe_lanes(wq[:, :, B_NOPE:].reshape(B_Q_LORA * B_HEADS, B_ROPE)).reshape(B_Q_LORA, B_HEADS, LANES)],
        axis=2).reshape(B_Q_LORA, B_HEADS * B_QK).astype(BF16)
    wkv = w_ukv.reshape(B_KV_LORA, B_HEADS, B_NOPE + B_DV)
    wkv_p = jnp.concatenate([wkv[:, :, :B_NOPE].reshape(B_KV_LORA, B_HEADS * B_NOPE),
                             wkv[:, :, B_NOPE:].reshape(B_KV_LORA, B_HEADS * B_DV)], axis=1).astype(BF16)
    return w_tail, wq_p, wkv_p


def _layer_ab(h, rope, w_in_all, j, conv_w, a_log, dt_bias, out_norm, q_norm, w_uq, kv_norm, w_ukv, batch, seq):
    cos_b, sin_b, _, _ = rope
    w_tail, wq_p, wkv_p = _ab_weights(w_in_all[j], w_uq, w_ukv)
    proj = _mm(h, w_in_all, F32, layer=j, ncols=AB_MAIN)
    tail = _mm(h, w_tail, F32, tn=w_tail.shape[1])
    cq_block, ckv_block = 0, 1
    kpe_block = (B_Q_LORA + B_KV_LORA) // LANES
    gates = _gdn_gates(tail, kpe_block + 1, a_log, dt_bias, GDN_HG)
    g_row = gates[0][:, :A_HEADS].T.reshape(A_HEADS, 1, h.shape[0])
    o_a = _gdn(proj, conv_w, gates, g_row, out_norm, batch, seq)
    q_b = _mla_q(tail, cq_block, q_norm, wq_p, cos_b, sin_b)
    k_b, v_b = _mla_kv(tail, ckv_block, kv_norm, wkv_p, kpe_block, cos_b, sin_b)
    o_b = _attention(q_b, k_b, v_b, B_QK, B_DV, B_HEADS, batch, seq, CHUNK)
    return [o_a, o_b]


def _layer_cd(h, rope, w_in_all, j, group_norm, forget_bias, batch, seq):
    _, _, cos_c, sin_c = rope
    n = C_HEADS * C_DK
    ones = jnp.ones((n,), F32)
    scale_c = jnp.concatenate([ones, ones * (C_DK ** -0.5), ones, ones])
    scale_d = jnp.concatenate([ones * (D_DH ** -0.5 * LOG2E), ones, ones])
    proj_c = _mm(h, w_in_all, F32, layer=j, ncols=4 * n, colscale=scale_c)
    proj_d = _mm(h, w_in_all, BF16, layer=j, col0=4 * n, ncols=3 * n, colscale=scale_d)
    w_f = jnp.concatenate([w_in_all[j][:, 7 * n:7 * n + D_HEADS],
                           jnp.zeros((w_in_all.shape[1], LANES - D_HEADS), w_in_all.dtype)], axis=1).astype(BF16)
    slab = _mm(h, w_f, F32, tn=LANES)
    o_c = _retention(proj_c, cos_c, sin_c, group_norm, batch, seq)
    ck = _fox_cum(slab, forget_bias, batch, seq, ATTN_HG)
    cq = ck[0][:, :D_HEADS].reshape(batch, seq, D_HEADS).transpose(0, 2, 1).reshape(batch, D_HEADS, 1, seq)
    o_d = _attention(proj_d, proj_d, proj_d, D_DH, D_DH, D_HEADS, batch, seq, 1,
                     q_off=0, k_off=D_HEADS * D_DH, v_off=2 * D_HEADS * D_DH, bias=(cq, ck))
    return [o_c, o_d]


def kernel(x, c, positions, ada_w, ada_b, mix_pre_norm, mix_post_norm, ffn_pre_norm, ffn_post_norm, ab_w_in, ab_conv_w, ab_a_log, ab_dt_bias, ab_out_norm, ab_q_norm, ab_w_uq, ab_kv_norm, ab_w_ukv, ab_w_out, cd_w_in, cd_group_norm, cd_forget_bias, cd_w_out, ffn_w_gate, ffn_w_up, ffn_w_down):
    batch, seq, d = x.shape
    depth = ada_w.shape[0]
    t = batch * seq
    rows = 8
    c_pad = jnp.concatenate([c, jnp.zeros((rows - batch, d), c.dtype)], axis=0)
    mod = _ada(c_pad, ada_w, ada_b).reshape(depth, rows, 6, d)
    rope = _rope_tables(positions.astype(F32).reshape(t, 1))
    x2 = x.reshape(t, d)
    h = _prenorm(x2, mod[0], mix_pre_norm[0], seq, 0, 1)
    for layer in range(depth):
        j = layer // 2
        if layer % 2 == 0:
            y_in = _layer_ab(h, rope, ab_w_in, j, ab_conv_w[j], ab_a_log[j], ab_dt_bias[j], ab_out_norm[j],
                             ab_q_norm[j], ab_w_uq[j], ab_kv_norm[j], ab_w_ukv[j], batch, seq)
            w_out = ab_w_out
        else:
            y_in = _layer_cd(h, rope, cd_w_in, j, cd_group_norm[j], cd_forget_bias[j], batch, seq)
            w_out = cd_w_out
        x2, h = _mm_post(y_in, w_out, j, x2, mod[layer], mod[layer], mix_post_norm[layer],
                         ffn_pre_norm[layer], seq, 2, 3, 4, True, tm=512)
        hid = _ffn_up(h, ffn_w_gate, ffn_w_up, layer)
        last = layer == depth - 1
        nxt = layer if last else layer + 1
        x2, h = _mm_post([hid], ffn_w_down[layer].astype(BF16), None, x2, mod[layer], mod[nxt],
                         ffn_post_norm[layer], mix_pre_norm[nxt], seq, 5, 0, 1, not last, tm=256)
    return x2.reshape(batch, seq, d)
```

```python
import functools

import jax
import jax.numpy as jnp
import numpy as np
from jax import lax
from jax.experimental import pallas as pl
from jax.experimental.pallas import tpu as pltpu

F32 = jnp.float32
BF16 = jnp.bfloat16

D_MODEL = 2048
CHUNK = 64
EPS = 1e-6
ROPE_BASE = 10000.0
NEG_INF = -1e30

A_HEADS, A_DK, A_DV, A_CONV = 8, 128, 128, 4
B_HEADS, B_NOPE, B_ROPE, B_DV, B_Q_LORA, B_KV_LORA = 8, 128, 64, 128, 512, 512
C_HEADS, C_DK, C_DV = 8, 128, 128
D_HEADS, D_DH = 8, 128
LANES = 128
SUBLANES = 8
B_QK = B_NOPE + LANES

VMEM_LIMIT_BYTES = 56 * 1024 * 1024

RET_CHUNK = 128


def _cparams(n_axes):
    return pltpu.CompilerParams(dimension_semantics=("arbitrary",) * n_axes,
                                vmem_limit_bytes=VMEM_LIMIT_BYTES)


def _dot(a, b):
    return jnp.dot(a, b, preferred_element_type=F32)


def _dot_nt(a, b):
    return lax.dot_general(a, b, (((1,), (1,)), ((), ())), preferred_element_type=F32)


def _dot_tn(a, b):
    return lax.dot_general(a, b, (((0,), (0,)), ((), ())), preferred_element_type=F32)


def _split_bf16(a):
    hi = a.astype(BF16)
    lo = (a - hi.astype(F32)).astype(BF16)
    return hi, lo


def _dot3(a, b):
    ah, al = _split_bf16(a)
    bh, bl = _split_bf16(b)
    return _dot(ah, bh) + _dot(ah, bl) + _dot(al, bh)


def _sigmoid(x):
    return 1.0 / (1.0 + jnp.exp(-x))


def _silu(x):
    return x * _sigmoid(x)


def _softplus(x):
    return jnp.maximum(x, 0.0) + jnp.log(1.0 + jnp.exp(-jnp.abs(x)))


def _rms(x, w):
    return x * lax.rsqrt(jnp.mean(x * x, axis=-1, keepdims=True) + EPS) * w


def _ada_kernel(c_ref, w_ref, b_ref, o_ref):
    cond = _silu(c_ref[...]).astype(BF16)
    o_ref[...] = _dot(cond, w_ref[...].astype(BF16)) + b_ref[...]


def _ada(c_pad, ada_w, ada_b):
    depth, d, n = ada_w.shape
    rows = c_pad.shape[0]
    tn = 1024
    return pl.pallas_call(
        _ada_kernel,
        grid=(depth, n // tn),
        in_specs=[pl.BlockSpec((rows, d), lambda l, j: (0, 0)),
                  pl.BlockSpec((None, d, tn), lambda l, j: (l, 0, j)),
                  pl.BlockSpec((None, 1, tn), lambda l, j: (l, 0, j))],
        out_specs=pl.BlockSpec((None, rows, tn), lambda l, j: (l, 0, j)),
        out_shape=jax.ShapeDtypeStruct((depth, rows, n), F32),
        compiler_params=_cparams(2),
        name="ada_mod",
    )(c_pad, ada_w, ada_b.reshape(depth, 1, n))


def _rope_kernel(pos_ref, f_ref, sb_ref, sc_ref, cb_ref, snb_ref, cc_ref, snc_ref):
    ang = pos_ref[...] * f_ref[...]
    cos = jnp.cos(ang)
    sin = jnp.sin(ang)
    lane = lax.broadcasted_iota(jnp.int32, ang.shape, 1)
    nb, nc = B_ROPE // 2, C_DK // 2
    in_b = (lane < nb) | ((lane >= 2 * nb) & (lane < 3 * nb))

    def lay_b(x):
        return jnp.where(in_b, jnp.where(lane < nb, x, pltpu.roll(x, 2 * nb, 1)), 0.0)

    def lay_c(x):
        return jnp.where(lane < nc, pltpu.roll(x, LANES - nb, 1), pltpu.roll(x, nc - nb, 1))

    cb_ref[...] = lay_b(cos)
    snb_ref[...] = lay_b(sin) * sb_ref[...]
    cc_ref[...] = lay_c(cos)
    snc_ref[...] = lay_c(sin) * sc_ref[...]


def _rope_tables(pos_col):
    t = pos_col.shape[0]
    tb = min(1024, t)
    inv_b = ROPE_BASE ** (-jnp.arange(0, B_ROPE, 2, dtype=F32) / B_ROPE)
    inv_c = ROPE_BASE ** (-jnp.arange(0, C_DK, 2, dtype=F32) / C_DK)
    z32 = jnp.zeros((B_ROPE // 2,), F32)
    o32 = jnp.ones((B_ROPE // 2,), F32)
    o64 = jnp.ones((C_DK // 2,), F32)
    freqs = jnp.concatenate([inv_b, inv_c, z32])[None, :]
    sb = jnp.concatenate([-o32, z32, o32, z32])[None, :]
    sc = jnp.concatenate([-o64, o64])[None, :]
    row = pl.BlockSpec((1, LANES), lambda i: (0, 0))
    tab = pl.BlockSpec((tb, LANES), lambda i: (i, 0))
    shp = jax.ShapeDtypeStruct((t, LANES), F32)
    return pl.pallas_call(
        _rope_kernel,
        grid=(t // tb,),
        in_specs=[pl.BlockSpec((tb, 1), lambda i: (i, 0)), row, row, row],
        out_specs=[tab, tab, tab, tab],
        out_shape=[shp, shp, shp, shp],
        compiler_params=_cparams(1),
        name="rope_tables",
    )(pos_col, freqs, sb, sc)


def _prenorm_kernel(x_ref, mod_ref, w_ref, h_ref, *, shift_idx, scale_idx):
    y = _rms(x_ref[...], w_ref[...])
    h = y * (1.0 + mod_ref[scale_idx:scale_idx + 1, :]) + mod_ref[shift_idx:shift_idx + 1, :]
    h_ref[...] = h.astype(h_ref.dtype)


def _prenorm(x2, mod_l, w, seq, shift_idx, scale_idx):
    t, d = x2.shape
    tm = min(512, seq)
    bpb = seq // tm
    return pl.pallas_call(
        functools.partial(_prenorm_kernel, shift_idx=shift_idx, scale_idx=scale_idx),
        grid=(t // tm,),
        in_specs=[pl.BlockSpec((tm, d), lambda i: (i, 0)),
                  pl.BlockSpec((None, 6, d), lambda i: (i // bpb, 0, 0)),
                  pl.BlockSpec((1, d), lambda i: (0, 0))],
        out_specs=pl.BlockSpec((tm, d), lambda i: (i, 0)),
        out_shape=jax.ShapeDtypeStruct((t, d), BF16),
        compiler_params=_cparams(1),
        name="prenorm",
    )(x2, mod_l, w.reshape(1, d))


def _mm_kernel(*refs, cast_w, scaled):
    a_ref, w_ref = refs[0], refs[1]
    s_ref = refs[2] if scaled else None
    o_ref = refs[3] if scaled else refs[2]
    if cast_w:
        wb_ref = refs[-1]

        @pl.when(pl.program_id(1) == 0)
        def _():
            wb_ref[...] = w_ref[...].astype(BF16)

        w = wb_ref[...]
    else:
        w = w_ref[...]
    y = _dot(a_ref[...], w)
    if scaled:
        y = y * s_ref[...]
    o_ref[...] = y.astype(o_ref.dtype)


def _mm(a, w, out_dtype, layer=None, col0=0, ncols=None, colscale=None, tm=512, tn=1024):
    m, k = a.shape
    n = w.shape[-1] if ncols is None else ncols
    tm = min(tm, m)
    tn = min(tn, n)
    jb0 = col0 // tn
    cast_w = w.dtype != BF16
    if layer is None:
        wspec = pl.BlockSpec((k, tn), lambda j, i: (0, jb0 + j))
    else:
        wspec = pl.BlockSpec((None, k, tn), lambda j, i: (layer, 0, jb0 + j))
    in_specs = [pl.BlockSpec((tm, k), lambda j, i: (i, 0)), wspec]
    args = [a, w]
    if colscale is not None:
        in_specs.append(pl.BlockSpec((1, tn), lambda j, i: (0, j)))
        args.append(colscale.reshape(1, n).astype(F32))
    return pl.pallas_call(
        functools.partial(_mm_kernel, cast_w=cast_w, scaled=colscale is not None),
        grid=(n // tn, m // tm),
        in_specs=in_specs,
        out_specs=pl.BlockSpec((tm, tn), lambda j, i: (i, j)),
        out_shape=jax.ShapeDtypeStruct((m, n), out_dtype),
        scratch_shapes=[pltpu.VMEM((k, tn), BF16)] if cast_w else [],
        compiler_params=_cparams(2),
        name="matmul",
    )(*args)


MM_POST_SPLIT = 2


def _mm_post_kernel(*refs, n_in, cast_w, gate_idx, nshift_idx, nscale_idx, emit_h):
    a_refs = refs[:n_in]
    w_refs = refs[n_in:2 * n_in]
    x_ref, mod_ref, modn_ref, pw_ref, nw_ref = refs[2 * n_in:2 * n_in + 5]
    n_out = 2 if emit_h else 1
    outs = refs[2 * n_in + 5:2 * n_in + 5 + n_out]
    if cast_w:
        wb_refs = refs[2 * n_in + 5 + n_out:]

        @pl.when(pl.program_id(0) == 0)
        def _():
            for w_ref, wb_ref in zip(w_refs, wb_refs):
                wb_ref[...] = w_ref[...].astype(BF16)

        w_refs = wb_refs
    tm = x_ref.shape[0]
    for r0 in range(0, tm, tm // MM_POST_SPLIT):
        rows = slice(r0, r0 + tm // MM_POST_SPLIT)
        y = _dot(a_refs[0][rows, :], w_refs[0][...])
        for a_ref, w_ref in zip(a_refs[1:], w_refs[1:]):
            y = y + _dot(a_ref[rows, :], w_ref[...])
        xn = x_ref[rows, :] + mod_ref[gate_idx:gate_idx + 1, :] * _rms(y, pw_ref[...])
        outs[0][rows, :] = xn
        if emit_h:
            hn = _rms(xn, nw_ref[...])
            hn = hn * (1.0 + modn_ref[nscale_idx:nscale_idx + 1, :]) + modn_ref[nshift_idx:nshift_idx + 1, :]
            outs[1][rows, :] = hn.astype(outs[1].dtype)


def _mm_post(a_list, w, w_layer, x2, mod_l, modn_l, post_w, next_w, seq, gate_idx, nshift_idx, nscale_idx, emit_h,
             tm):
    m = a_list[0].shape[0]
    d = w.shape[-1]
    ka = a_list[0].shape[1]
    tm = min(tm, seq)
    bpb = seq // tm
    cast_w = w.dtype != BF16
    row = pl.BlockSpec((1, d), lambda i: (0, 0))
    modspec = pl.BlockSpec((None, 6, d), lambda i: (i // bpb, 0, 0))
    xspec = pl.BlockSpec((tm, d), lambda i: (i, 0))
    a_specs = [pl.BlockSpec((tm, ka), lambda i: (i, 0)) for _ in a_list]
    w_specs = [pl.BlockSpec((None, ka, d), lambda i, r=r: (w_layer, r, 0), pipeline_mode=pl.Buffered(1))
               for r in range(len(a_list))]
    out_specs = [xspec]
    out_shape = [jax.ShapeDtypeStruct((m, d), F32)]
    if emit_h:
        out_specs.append(xspec)
        out_shape.append(jax.ShapeDtypeStruct((m, d), BF16))
    res = pl.pallas_call(
        functools.partial(_mm_post_kernel, n_in=len(a_list), cast_w=cast_w, gate_idx=gate_idx,
                          nshift_idx=nshift_idx, nscale_idx=nscale_idx, emit_h=emit_h),
        grid=(m // tm,),
        in_specs=a_specs + w_specs + [xspec, modspec, modspec, row, row],
        out_specs=out_specs,
        out_shape=out_shape,
        scratch_shapes=[pltpu.VMEM((ka, d), BF16) for _ in a_list] if cast_w else [],
        compiler_params=_cparams(1),
        name="matmul_post",
    )(*a_list, *([w] * len(a_list)), x2, mod_l, modn_l, post_w.reshape(1, d), next_w.reshape(1, d))
    return res if emit_h else (res[0], None)


def _ffn_up_kernel(a_ref, wg_ref, wu_ref, o_ref, wgb_ref, wub_ref):
    @pl.when(pl.program_id(1) == 0)
    def _():
        wgb_ref[...] = wg_ref[...].astype(BF16)
        wub_ref[...] = wu_ref[...].astype(BF16)

    a = a_ref[...]
    g = _dot(a, wgb_ref[...])
    u = _dot(a, wub_ref[...])
    o_ref[...] = (_silu(g) * u).astype(o_ref.dtype)


def _ffn_up(a, wg, wu, layer, tm=512, tn=512):
    m, k = a.shape
    n = wg.shape[-1]
    tm = min(tm, m)
    wspec = pl.BlockSpec((None, k, tn), lambda j, i: (layer, 0, j))
    return pl.pallas_call(
        _ffn_up_kernel,
        grid=(n // tn, m // tm),
        in_specs=[pl.BlockSpec((tm, k), lambda j, i: (i, 0)), wspec, wspec],
        out_specs=pl.BlockSpec((tm, tn), lambda j, i: (i, j)),
        out_shape=jax.ShapeDtypeStruct((m, n), BF16),
        scratch_shapes=[pltpu.VMEM((k, tn), BF16), pltpu.VMEM((k, tn), BF16)],
        compiler_params=_cparams(2),
        name="ffn_up",
    )(a, wg, wu)


def _group_slabs(x, o_ref, heads_per_group):
    for g in range(o_ref.shape[0]):
        shift = (LANES - g * heads_per_group) % LANES
        o_ref[g] = x if shift == 0 else pltpu.roll(x, shift, 1)


def _gdn_gate_kernel(s_ref, alog_ref, dt_ref, o_ref, *, hg):
    s = s_ref[...]
    g = -jnp.exp(alog_ref[...]) * _softplus(s + dt_ref[...])
    beta = _sigmoid(s)
    lane = lax.broadcasted_iota(jnp.int32, s.shape, 1)
    _group_slabs(jnp.where(lane < A_HEADS, g, beta), o_ref, hg)


def _gdn_gates(slab_src, col_block, a_log, dt_bias, hg):
    t = slab_src.shape[0]
    tb = min(1024, t)
    ng = A_HEADS // hg
    pad = jnp.zeros((LANES - A_HEADS,), F32)
    alog_row = jnp.concatenate([a_log.astype(F32), pad])[None, :]
    dt_row = jnp.concatenate([dt_bias.astype(F32), pad])[None, :]
    row = pl.BlockSpec((1, LANES), lambda i: (0, 0))
    return pl.pallas_call(
        functools.partial(_gdn_gate_kernel, hg=hg),
        grid=(t // tb,),
        in_specs=[pl.BlockSpec((tb, LANES), lambda i: (i, col_block)), row, row],
        out_specs=pl.BlockSpec((ng, tb, LANES), lambda i: (0, i, 0)),
        out_shape=jax.ShapeDtypeStruct((ng, t, LANES), F32),
        compiler_params=_cparams(1),
        name="gdn_gates",
    )(slab_src, alog_row, dt_row)


GDN_HG = 4
GDN_TB = 256


def _unit_lower_inverses(n_mats):
    c = n_mats[0].shape[0]
    r = lax.broadcasted_iota(jnp.int32, (c, c), 0)
    col = lax.broadcasted_iota(jnp.int32, (c, c), 1)
    eye = jnp.where(r == col, 1.0, 0.0)

    def dot1(a, b):
        return _dot(a.astype(BF16), b.astype(BF16))

    ps = [eye - n for n in n_mats]
    ms = [dot1(n, n) for n in n_mats]
    power = 2
    while True:
        ps = [p + dot1(p, m) for p, m in zip(ps, ms)]
        power *= 2
        if power >= c:
            break
        ms = [dot1(m, m) for m in ms]
    res = [eye - p - _dot3(n, p) for n, p in zip(n_mats, ps)]
    return [p + dot1(p, e) for p, e in zip(ps, res)]


def _gdn_kernel(q_ref, k_ref, v_ref, z_ref, cwq_ref, cwk_ref, cwv_ref, gates_ref, grow_ref, onw_ref,
                o_ref, xq_ref, xk_ref, xv_ref, st_ref, *, tb, hg):
    j = pl.program_id(2)

    @pl.when(j == 0)
    def _():
        st_ref[...] = jnp.zeros_like(st_ref)
        for h_ref in (xq_ref, xk_ref, xv_ref):
            h_ref[...] = jnp.zeros_like(h_ref)

    def conv_silu(src_ref, hist_ref, cw_ref):
        x = src_ref[...]
        prev = hist_ref[...]
        hrows = prev.shape[0]
        row = lax.broadcasted_iota(jnp.int32, prev.shape, 0)
        acc = x * cw_ref[A_CONV - 1:A_CONV, :]
        for s in range(1, A_CONV):
            xs = pltpu.roll(x, s, 0)
            top = jnp.where(row < s, pltpu.roll(prev, s, 0), xs[0:hrows])
            shifted = jnp.concatenate([top, xs[hrows:]], axis=0)
            acc = acc + shifted * cw_ref[A_CONV - 1 - s:A_CONV - s, :]
        hist_ref[...] = x[tb - hrows:tb]
        return _silu(acc)

    q_all = conv_silu(q_ref, xq_ref, cwq_ref)
    k_all = conv_silu(k_ref, xk_ref, cwk_ref)
    v_all = conv_silu(v_ref, xv_ref, cwv_ref)

    c = CHUNK
    r = lax.broadcasted_iota(jnp.int32, (c, c), 0)
    col = lax.broadcasted_iota(jnp.int32, (c, c), 1)
    tri = r >= col
    strict = r > col

    nch = tb // c
    items = [(hh, ci) for hh in range(hg) for ci in range(nch)]
    qn, kn, vn = [], [], []
    for hh in range(hg):
        lanes = slice(hh * A_DK, (hh + 1) * A_DK)
        qh = q_all[:, lanes]
        kh = k_all[:, lanes]
        qn.append(qh * lax.rsqrt(jnp.sum(qh * qh, axis=-1, keepdims=True) + EPS) * (A_DK ** -0.5))
        kn.append(kh * lax.rsqrt(jnp.sum(kh * kh, axis=-1, keepdims=True) + EPS))
        vn.append(v_all[:, lanes])

    def rows_of(ci):
        return slice(ci * c, (ci + 1) * c)

    qc = [qn[hh][rows_of(ci)] for hh, ci in items]
    kc = [kn[hh][rows_of(ci)] for hh, ci in items]
    vc = [vn[hh][rows_of(ci)] for hh, ci in items]
    b_col = [gates_ref[rows_of(ci), A_HEADS + hh:A_HEADS + hh + 1] for hh, ci in items]
    gc_col = [jnp.sum(jnp.where(tri, grow_ref[hh][:, rows_of(ci)], 0.0), axis=1, keepdims=True)
              for hh, ci in items]
    gc_row = [jnp.sum(jnp.where(r <= col, gates_ref[rows_of(ci), hh:hh + 1], 0.0), axis=0, keepdims=True)
              for hh, ci in items]
    decay = [jnp.where(tri, jnp.exp(jnp.minimum(a - b, 0.0)), 0.0) for a, b in zip(gc_col, gc_row)]
    e_gc = [jnp.exp(a) for a in gc_col]
    g_last = [a[c - 1:c, :] for a in gc_col]
    kb = [x.astype(BF16) for x in kc]
    kk = [_dot_nt(x, x) for x in kb]
    qk = [_dot_nt(x.astype(BF16), y) for x, y in zip(qc, kb)]
    n_mat = [jnp.where(strict, b * x * d, 0.0) for b, x, d in zip(b_col, kk, decay)]
    qk = [jnp.where(tri, x * d, 0.0).astype(BF16) for x, d in zip(qk, decay)]
    t_inv = _unit_lower_inverses(n_mat)
    wu = [_dot3(t, jnp.concatenate([k * (b * e), v * b], axis=1)).astype(BF16)
          for t, k, v, b, e in zip(t_inv, kc, vc, b_col, e_gc)]
    kdb = [(k * jnp.exp(gl - g)).astype(BF16) for k, gl, g in zip(kc, g_last, gc_col)]
    mb = [_dot_tn(kd, x) for kd, x in zip(kdb, wu)]
    m_c = [x[:, :A_DK].astype(BF16) for x in mb]
    b_c = [x[:, A_DK:] for x in mb]
    qwu = [_dot(a, x) for a, x in zip(qk, wu)]
    q_eff = [(q * e - x[:, :A_DK]).astype(BF16) for q, e, x in zip(qc, e_gc, qwu)]
    qku = [x[:, A_DK:] for x in qwu]
    e_gl = [jnp.exp(x) for x in g_last]
    states = [st_ref[hh] for hh in range(hg)]
    o_c = [None] * len(items)
    for ci in range(nch):
        idx = [hh * nch + ci for hh in range(hg)]
        sb = [s.astype(BF16) for s in states]
        for hh in range(hg):
            o_c[idx[hh]] = _dot(q_eff[idx[hh]], sb[hh]) + qku[idx[hh]]
        states = [states[hh] * e_gl[idx[hh]] + b_c[idx[hh]] - _dot(m_c[idx[hh]], sb[hh]) for hh in range(hg)]
    for hh in range(hg):
        st_ref[hh] = states[hh]
    for i, (hh, ci) in enumerate(items):
        lanes = slice(hh * A_DK, (hh + 1) * A_DK)
        zc = z_ref[rows_of(ci), lanes]
        o_ref[rows_of(ci), lanes] = (_rms(o_c[i], onw_ref[...]) * _silu(zc)).astype(o_ref.dtype)


def _gdn(proj, conv_w, gates, g_row, out_norm, batch, seq):
    t = proj.shape[0]
    tb = min(GDN_TB, seq)
    hg = GDN_HG
    w = hg * A_DK
    nj = seq // tb
    ng = A_HEADS // hg
    kq = A_HEADS * A_DK
    off_k = kq // w
    off_v = 2 * kq // w
    off_z = 3 * kq // w

    def colspec(off):
        return pl.BlockSpec((tb, w), lambda b, g, j: (b * nj + j, off + g))

    def cwspec(off):
        return pl.BlockSpec((A_CONV, w), lambda b, g, j: (0, off + g))

    colv = pl.BlockSpec((None, tb, LANES), lambda b, g, j: (g, b * nj + j, 0))
    rowv = pl.BlockSpec((hg, 1, tb), lambda b, g, j: (g, 0, b * nj + j))
    return pl.pallas_call(
        functools.partial(_gdn_kernel, tb=tb, hg=hg),
        grid=(batch, ng, nj),
        in_specs=[colspec(0), colspec(off_k), colspec(off_v), colspec(off_z),
                  cwspec(0), cwspec(off_k), cwspec(off_v), colv, rowv,
                  pl.BlockSpec((1, A_DV), lambda b, g, j: (0, 0))],
        out_specs=pl.BlockSpec((tb, w), lambda b, g, j: (b * nj + j, g)),
        out_shape=jax.ShapeDtypeStruct((t, A_HEADS * A_DV), BF16),
        scratch_shapes=[pltpu.VMEM((SUBLANES, w), F32), pltpu.VMEM((SUBLANES, w), F32),
                        pltpu.VMEM((SUBLANES, w), F32), pltpu.VMEM((hg, A_DK, A_DV), F32)],
        compiler_params=_cparams(3),
        name="gated_deltanet",
    )(proj, proj, proj, proj, conv_w, conv_w, conv_w, gates, g_row, out_norm.reshape(1, A_DV))


def _rope_lanes(x, cos, sin_signed):
    return x * cos + pltpu.roll(x, LANES // 2, 1) * sin_signed


def _mla_q_kernel(cq_ref, nw_ref, w_ref, cos_ref, sin_ref, o_ref):
    n = _rms(cq_ref[...], nw_ref[...]).astype(BF16)
    cos = cos_ref[...]
    sin = sin_ref[...]
    for h in range(B_HEADS):
        qh = _dot(n, w_ref[:, h * B_QK:(h + 1) * B_QK])
        o_ref[:, h * B_QK:h * B_QK + B_NOPE] = qh[:, :B_NOPE].astype(o_ref.dtype)
        o_ref[:, h * B_QK + B_NOPE:(h + 1) * B_QK] = _rope_lanes(qh[:, B_NOPE:], cos, sin).astype(o_ref.dtype)


def _mla_q(proj, col_block, q_norm, w_uq_p, cos_b, sin_b):
    t = proj.shape[0]
    tm = min(512, t)
    n = w_uq_p.shape[1]
    return pl.pallas_call(
        _mla_q_kernel,
        grid=(t // tm,),
        in_specs=[pl.BlockSpec((tm, B_Q_LORA), lambda i: (i, col_block)),
                  pl.BlockSpec((1, B_Q_LORA), lambda i: (0, 0)),
                  pl.BlockSpec((B_Q_LORA, n), lambda i: (0, 0)),
                  pl.BlockSpec((tm, LANES), lambda i: (i, 0)),
                  pl.BlockSpec((tm, LANES), lambda i: (i, 0))],
        out_specs=pl.BlockSpec((tm, n), lambda i: (i, 0)),
        out_shape=jax.ShapeDtypeStruct((t, n), BF16),
        compiler_params=_cparams(1),
        name="mla_q_up",
    )(proj, q_norm.reshape(1, B_Q_LORA), w_uq_p, cos_b, sin_b)


def _mla_kv_kernel(ckv_ref, nw_ref, w_ref, kpe_ref, cos_ref, sin_ref, k_ref, v_ref):
    n = _rms(ckv_ref[...], nw_ref[...]).astype(BF16)
    kpe = _rope_lanes(kpe_ref[...], cos_ref[...], sin_ref[...]).astype(k_ref.dtype)
    nk = B_HEADS * B_NOPE
    kn = _dot(n, w_ref[:, :nk])
    for h in range(B_HEADS):
        k_ref[:, h * B_QK:h * B_QK + B_NOPE] = kn[:, h * B_NOPE:(h + 1) * B_NOPE].astype(k_ref.dtype)
        k_ref[:, h * B_QK + B_NOPE:(h + 1) * B_QK] = kpe
    v_ref[...] = _dot(n, w_ref[:, nk:]).astype(v_ref.dtype)


def _mla_kv(proj, col_block, kv_norm, w_ukv_p, kpe_block, cos_b, sin_b):
    t = proj.shape[0]
    tm = min(512, t)
    n = w_ukv_p.shape[1]
    tab = pl.BlockSpec((tm, LANES), lambda i: (i, 0))
    kpe_spec = pl.BlockSpec((tm, LANES), lambda i: (i, kpe_block))
    return pl.pallas_call(
        _mla_kv_kernel,
        grid=(t // tm,),
        in_specs=[pl.BlockSpec((tm, B_KV_LORA), lambda i: (i, col_block)),
                  pl.BlockSpec((1, B_KV_LORA), lambda i: (0, 0)),
                  pl.BlockSpec((B_KV_LORA, n), lambda i: (0, 0)),
                  kpe_spec, tab, tab],
        out_specs=[pl.BlockSpec((tm, B_HEADS * B_QK), lambda i: (i, 0)),
                   pl.BlockSpec((tm, B_HEADS * B_DV), lambda i: (i, 0))],
        out_shape=[jax.ShapeDtypeStruct((t, B_HEADS * B_QK), BF16),
                   jax.ShapeDtypeStruct((t, B_HEADS * B_DV), BF16)],
        compiler_params=_cparams(1),
        name="mla_kv_up",
    )(proj, kv_norm.reshape(1, B_KV_LORA), w_ukv_p, proj, cos_b, sin_b)


ATTN_T = 512


def _attn_kernel(*refs, tq, nq, dk, dv, hg, mask_gran, has_bias):
    if has_bias:
        q_ref, k_ref, v_ref, cq_ref, ck_ref, o_ref, m_ref, acc_ref, vt_ref = refs
    else:
        q_ref, k_ref, v_ref, o_ref, m_ref, acc_ref, vt_ref = refs
    qi = pl.program_id(2)

    @pl.when(qi == 0)
    def _():
        for hh in range(hg):
            for jb in range(nq):
                vblk = v_ref[jb * tq:(jb + 1) * tq, hh * dv:(hh + 1) * dv]
                vt_ref[hh, jb, 0:dv, :] = vblk.astype(F32).T.astype(BF16)
                vt_ref[hh, jb, dv:, :] = jnp.ones((ATTN_ONES, tq), BF16)

    m_ref[...] = jnp.full(m_ref.shape, NEG_INF, F32)
    acc_ref[...] = jnp.zeros_like(acc_ref)
    heads = range(hg)

    def step(jb, masked):
        start = pl.multiple_of(jb * tq, tq)
        s = [_dot_nt(k_ref[pl.ds(start, tq), hh * dk:(hh + 1) * dk], q_ref[:, hh * dk:(hh + 1) * dk])
             for hh in heads]
        if has_bias:
            s = [s[hh] - ck_ref[pl.ds(start, tq), hh:hh + 1] for hh in heads]
        if masked:
            r = lax.broadcasted_iota(jnp.int32, (tq, tq), 0)
            c = lax.broadcasted_iota(jnp.int32, (tq, tq), 1)
            if mask_gran > 1:
                shift = int(np.log2(mask_gran))
                r = lax.shift_right_logical(r, shift)
                c = lax.shift_right_logical(c, shift)
            keep = r <= c
            s = [jnp.where(keep, s[hh], NEG_INF) for hh in heads]
        m_prev = [m_ref[hh] for hh in heads]
        m_cur = [jnp.max(s[hh], axis=0, keepdims=True) for hh in heads]
        if has_bias:
            m_cur = [m_cur[hh] + cq_ref[hh] for hh in heads]
        m_new = [jnp.maximum(m_prev[hh], m_cur[hh]) for hh in heads]
        sub = [m_new[hh] - cq_ref[hh] for hh in heads] if has_bias else m_new
        p = [jnp.exp2(s[hh] - sub[hh]).astype(BF16) for hh in heads]
        alpha = [jnp.exp2(m_prev[hh] - m_new[hh]) for hh in heads]
        for hh in heads:
            m_ref[hh] = m_new[hh]
        pv = [_dot(vt_ref[hh, jb], p[hh]) for hh in heads]
        for hh in heads:
            acc_ref[hh] = alpha[hh] * acc_ref[hh] + pv[hh]

    def body(jb, carry):
        step(jb, False)
        return carry

    lax.fori_loop(0, qi, body, 0)
    step(qi, True)
    for hh in heads:
        acc = acc_ref[hh]
        o_ref[:, hh * dv:(hh + 1) * dv] = (acc[:dv] / acc[dv:dv + 1]).T.astype(o_ref.dtype)


ATTN_HG = 4
ATTN_ONES = 16
LOG2E = 1.4426950408889634


def _attention(q, k, v, dk, dv, heads, batch, seq, mask_gran, q_off=0, k_off=0, v_off=0, bias=None):
    t = q.shape[0]
    tq = min(ATTN_T, seq)
    nq = seq // tq
    hg = ATTN_HG
    qb, kb, vb = q_off // (hg * dk), k_off // (hg * dk), v_off // (hg * dv)
    dvx = dv + ATTN_ONES
    in_specs = [pl.BlockSpec((tq, hg * dk), lambda b, g, i: (b * nq + i, qb + g)),
                pl.BlockSpec((seq, hg * dk), lambda b, g, i: (b, kb + g)),
                pl.BlockSpec((seq, hg * dv), lambda b, g, i: (b, vb + g))]
    args = [q, k, v]
    if bias is not None:
        cq, ck = bias
        in_specs += [pl.BlockSpec((None, hg, 1, tq), lambda b, g, i: (b, g, 0, i)),
                     pl.BlockSpec((None, seq, LANES), lambda b, g, i: (g, b, 0))]
        args += [cq, ck]
    return pl.pallas_call(
        functools.partial(_attn_kernel, tq=tq, nq=nq, dk=dk, dv=dv, hg=hg, mask_gran=mask_gran,
                          has_bias=bias is not None),
        grid=(batch, heads // hg, nq),
        in_specs=in_specs,
        out_specs=pl.BlockSpec((tq, hg * dv), lambda b, g, i: (b * nq + i, g)),
        out_shape=jax.ShapeDtypeStruct((t, heads * dv), BF16),
        scratch_shapes=[pltpu.VMEM((hg, 1, tq), F32), pltpu.VMEM((hg, dvx, tq), F32),
                        pltpu.VMEM((hg, nq, dvx, tq), BF16)],
        compiler_params=_cparams(3),
        name="flash_attention",
    )(*args)


RET_HG = 2
RET_TB = 512


def _ret_kernel(q_ref, k_ref, v_ref, g_ref, cos_ref, sin_ref, dec_ref, xi_ref, zeta_ref, gch_ref, gnw_ref,
                o_ref, st_ref, *, tb, hg, c):
    j = pl.program_id(2)

    @pl.when(j == 0)
    def _():
        st_ref[...] = jnp.zeros_like(st_ref)

    cos = cos_ref[...]
    sin = sin_ref[...]
    nch = tb // c
    items = [(hh, ci) for hh in range(hg) for ci in range(nch)]

    def lanes_of(hh):
        return slice(hh * C_DK, (hh + 1) * C_DK)

    def rows_of(ci):
        return slice(ci * c, (ci + 1) * c)

    qh = [_rope_lanes(q_ref[:, lanes_of(hh)], cos, sin) for hh in range(hg)]
    kh = [_rope_lanes(k_ref[:, lanes_of(hh)], cos, sin) for hh in range(hg)]
    qb = [qh[hh][rows_of(ci)].astype(BF16) for hh, ci in items]
    kc = [kh[hh][rows_of(ci)] for hh, ci in items]
    vb = [v_ref[rows_of(ci), lanes_of(hh)].astype(BF16) for hh, ci in items]
    scores = [(_dot_nt(q, k.astype(BF16)) * dec_ref[hh]).astype(BF16) for q, k, (hh, _) in zip(qb, kc, items)]
    inner = [_dot(s, v) for s, v in zip(scores, vb)]
    kzv = [_dot_tn((k * zeta_ref[hh]).astype(BF16), v) for k, v, (hh, _) in zip(kc, vb, items)]
    states = []
    for hh in range(hg):
        state = st_ref[hh]
        for ci in range(nch):
            states.append(state)
            state = state * gch_ref[hh] + kzv[hh * nch + ci]
        st_ref[hh] = state
    cross = [_dot(q, s.astype(BF16)) * xi_ref[hh] for q, s, (hh, _) in zip(qb, states, items)]
    for i, (hh, ci) in enumerate(items):
        o_c = inner[i] + cross[i]
        gate = _silu(g_ref[rows_of(ci), lanes_of(hh)])
        o_ref[rows_of(ci), lanes_of(hh)] = (_rms(o_c, gnw_ref[hh]) * gate).astype(o_ref.dtype)


def _retention(proj, cos_c, sin_c, group_norm, batch, seq):
    t = proj.shape[0]
    tb = min(RET_TB, seq)
    c = min(RET_CHUNK, seq)
    hg = RET_HG
    w = hg * C_DK
    nj = seq // tb
    ng = C_HEADS // hg
    nh = C_HEADS * C_DK // w

    log_gamma = jnp.log1p(-jnp.exp2(-5.0 - jnp.arange(C_HEADS, dtype=F32)))
    idx = jnp.arange(c, dtype=F32)
    rel = idx[:, None] - idx[None, :]
    dec = jnp.where(rel >= 0, jnp.exp(log_gamma[:, None, None] * jnp.maximum(rel, 0.0)), 0.0)
    xi = jnp.exp(log_gamma[:, None] * (idx + 1.0))
    zeta = jnp.exp(log_gamma[:, None] * (c - 1.0 - idx))
    gch = jnp.exp(log_gamma * c)
    xi_b = jnp.broadcast_to(xi[:, :, None], (C_HEADS, c, C_DV))
    zeta_b = jnp.broadcast_to(zeta[:, :, None], (C_HEADS, c, C_DK))
    gch_b = jnp.broadcast_to(gch[:, None, None], (C_HEADS, 1, C_DV))

    def colspec(off):
        return pl.BlockSpec((tb, w), lambda b, g, j: (b * nj + j, off + g))

    tab = pl.BlockSpec((tb, LANES), lambda b, g, j: (b * nj + j, 0))

    def hspec(r, cdim):
        return pl.BlockSpec((hg, r, cdim), lambda b, g, j: (g, 0, 0))

    return pl.pallas_call(
        functools.partial(_ret_kernel, tb=tb, hg=hg, c=c),
        grid=(batch, ng, nj),
        in_specs=[colspec(0), colspec(nh), colspec(2 * nh), colspec(3 * nh), tab, tab,
                  hspec(c, c), hspec(c, C_DV), hspec(c, C_DK), hspec(1, C_DV), hspec(1, C_DV)],
        out_specs=pl.BlockSpec((tb, w), lambda b, g, j: (b * nj + j, g)),
        out_shape=jax.ShapeDtypeStruct((t, C_HEADS * C_DV), BF16),
        scratch_shapes=[pltpu.VMEM((hg, C_DK, C_DV), F32)],
        compiler_params=_cparams(3),
        name="retention",
    )(proj, proj, proj, proj, cos_c, sin_c, dec, xi_b, zeta_b, gch_b, group_norm.reshape(C_HEADS, 1, C_DV))


def _fox_cum_kernel(s_ref, b_ref, o_ref, *, seq, hg):
    x = s_ref[...] + b_ref[...]
    acc = -_softplus(-x)
    row = lax.broadcasted_iota(jnp.int32, acc.shape, 0)
    shift = 1
    while shift < seq:
        acc = acc + jnp.where(row >= shift, pltpu.roll(acc, shift, 0), 0.0)
        shift *= 2
    _group_slabs(acc * LOG2E, o_ref, hg)


def _fox_cum(slab, forget_bias, batch, seq, hg):
    t = slab.shape[0]
    ng = D_HEADS // hg
    pad = jnp.zeros((LANES - D_HEADS,), F32)
    b_row = jnp.concatenate([forget_bias.astype(F32), pad])[None, :]
    return pl.pallas_call(
        functools.partial(_fox_cum_kernel, seq=seq, hg=hg),
        grid=(batch,),
        in_specs=[pl.BlockSpec((seq, LANES), lambda b: (b, 0)),
                  pl.BlockSpec((1, LANES), lambda b: (0, 0))],
        out_specs=pl.BlockSpec((ng, seq, LANES), lambda b: (0, b, 0)),
        out_shape=jax.ShapeDtypeStruct((ng, t, LANES), F32),
        compiler_params=_cparams(1),
        name="fox_cumsum",
    )(slab, b_row)


def _pe_lanes(w_pe):
    k = w_pe.shape[0]
    z = jnp.zeros((k, B_ROPE // 2), w_pe.dtype)
    return jnp.concatenate([w_pe[:, :B_ROPE // 2], z, w_pe[:, B_ROPE // 2:], z], axis=1)


AB_MAIN = A_HEADS * (2 * A_DK + A_DV) + A_HEADS * A_DV


def _ab_weights(w_in, w_uq, w_ukv):
    o = AB_MAIN
    w_a = w_in[:, o:o + A_HEADS]; o += A_HEADS
    w_b = w_in[:, o:o + A_HEADS]; o += A_HEADS
    w_cq = w_in[:, o:o + B_Q_LORA]; o += B_Q_LORA
    w_ckv = w_in[:, o:o + B_KV_LORA]; o += B_KV_LORA
    w_kpe = w_in[:, o:o + B_ROPE]
    d = w_in.shape[0]
    w_tail = jnp.concatenate([w_cq, w_ckv, _pe_lanes(w_kpe), w_a, w_b,
                              jnp.zeros((d, LANES - 2 * A_HEADS), w_in.dtype)], axis=1)
    scale = (B_NOPE + B_ROPE) ** -0.5 * LOG2E
    wq = (w_uq * scale).reshape(B_Q_LORA, B_HEADS, B_NOPE + B_ROPE)
    wq_p = jnp.concatenate(
        [wq[:, :, :B_NOPE],
         _pe_lanes(wq[:, :, B_NOPE:].reshape(B_Q_LORA * B_HEADS, B_ROPE)).reshape(B_Q_LORA, B_HEADS, LANES)],
        axis=2).reshape(B_Q_LORA, B_HEADS * B_QK).astype(BF16)
    wkv = w_ukv.reshape(B_KV_LORA, B_HEADS, B_NOPE + B_DV)
    wkv_p = jnp.concatenate([wkv[:, :, :B_NOPE].reshape(B_KV_LORA, B_HEADS * B_NOPE),
                             wkv[:, :, B_NOPE:].reshape(B_KV_LORA, B_HEADS * B_DV)], axis=1).astype(BF16)
    return w_tail, wq_p, wkv_p


def _layer_ab(h, rope, w_in_all, j, conv_w, a_log, dt_bias, out_norm, q_norm, w_uq, kv_norm, w_ukv, batch, seq):
    cos_b, sin_b, _, _ = rope
    w_tail, wq_p, wkv_p = _ab_weights(w_in_all[j], w_uq, w_ukv)
    proj = _mm(h, w_in_all, F32, layer=j, ncols=AB_MAIN)
    tail = _mm(h, w_tail, F32, tn=w_tail.shape[1])
    cq_block, ckv_block = 0, 1
    kpe_block = (B_Q_LORA + B_KV_LORA) // LANES
    gates = _gdn_gates(tail, kpe_block + 1, a_log, dt_bias, GDN_HG)
    g_row = gates[0][:, :A_HEADS].T.reshape(A_HEADS, 1, h.shape[0])
    o_a = _gdn(proj, conv_w, gates, g_row, out_norm, batch, seq)
    q_b = _mla_q(tail, cq_block, q_norm, wq_p, cos_b, sin_b)
    k_b, v_b = _mla_kv(tail, ckv_block, kv_norm, wkv_p, kpe_block, cos_b, sin_b)
    o_b = _attention(q_b, k_b, v_b, B_QK, B_DV, B_HEADS, batch, seq, CHUNK)
    return [o_a, o_b]


def _layer_cd(h, rope, w_in_all, j, group_norm, forget_bias, batch, seq):
    _, _, cos_c, sin_c = rope
    n = C_HEADS * C_DK
    ones = jnp.ones((n,), F32)
    scale_c = jnp.concatenate([ones, ones * (C_DK ** -0.5), ones, ones])
    scale_d = jnp.concatenate([ones * (D_DH ** -0.5 * LOG2E), ones, ones])
    proj_c = _mm(h, w_in_all, F32, layer=j, ncols=4 * n, colscale=scale_c)
    proj_d = _mm(h, w_in_all, BF16, layer=j, col0=4 * n, ncols=3 * n, colscale=scale_d)
    w_f = jnp.concatenate([w_in_all[j][:, 7 * n:7 * n + D_HEADS],
                           jnp.zeros((w_in_all.shape[1], LANES - D_HEADS), w_in_all.dtype)], axis=1)
    slab = _mm(h, w_f, F32, tn=LANES)
    o_c = _retention(proj_c, cos_c, sin_c, group_norm, batch, seq)
    ck = _fox_cum(slab, forget_bias, batch, seq, ATTN_HG)
    cq = ck[0][:, :D_HEADS].reshape(batch, seq, D_HEADS).transpose(0, 2, 1).reshape(batch, D_HEADS, 1, seq)
    o_d = _attention(proj_d, proj_d, proj_d, D_DH, D_DH, D_HEADS, batch, seq, 1,
                     q_off=0, k_off=D_HEADS * D_DH, v_off=2 * D_HEADS * D_DH, bias=(cq, ck))
    return [o_c, o_d]


def kernel(x, c, positions, ada_w, ada_b, mix_pre_norm, mix_post_norm, ffn_pre_norm, ffn_post_norm, ab_w_in, ab_conv_w, ab_a_log, ab_dt_bias, ab_out_norm, ab_q_norm, ab_w_uq, ab_kv_norm, ab_w_ukv, ab_w_out, cd_w_in, cd_group_norm, cd_forget_bias, cd_w_out, ffn_w_gate, ffn_w_up, ffn_w_down):
    batch, seq, d = x.shape
    depth = ada_w.shape[0]
    t = batch * seq
    rows = 8
    c_pad = jnp.concatenate([c, jnp.zeros((rows - batch, d), c.dtype)], axis=0)
    mod = _ada(c_pad, ada_w, ada_b).reshape(depth, rows, 6, d)
    rope = _rope_tables(positions.astype(F32).reshape(t, 1))
    x2 = x.reshape(t, d)
    h = _prenorm(x2, mod[0], mix_pre_norm[0], seq, 0, 1)
    w_down = ffn_w_down.astype(BF16)
    for layer in range(depth):
        j = layer // 2
        if layer % 2 == 0:
            y_in = _layer_ab(h, rope, ab_w_in, j, ab_conv_w[j], ab_a_log[j], ab_dt_bias[j], ab_out_norm[j],
                             ab_q_norm[j], ab_w_uq[j], ab_kv_norm[j], ab_w_ukv[j], batch, seq)
            w_out = ab_w_out
        else:
            y_in = _layer_cd(h, rope, cd_w_in, j, cd_group_norm[j], cd_forget_bias[j], batch, seq)
            w_out = cd_w_out
        x2, h = _mm_post(y_in, w_out, j, x2, mod[layer], mod[layer], mix_post_norm[layer],
                         ffn_pre_norm[layer], seq, 2, 3, 4, True, tm=512)
        hid = _ffn_up(h, ffn_w_gate, ffn_w_up, layer)
        last = layer == depth - 1
        nxt = layer if last else layer + 1
        x2, h = _mm_post([hid], w_down, layer, x2, mod[layer], mod[nxt],
                         ffn_post_norm[layer], mix_pre_norm[nxt], seq, 5, 0, 1, not last, tm=256)
    return x2.reshape(batch, seq, d)
```

```python
import functools

import jax
import jax.numpy as jnp
import numpy as np
from jax import lax
from jax.experimental import pallas as pl
from jax.experimental.pallas import tpu as pltpu

F32 = jnp.float32
BF16 = jnp.bfloat16

D_MODEL = 2048
CHUNK = 64
EPS = 1e-6
ROPE_BASE = 10000.0
NEG_INF = -1e30

A_HEADS, A_DK, A_DV, A_CONV = 8, 128, 128, 4
B_HEADS, B_NOPE, B_ROPE, B_DV, B_Q_LORA, B_KV_LORA = 8, 128, 64, 128, 512, 512
C_HEADS, C_DK, C_DV = 8, 128, 128
D_HEADS, D_DH = 8, 128
LANES = 128
SUBLANES = 8
B_QK = B_NOPE + LANES

VMEM_LIMIT_BYTES = 56 * 1024 * 1024

RET_CHUNK = 128


def _cparams(n_axes):
    return pltpu.CompilerParams(dimension_semantics=("arbitrary",) * n_axes,
                                vmem_limit_bytes=VMEM_LIMIT_BYTES)


def _dot(a, b):
    return jnp.dot(a, b, preferred_element_type=F32)


def _dot_nt(a, b):
    return lax.dot_general(a, b, (((1,), (1,)), ((), ())), preferred_element_type=F32)


def _dot_tn(a, b):
    return lax.dot_general(a, b, (((0,), (0,)), ((), ())), preferred_element_type=F32)


def _split_bf16(a):
    hi = a.astype(BF16)
    lo = (a - hi.astype(F32)).astype(BF16)
    return hi, lo


def _dot3(a, b):
    ah, al = _split_bf16(a)
    bh, bl = _split_bf16(b)
    return _dot(ah, bh) + _dot(ah, bl) + _dot(al, bh)


def _sigmoid(x):
    return 1.0 / (1.0 + jnp.exp(-x))


def _silu(x):
    return x * _sigmoid(x)


def _softplus(x):
    return jnp.maximum(x, 0.0) + jnp.log(1.0 + jnp.exp(-jnp.abs(x)))


def _rms(x, w):
    return x * lax.rsqrt(jnp.mean(x * x, axis=-1, keepdims=True) + EPS) * w


def _ada_kernel(c_ref, w_ref, b_ref, o_ref):
    cond = _silu(c_ref[...]).astype(BF16)
    o_ref[...] = _dot(cond, w_ref[...].astype(BF16)) + b_ref[...]


def _ada(c_pad, ada_w, ada_b):
    depth, d, n = ada_w.shape
    rows = c_pad.shape[0]
    tn = 1024
    return pl.pallas_call(
        _ada_kernel,
        grid=(depth, n // tn),
        in_specs=[pl.BlockSpec((rows, d), lambda l, j: (0, 0)),
                  pl.BlockSpec((None, d, tn), lambda l, j: (l, 0, j)),
                  pl.BlockSpec((None, 1, tn), lambda l, j: (l, 0, j))],
        out_specs=pl.BlockSpec((None, rows, tn), lambda l, j: (l, 0, j)),
        out_shape=jax.ShapeDtypeStruct((depth, rows, n), F32),
        compiler_params=_cparams(2),
        name="ada_mod",
    )(c_pad, ada_w, ada_b.reshape(depth, 1, n))


def _rope_kernel(pos_ref, f_ref, sb_ref, sc_ref, cb_ref, snb_ref, cc_ref, snc_ref):
    ang = pos_ref[...] * f_ref[...]
    cos = jnp.cos(ang)
    sin = jnp.sin(ang)
    lane = lax.broadcasted_iota(jnp.int32, ang.shape, 1)
    nb, nc = B_ROPE // 2, C_DK // 2
    in_b = (lane < nb) | ((lane >= 2 * nb) & (lane < 3 * nb))

    def lay_b(x):
        return jnp.where(in_b, jnp.where(lane < nb, x, pltpu.roll(x, 2 * nb, 1)), 0.0)

    def lay_c(x):
        return jnp.where(lane < nc, pltpu.roll(x, LANES - nb, 1), pltpu.roll(x, nc - nb, 1))

    cb_ref[...] = lay_b(cos)
    snb_ref[...] = lay_b(sin) * sb_ref[...]
    cc_ref[...] = lay_c(cos)
    snc_ref[...] = lay_c(sin) * sc_ref[...]


def _rope_tables(pos_col):
    t = pos_col.shape[0]
    tb = min(1024, t)
    inv_b = ROPE_BASE ** (-jnp.arange(0, B_ROPE, 2, dtype=F32) / B_ROPE)
    inv_c = ROPE_BASE ** (-jnp.arange(0, C_DK, 2, dtype=F32) / C_DK)
    z32 = jnp.zeros((B_ROPE // 2,), F32)
    o32 = jnp.ones((B_ROPE // 2,), F32)
    o64 = jnp.ones((C_DK // 2,), F32)
    freqs = jnp.concatenate([inv_b, inv_c, z32])[None, :]
    sb = jnp.concatenate([-o32, z32, o32, z32])[None, :]
    sc = jnp.concatenate([-o64, o64])[None, :]
    row = pl.BlockSpec((1, LANES), lambda i: (0, 0))
    tab = pl.BlockSpec((tb, LANES), lambda i: (i, 0))
    shp = jax.ShapeDtypeStruct((t, LANES), F32)
    return pl.pallas_call(
        _rope_kernel,
        grid=(t // tb,),
        in_specs=[pl.BlockSpec((tb, 1), lambda i: (i, 0)), row, row, row],
        out_specs=[tab, tab, tab, tab],
        out_shape=[shp, shp, shp, shp],
        compiler_params=_cparams(1),
        name="rope_tables",
    )(pos_col, freqs, sb, sc)


def _prenorm_kernel(x_ref, mod_ref, w_ref, h_ref, *, shift_idx, scale_idx):
    y = _rms(x_ref[...], w_ref[...])
    h = y * (1.0 + mod_ref[scale_idx:scale_idx + 1, :]) + mod_ref[shift_idx:shift_idx + 1, :]
    h_ref[...] = h.astype(h_ref.dtype)


def _prenorm(x2, mod_l, w, seq, shift_idx, scale_idx):
    t, d = x2.shape
    tm = min(512, seq)
    bpb = seq // tm
    return pl.pallas_call(
        functools.partial(_prenorm_kernel, shift_idx=shift_idx, scale_idx=scale_idx),
        grid=(t // tm,),
        in_specs=[pl.BlockSpec((tm, d), lambda i: (i, 0)),
                  pl.BlockSpec((None, 6, d), lambda i: (i // bpb, 0, 0)),
                  pl.BlockSpec((1, d), lambda i: (0, 0))],
        out_specs=pl.BlockSpec((tm, d), lambda i: (i, 0)),
        out_shape=jax.ShapeDtypeStruct((t, d), BF16),
        compiler_params=_cparams(1),
        name="prenorm",
    )(x2, mod_l, w.reshape(1, d))


def _mm_kernel(*refs, cast_w, scaled):
    a_ref, w_ref = refs[0], refs[1]
    s_ref = refs[2] if scaled else None
    o_ref = refs[3] if scaled else refs[2]
    if cast_w:
        wb_ref = refs[-1]

        @pl.when(pl.program_id(1) == 0)
        def _():
            wb_ref[...] = w_ref[...].astype(BF16)

        w = wb_ref[...]
    else:
        w = w_ref[...]
    y = _dot(a_ref[...], w)
    if scaled:
        y = y * s_ref[...]
    o_ref[...] = y.astype(o_ref.dtype)


def _mm(a, w, out_dtype, layer=None, col0=0, ncols=None, colscale=None, tm=1024, tn=1024):
    m, k = a.shape
    n = w.shape[-1] if ncols is None else ncols
    tm = min(tm, m)
    tn = min(tn, n)
    jb0 = col0 // tn
    cast_w = w.dtype != BF16
    if layer is None:
        wspec = pl.BlockSpec((k, tn), lambda j, i: (0, jb0 + j))
    else:
        wspec = pl.BlockSpec((None, k, tn), lambda j, i: (layer, 0, jb0 + j))
    in_specs = [pl.BlockSpec((tm, k), lambda j, i: (i, 0)), wspec]
    args = [a, w]
    if colscale is not None:
        in_specs.append(pl.BlockSpec((1, tn), lambda j, i: (0, j)))
        args.append(colscale.reshape(1, n).astype(F32))
    return pl.pallas_call(
        functools.partial(_mm_kernel, cast_w=cast_w, scaled=colscale is not None),
        grid=(n // tn, m // tm),
        in_specs=in_specs,
        out_specs=pl.BlockSpec((tm, tn), lambda j, i: (i, j)),
        out_shape=jax.ShapeDtypeStruct((m, n), out_dtype),
        scratch_shapes=[pltpu.VMEM((k, tn), BF16)] if cast_w else [],
        compiler_params=_cparams(2),
        name="matmul",
    )(*args)


MM_POST_SPLIT = 2


def _mm_post_kernel(*refs, n_in, cast_w, gate_idx, nshift_idx, nscale_idx, emit_h):
    a_refs = refs[:n_in]
    w_refs = refs[n_in:2 * n_in]
    x_ref, mod_ref, modn_ref, pw_ref, nw_ref = refs[2 * n_in:2 * n_in + 5]
    n_out = 2 if emit_h else 1
    outs = refs[2 * n_in + 5:2 * n_in + 5 + n_out]
    if cast_w:
        wb_refs = refs[2 * n_in + 5 + n_out:]

        @pl.when(pl.program_id(0) == 0)
        def _():
            for w_ref, wb_ref in zip(w_refs, wb_refs):
                wb_ref[...] = w_ref[...].astype(BF16)

        w_refs = wb_refs
    tm = x_ref.shape[0]
    for r0 in range(0, tm, tm // MM_POST_SPLIT):
        rows = slice(r0, r0 + tm // MM_POST_SPLIT)
        y = _dot(a_refs[0][rows, :], w_refs[0][...])
        for a_ref, w_ref in zip(a_refs[1:], w_refs[1:]):
            y = y + _dot(a_ref[rows, :], w_ref[...])
        xn = x_ref[rows, :] + mod_ref[gate_idx:gate_idx + 1, :] * _rms(y, pw_ref[...])
        outs[0][rows, :] = xn
        if emit_h:
            hn = _rms(xn, nw_ref[...])
            hn = hn * (1.0 + modn_ref[nscale_idx:nscale_idx + 1, :]) + modn_ref[nshift_idx:nshift_idx + 1, :]
            outs[1][rows, :] = hn.astype(outs[1].dtype)


def _mm_post(a_list, w, w_layer, x2, mod_l, modn_l, post_w, next_w, seq, gate_idx, nshift_idx, nscale_idx, emit_h,
             tm):
    m = a_list[0].shape[0]
    d = w.shape[-1]
    ka = a_list[0].shape[1]
    tm = min(tm, seq)
    bpb = seq // tm
    cast_w = w.dtype != BF16
    row = pl.BlockSpec((1, d), lambda i: (0, 0))
    modspec = pl.BlockSpec((None, 6, d), lambda i: (i // bpb, 0, 0))
    xspec = pl.BlockSpec((tm, d), lambda i: (i, 0))
    a_specs = [pl.BlockSpec((tm, ka), lambda i: (i, 0)) for _ in a_list]
    w_specs = [pl.BlockSpec((None, ka, d), lambda i, r=r: (w_layer, r, 0), pipeline_mode=pl.Buffered(1))
               for r in range(len(a_list))]
    out_specs = [xspec]
    out_shape = [jax.ShapeDtypeStruct((m, d), F32)]
    if emit_h:
        out_specs.append(xspec)
        out_shape.append(jax.ShapeDtypeStruct((m, d), BF16))
    res = pl.pallas_call(
        functools.partial(_mm_post_kernel, n_in=len(a_list), cast_w=cast_w, gate_idx=gate_idx,
                          nshift_idx=nshift_idx, nscale_idx=nscale_idx, emit_h=emit_h),
        grid=(m // tm,),
        in_specs=a_specs + w_specs + [xspec, modspec, modspec, row, row],
        out_specs=out_specs,
        out_shape=out_shape,
        scratch_shapes=[pltpu.VMEM((ka, d), BF16) for _ in a_list] if cast_w else [],
        compiler_params=_cparams(1),
        name="matmul_post",
    )(*a_list, *([w] * len(a_list)), x2, mod_l, modn_l, post_w.reshape(1, d), next_w.reshape(1, d))
    return res if emit_h else (res[0], None)


def _ffn_up_kernel(a_ref, wg_ref, wu_ref, o_ref, wgb_ref, wub_ref):
    @pl.when(pl.program_id(1) == 0)
    def _():
        wgb_ref[...] = wg_ref[...].astype(BF16)
        wub_ref[...] = wu_ref[...].astype(BF16)

    a = a_ref[...]
    g = _dot(a, wgb_ref[...])
    u = _dot(a, wub_ref[...])
    o_ref[...] = (_silu(g) * u).astype(o_ref.dtype)


def _ffn_up(a, wg, wu, layer, tm=1024, tn=512):
    m, k = a.shape
    n = wg.shape[-1]
    tm = min(tm, m)
    wspec = pl.BlockSpec((None, k, tn), lambda j, i: (layer, 0, j))
    return pl.pallas_call(
        _ffn_up_kernel,
        grid=(n // tn, m // tm),
        in_specs=[pl.BlockSpec((tm, k), lambda j, i: (i, 0)), wspec, wspec],
        out_specs=pl.BlockSpec((tm, tn), lambda j, i: (i, j)),
        out_shape=jax.ShapeDtypeStruct((m, n), BF16),
        scratch_shapes=[pltpu.VMEM((k, tn), BF16), pltpu.VMEM((k, tn), BF16)],
        compiler_params=_cparams(2),
        name="ffn_up",
    )(a, wg, wu)


def _group_slabs(x, o_ref, heads_per_group):
    for g in range(o_ref.shape[0]):
        shift = (LANES - g * heads_per_group) % LANES
        o_ref[g] = x if shift == 0 else pltpu.roll(x, shift, 1)


def _gdn_gate_kernel(s_ref, alog_ref, dt_ref, o_ref, *, hg):
    s = s_ref[...]
    g = -jnp.exp(alog_ref[...]) * _softplus(s + dt_ref[...])
    beta = _sigmoid(s)
    lane = lax.broadcasted_iota(jnp.int32, s.shape, 1)
    _group_slabs(jnp.where(lane < A_HEADS, g, beta), o_ref, hg)


def _gdn_gates(slab_src, col_block, a_log, dt_bias, hg):
    t = slab_src.shape[0]
    tb = min(1024, t)
    ng = A_HEADS // hg
    pad = jnp.zeros((LANES - A_HEADS,), F32)
    alog_row = jnp.concatenate([a_log.astype(F32), pad])[None, :]
    dt_row = jnp.concatenate([dt_bias.astype(F32), pad])[None, :]
    row = pl.BlockSpec((1, LANES), lambda i: (0, 0))
    return pl.pallas_call(
        functools.partial(_gdn_gate_kernel, hg=hg),
        grid=(t // tb,),
        in_specs=[pl.BlockSpec((tb, LANES), lambda i: (i, col_block)), row, row],
        out_specs=pl.BlockSpec((ng, tb, LANES), lambda i: (0, i, 0)),
        out_shape=jax.ShapeDtypeStruct((ng, t, LANES), F32),
        compiler_params=_cparams(1),
        name="gdn_gates",
    )(slab_src, alog_row, dt_row)


GDN_HG = 4
GDN_TB = 256


def _unit_lower_inverses(n_mats):
    c = n_mats[0].shape[0]
    r = lax.broadcasted_iota(jnp.int32, (c, c), 0)
    col = lax.broadcasted_iota(jnp.int32, (c, c), 1)
    eye = jnp.where(r == col, 1.0, 0.0)

    def dot1(a, b):
        return _dot(a.astype(BF16), b.astype(BF16))

    ps = [eye - n for n in n_mats]
    ms = [dot1(n, n) for n in n_mats]
    power = 2
    while True:
        ps = [p + dot1(p, m) for p, m in zip(ps, ms)]
        power *= 2
        if power >= c:
            break
        ms = [dot1(m, m) for m in ms]
    res = [eye - p - _dot3(n, p) for n, p in zip(n_mats, ps)]
    return [p + dot1(p, e) for p, e in zip(ps, res)]


def _gdn_kernel(q_ref, k_ref, v_ref, z_ref, cwq_ref, cwk_ref, cwv_ref, gates_ref, grow_ref, onw_ref,
                o_ref, xq_ref, xk_ref, xv_ref, st_ref, *, tb, hg):
    j = pl.program_id(2)

    @pl.when(j == 0)
    def _():
        st_ref[...] = jnp.zeros_like(st_ref)
        for h_ref in (xq_ref, xk_ref, xv_ref):
            h_ref[...] = jnp.zeros_like(h_ref)

    def conv_silu(src_ref, hist_ref, cw_ref):
        x = src_ref[...]
        prev = hist_ref[...]
        hrows = prev.shape[0]
        row = lax.broadcasted_iota(jnp.int32, prev.shape, 0)
        acc = x * cw_ref[A_CONV - 1:A_CONV, :]
        for s in range(1, A_CONV):
            xs = pltpu.roll(x, s, 0)
            top = jnp.where(row < s, pltpu.roll(prev, s, 0), xs[0:hrows])
            shifted = jnp.concatenate([top, xs[hrows:]], axis=0)
            acc = acc + shifted * cw_ref[A_CONV - 1 - s:A_CONV - s, :]
        hist_ref[...] = x[tb - hrows:tb]
        return _silu(acc)

    q_all = conv_silu(q_ref, xq_ref, cwq_ref)
    k_all = conv_silu(k_ref, xk_ref, cwk_ref)
    v_all = conv_silu(v_ref, xv_ref, cwv_ref)

    c = CHUNK
    r = lax.broadcasted_iota(jnp.int32, (c, c), 0)
    col = lax.broadcasted_iota(jnp.int32, (c, c), 1)
    tri = r >= col
    strict = r > col

    nch = tb // c
    items = [(hh, ci) for hh in range(hg) for ci in range(nch)]
    qn, kn, vn = [], [], []
    for hh in range(hg):
        lanes = slice(hh * A_DK, (hh + 1) * A_DK)
        qh = q_all[:, lanes]
        kh = k_all[:, lanes]
        qn.append(qh * lax.rsqrt(jnp.sum(qh * qh, axis=-1, keepdims=True) + EPS) * (A_DK ** -0.5))
        kn.append(kh * lax.rsqrt(jnp.sum(kh * kh, axis=-1, keepdims=True) + EPS))
        vn.append(v_all[:, lanes])

    def rows_of(ci):
        return slice(ci * c, (ci + 1) * c)

    qc = [qn[hh][rows_of(ci)] for hh, ci in items]
    kc = [kn[hh][rows_of(ci)] for hh, ci in items]
    vc = [vn[hh][rows_of(ci)] for hh, ci in items]
    b_col = [gates_ref[rows_of(ci), A_HEADS + hh:A_HEADS + hh + 1] for hh, ci in items]
    gc_col = [jnp.sum(jnp.where(tri, grow_ref[hh][:, rows_of(ci)], 0.0), axis=1, keepdims=True)
              for hh, ci in items]
    gc_row = [jnp.sum(jnp.where(r <= col, gates_ref[rows_of(ci), hh:hh + 1], 0.0), axis=0, keepdims=True)
              for hh, ci in items]
    decay = [jnp.where(tri, jnp.exp(jnp.minimum(a - b, 0.0)), 0.0) for a, b in zip(gc_col, gc_row)]
    e_gc = [jnp.exp(a) for a in gc_col]
    g_last = [a[c - 1:c, :] for a in gc_col]
    kb = [x.astype(BF16) for x in kc]
    kk = [_dot_nt(x, x) for x in kb]
    qk = [_dot_nt(x.astype(BF16), y) for x, y in zip(qc, kb)]
    n_mat = [jnp.where(strict, b * x * d, 0.0) for b, x, d in zip(b_col, kk, decay)]
    qk = [jnp.where(tri, x * d, 0.0).astype(BF16) for x, d in zip(qk, decay)]
    t_inv = _unit_lower_inverses(n_mat)
    wu = [_dot3(t, jnp.concatenate([k * (b * e), v * b], axis=1)).astype(BF16)
          for t, k, v, b, e in zip(t_inv, kc, vc, b_col, e_gc)]
    kdb = [(k * jnp.exp(gl - g)).astype(BF16) for k, gl, g in zip(kc, g_last, gc_col)]
    mb = [_dot_tn(kd, x) for kd, x in zip(kdb, wu)]
    m_c = [x[:, :A_DK].astype(BF16) for x in mb]
    b_c = [x[:, A_DK:] for x in mb]
    qwu = [_dot(a, x) for a, x in zip(qk, wu)]
    q_eff = [(q * e - x[:, :A_DK]).astype(BF16) for q, e, x in zip(qc, e_gc, qwu)]
    qku = [x[:, A_DK:] for x in qwu]
    e_gl = [jnp.exp(x) for x in g_last]
    states = [st_ref[hh] for hh in range(hg)]
    o_c = [None] * len(items)
    for ci in range(nch):
        idx = [hh * nch + ci for hh in range(hg)]
        sb = [s.astype(BF16) for s in states]
        for hh in range(hg):
            o_c[idx[hh]] = _dot(q_eff[idx[hh]], sb[hh]) + qku[idx[hh]]
        states = [states[hh] * e_gl[idx[hh]] + b_c[idx[hh]] - _dot(m_c[idx[hh]], sb[hh]) for hh in range(hg)]
    for hh in range(hg):
        st_ref[hh] = states[hh]
    for i, (hh, ci) in enumerate(items):
        lanes = slice(hh * A_DK, (hh + 1) * A_DK)
        zc = z_ref[rows_of(ci), lanes]
        o_ref[rows_of(ci), lanes] = (_rms(o_c[i], onw_ref[...]) * _silu(zc)).astype(o_ref.dtype)


def _gdn(proj, conv_w, gates, g_row, out_norm, batch, seq):
    t = proj.shape[0]
    tb = min(GDN_TB, seq)
    hg = GDN_HG
    w = hg * A_DK
    nj = seq // tb
    ng = A_HEADS // hg
    kq = A_HEADS * A_DK
    off_k = kq // w
    off_v = 2 * kq // w
    off_z = 3 * kq // w

    def colspec(off):
        return pl.BlockSpec((tb, w), lambda b, g, j: (b * nj + j, off + g))

    def cwspec(off):
        return pl.BlockSpec((A_CONV, w), lambda b, g, j: (0, off + g))

    colv = pl.BlockSpec((None, tb, LANES), lambda b, g, j: (g, b * nj + j, 0))
    rowv = pl.BlockSpec((hg, 1, tb), lambda b, g, j: (g, 0, b * nj + j))
    return pl.pallas_call(
        functools.partial(_gdn_kernel, tb=tb, hg=hg),
        grid=(batch, ng, nj),
        in_specs=[colspec(0), colspec(off_k), colspec(off_v), colspec(off_z),
                  cwspec(0), cwspec(off_k), cwspec(off_v), colv, rowv,
                  pl.BlockSpec((1, A_DV), lambda b, g, j: (0, 0))],
        out_specs=pl.BlockSpec((tb, w), lambda b, g, j: (b * nj + j, g)),
        out_shape=jax.ShapeDtypeStruct((t, A_HEADS * A_DV), BF16),
        scratch_shapes=[pltpu.VMEM((SUBLANES, w), F32), pltpu.VMEM((SUBLANES, w), F32),
                        pltpu.VMEM((SUBLANES, w), F32), pltpu.VMEM((hg, A_DK, A_DV), F32)],
        compiler_params=_cparams(3),
        name="gated_deltanet",
    )(proj, proj, proj, proj, conv_w, conv_w, conv_w, gates, g_row, out_norm.reshape(1, A_DV))


def _rope_lanes(x, cos, sin_signed):
    return x * cos + pltpu.roll(x, LANES // 2, 1) * sin_signed


def _mla_q_kernel(cq_ref, nw_ref, w_ref, cos_ref, sin_ref, o_ref):
    n = _rms(cq_ref[...], nw_ref[...]).astype(BF16)
    cos = cos_ref[...]
    sin = sin_ref[...]
    for h in range(B_HEADS):
        qh = _dot(n, w_ref[:, h * B_QK:(h + 1) * B_QK])
        o_ref[:, h * B_QK:h * B_QK + B_NOPE] = qh[:, :B_NOPE].astype(o_ref.dtype)
        o_ref[:, h * B_QK + B_NOPE:(h + 1) * B_QK] = _rope_lanes(qh[:, B_NOPE:], cos, sin).astype(o_ref.dtype)


def _mla_q(proj, col_block, q_norm, w_uq_p, cos_b, sin_b):
    t = proj.shape[0]
    tm = min(512, t)
    n = w_uq_p.shape[1]
    return pl.pallas_call(
        _mla_q_kernel,
        grid=(t // tm,),
        in_specs=[pl.BlockSpec((tm, B_Q_LORA), lambda i: (i, col_block)),
                  pl.BlockSpec((1, B_Q_LORA), lambda i: (0, 0)),
                  pl.BlockSpec((B_Q_LORA, n), lambda i: (0, 0)),
                  pl.BlockSpec((tm, LANES), lambda i: (i, 0)),
                  pl.BlockSpec((tm, LANES), lambda i: (i, 0))],
        out_specs=pl.BlockSpec((tm, n), lambda i: (i, 0)),
        out_shape=jax.ShapeDtypeStruct((t, n), BF16),
        compiler_params=_cparams(1),
        name="mla_q_up",
    )(proj, q_norm.reshape(1, B_Q_LORA), w_uq_p, cos_b, sin_b)


def _mla_kv_kernel(ckv_ref, nw_ref, w_ref, kpe_ref, cos_ref, sin_ref, k_ref, v_ref):
    n = _rms(ckv_ref[...], nw_ref[...]).astype(BF16)
    kpe = _rope_lanes(kpe_ref[...], cos_ref[...], sin_ref[...]).astype(k_ref.dtype)
    nk = B_HEADS * B_NOPE
    kn = _dot(n, w_ref[:, :nk])
    for h in range(B_HEADS):
        k_ref[:, h * B_QK:h * B_QK + B_NOPE] = kn[:, h * B_NOPE:(h + 1) * B_NOPE].astype(k_ref.dtype)
        k_ref[:, h * B_QK + B_NOPE:(h + 1) * B_QK] = kpe
    v_ref[...] = _dot(n, w_ref[:, nk:]).astype(v_ref.dtype)


def _mla_kv(proj, col_block, kv_norm, w_ukv_p, kpe_block, cos_b, sin_b):
    t = proj.shape[0]
    tm = min(512, t)
    n = w_ukv_p.shape[1]
    tab = pl.BlockSpec((tm, LANES), lambda i: (i, 0))
    kpe_spec = pl.BlockSpec((tm, LANES), lambda i: (i, kpe_block))
    return pl.pallas_call(
        _mla_kv_kernel,
        grid=(t // tm,),
        in_specs=[pl.BlockSpec((tm, B_KV_LORA), lambda i: (i, col_block)),
                  pl.BlockSpec((1, B_KV_LORA), lambda i: (0, 0)),
                  pl.BlockSpec((B_KV_LORA, n), lambda i: (0, 0)),
                  kpe_spec, tab, tab],
        out_specs=[pl.BlockSpec((tm, B_HEADS * B_QK), lambda i: (i, 0)),
                   pl.BlockSpec((tm, B_HEADS * B_DV), lambda i: (i, 0))],
        out_shape=[jax.ShapeDtypeStruct((t, B_HEADS * B_QK), BF16),
                   jax.ShapeDtypeStruct((t, B_HEADS * B_DV), BF16)],
        compiler_params=_cparams(1),
        name="mla_kv_up",
    )(proj, kv_norm.reshape(1, B_KV_LORA), w_ukv_p, proj, cos_b, sin_b)


ATTN_T = 512


def _attn_kernel(*refs, tq, nq, dk, dv, hg, mask_gran, has_bias):
    if has_bias:
        q_ref, k_ref, v_ref, cq_ref, ck_ref, o_ref, m_ref, acc_ref, vt_ref = refs
    else:
        q_ref, k_ref, v_ref, o_ref, m_ref, acc_ref, vt_ref = refs
    qi = pl.program_id(2)

    @pl.when(qi == 0)
    def _():
        for hh in range(hg):
            for jb in range(nq):
                vblk = v_ref[jb * tq:(jb + 1) * tq, hh * dv:(hh + 1) * dv]
                vt_ref[hh, jb, 0:dv, :] = vblk.astype(F32).T.astype(BF16)
                vt_ref[hh, jb, dv:, :] = jnp.ones((ATTN_ONES, tq), BF16)

    m_ref[...] = jnp.full(m_ref.shape, NEG_INF, F32)
    acc_ref[...] = jnp.zeros_like(acc_ref)
    heads = range(hg)

    def step(jb, masked):
        start = pl.multiple_of(jb * tq, tq)
        s = [_dot_nt(k_ref[pl.ds(start, tq), hh * dk:(hh + 1) * dk], q_ref[:, hh * dk:(hh + 1) * dk])
             for hh in heads]
        if has_bias:
            s = [s[hh] - ck_ref[pl.ds(start, tq), hh:hh + 1] for hh in heads]
        if masked:
            r = lax.broadcasted_iota(jnp.int32, (tq, tq), 0)
            c = lax.broadcasted_iota(jnp.int32, (tq, tq), 1)
            if mask_gran > 1:
                shift = int(np.log2(mask_gran))
                r = lax.shift_right_logical(r, shift)
                c = lax.shift_right_logical(c, shift)
            keep = r <= c
            s = [jnp.where(keep, s[hh], NEG_INF) for hh in heads]
        m_prev = [m_ref[hh] for hh in heads]
        m_cur = [jnp.max(s[hh], axis=0, keepdims=True) for hh in heads]
        if has_bias:
            m_cur = [m_cur[hh] + cq_ref[hh] for hh in heads]
        m_new = [jnp.maximum(m_prev[hh], m_cur[hh]) for hh in heads]
        sub = [m_new[hh] - cq_ref[hh] for hh in heads] if has_bias else m_new
        p = [jnp.exp2(s[hh] - sub[hh]).astype(BF16) for hh in heads]
        alpha = [jnp.exp2(m_prev[hh] - m_new[hh]) for hh in heads]
        for hh in heads:
            m_ref[hh] = m_new[hh]
        pv = [_dot(vt_ref[hh, jb], p[hh]) for hh in heads]
        for hh in heads:
            acc_ref[hh] = alpha[hh] * acc_ref[hh] + pv[hh]

    def body(jb, carry):
        step(jb, False)
        return carry

    lax.fori_loop(0, qi, body, 0)
    step(qi, True)
    for hh in heads:
        acc = acc_ref[hh]
        o_ref[:, hh * dv:(hh + 1) * dv] = (acc[:dv] / acc[dv:dv + 1]).T.astype(o_ref.dtype)


ATTN_HG = 4
ATTN_ONES = 16
LOG2E = 1.4426950408889634


def _attention(q, k, v, dk, dv, heads, batch, seq, mask_gran, q_off=0, k_off=0, v_off=0, bias=None):
    t = q.shape[0]
    tq = min(ATTN_T, seq)
    nq = seq // tq
    hg = ATTN_HG
    qb, kb, vb = q_off // (hg * dk), k_off // (hg * dk), v_off // (hg * dv)
    dvx = dv + ATTN_ONES
    in_specs = [pl.BlockSpec((tq, hg * dk), lambda b, g, i: (b * nq + i, qb + g)),
                pl.BlockSpec((seq, hg * dk), lambda b, g, i: (b, kb + g)),
                pl.BlockSpec((seq, hg * dv), lambda b, g, i: (b, vb + g))]
    args = [q, k, v]
    if bias is not None:
        cq, ck = bias
        in_specs += [pl.BlockSpec((None, hg, 1, tq), lambda b, g, i: (b, g, 0, i)),
                     pl.BlockSpec((None, seq, LANES), lambda b, g, i: (g, b, 0))]
        args += [cq, ck]
    return pl.pallas_call(
        functools.partial(_attn_kernel, tq=tq, nq=nq, dk=dk, dv=dv, hg=hg, mask_gran=mask_gran,
                          has_bias=bias is not None),
        grid=(batch, heads // hg, nq),
        in_specs=in_specs,
        out_specs=pl.BlockSpec((tq, hg * dv), lambda b, g, i: (b * nq + i, g)),
        out_shape=jax.ShapeDtypeStruct((t, heads * dv), BF16),
        scratch_shapes=[pltpu.VMEM((hg, 1, tq), F32), pltpu.VMEM((hg, dvx, tq), F32),
                        pltpu.VMEM((hg, nq, dvx, tq), BF16)],
        compiler_params=_cparams(3),
        name="flash_attention",
    )(*args)


RET_HG = 2
RET_TB = 512


def _ret_kernel(q_ref, k_ref, v_ref, g_ref, cos_ref, sin_ref, dec_ref, xi_ref, zeta_ref, gch_ref, gnw_ref,
                o_ref, st_ref, *, tb, hg, c):
    j = pl.program_id(2)

    @pl.when(j == 0)
    def _():
        st_ref[...] = jnp.zeros_like(st_ref)

    cos = cos_ref[...]
    sin = sin_ref[...]
    nch = tb // c
    items = [(hh, ci) for hh in range(hg) for ci in range(nch)]

    def lanes_of(hh):
        return slice(hh * C_DK, (hh + 1) * C_DK)

    def rows_of(ci):
        return slice(ci * c, (ci + 1) * c)

    qh = [_rope_lanes(q_ref[:, lanes_of(hh)], cos, sin) for hh in range(hg)]
    kh = [_rope_lanes(k_ref[:, lanes_of(hh)], cos, sin) for hh in range(hg)]
    qb = [qh[hh][rows_of(ci)].astype(BF16) for hh, ci in items]
    kc = [kh[hh][rows_of(ci)] for hh, ci in items]
    vb = [v_ref[rows_of(ci), lanes_of(hh)].astype(BF16) for hh, ci in items]
    scores = [(_dot_nt(q, k.astype(BF16)) * dec_ref[hh]).astype(BF16) for q, k, (hh, _) in zip(qb, kc, items)]
    inner = [_dot(s, v) for s, v in zip(scores, vb)]
    kzv = [_dot_tn((k * zeta_ref[hh]).astype(BF16), v) for k, v, (hh, _) in zip(kc, vb, items)]
    states = []
    for hh in range(hg):
        state = st_ref[hh]
        for ci in range(nch):
            states.append(state)
            state = state * gch_ref[hh] + kzv[hh * nch + ci]
        st_ref[hh] = state
    cross = [_dot(q, s.astype(BF16)) * xi_ref[hh] for q, s, (hh, _) in zip(qb, states, items)]
    for i, (hh, ci) in enumerate(items):
        o_c = inner[i] + cross[i]
        gate = _silu(g_ref[rows_of(ci), lanes_of(hh)])
        o_ref[rows_of(ci), lanes_of(hh)] = (_rms(o_c, gnw_ref[hh]) * gate).astype(o_ref.dtype)


def _retention(proj, cos_c, sin_c, group_norm, batch, seq):
    t = proj.shape[0]
    tb = min(RET_TB, seq)
    c = min(RET_CHUNK, seq)
    hg = RET_HG
    w = hg * C_DK
    nj = seq // tb
    ng = C_HEADS // hg
    nh = C_HEADS * C_DK // w

    log_gamma = jnp.log1p(-jnp.exp2(-5.0 - jnp.arange(C_HEADS, dtype=F32)))
    idx = jnp.arange(c, dtype=F32)
    rel = idx[:, None] - idx[None, :]
    dec = jnp.where(rel >= 0, jnp.exp(log_gamma[:, None, None] * jnp.maximum(rel, 0.0)), 0.0)
    xi = jnp.exp(log_gamma[:, None] * (idx + 1.0))
    zeta = jnp.exp(log_gamma[:, None] * (c - 1.0 - idx))
    gch = jnp.exp(log_gamma * c)
    xi_b = jnp.broadcast_to(xi[:, :, None], (C_HEADS, c, C_DV))
    zeta_b = jnp.broadcast_to(zeta[:, :, None], (C_HEADS, c, C_DK))
    gch_b = jnp.broadcast_to(gch[:, None, None], (C_HEADS, 1, C_DV))

    def colspec(off):
        return pl.BlockSpec((tb, w), lambda b, g, j: (b * nj + j, off + g))

    tab = pl.BlockSpec((tb, LANES), lambda b, g, j: (b * nj + j, 0))

    def hspec(r, cdim):
        return pl.BlockSpec((hg, r, cdim), lambda b, g, j: (g, 0, 0))

    return pl.pallas_call(
        functools.partial(_ret_kernel, tb=tb, hg=hg, c=c),
        grid=(batch, ng, nj),
        in_specs=[colspec(0), colspec(nh), colspec(2 * nh), colspec(3 * nh), tab, tab,
                  hspec(c, c), hspec(c, C_DV), hspec(c, C_DK), hspec(1, C_DV), hspec(1, C_DV)],
        out_specs=pl.BlockSpec((tb, w), lambda b, g, j: (b * nj + j, g)),
        out_shape=jax.ShapeDtypeStruct((t, C_HEADS * C_DV), BF16),
        scratch_shapes=[pltpu.VMEM((hg, C_DK, C_DV), F32)],
        compiler_params=_cparams(3),
        name="retention",
    )(proj, proj, proj, proj, cos_c, sin_c, dec, xi_b, zeta_b, gch_b, group_norm.reshape(C_HEADS, 1, C_DV))


def _fox_cum_kernel(s_ref, b_ref, o_ref, *, seq, hg):
    x = s_ref[...] + b_ref[...]
    acc = -_softplus(-x)
    row = lax.broadcasted_iota(jnp.int32, acc.shape, 0)
    shift = 1
    while shift < seq:
        acc = acc + jnp.where(row >= shift, pltpu.roll(acc, shift, 0), 0.0)
        shift *= 2
    _group_slabs(acc * LOG2E, o_ref, hg)


def _fox_cum(slab, forget_bias, batch, seq, hg):
    t = slab.shape[0]
    ng = D_HEADS // hg
    pad = jnp.zeros((LANES - D_HEADS,), F32)
    b_row = jnp.concatenate([forget_bias.astype(F32), pad])[None, :]
    return pl.pallas_call(
        functools.partial(_fox_cum_kernel, seq=seq, hg=hg),
        grid=(batch,),
        in_specs=[pl.BlockSpec((seq, LANES), lambda b: (b, 0)),
                  pl.BlockSpec((1, LANES), lambda b: (0, 0))],
        out_specs=pl.BlockSpec((ng, seq, LANES), lambda b: (0, b, 0)),
        out_shape=jax.ShapeDtypeStruct((ng, t, LANES), F32),
        compiler_params=_cparams(1),
        name="fox_cumsum",
    )(slab, b_row)


def _pe_lanes(w_pe):
    k = w_pe.shape[0]
    z = jnp.zeros((k, B_ROPE // 2), w_pe.dtype)
    return jnp.concatenate([w_pe[:, :B_ROPE // 2], z, w_pe[:, B_ROPE // 2:], z], axis=1)


AB_MAIN = A_HEADS * (2 * A_DK + A_DV) + A_HEADS * A_DV


def _ab_weights(w_in, w_uq, w_ukv):
    o = AB_MAIN
    w_a = w_in[:, o:o + A_HEADS]; o += A_HEADS
    w_b = w_in[:, o:o + A_HEADS]; o += A_HEADS
    w_cq = w_in[:, o:o + B_Q_LORA]; o += B_Q_LORA
    w_ckv = w_in[:, o:o + B_KV_LORA]; o += B_KV_LORA
    w_kpe = w_in[:, o:o + B_ROPE]
    d = w_in.shape[0]
    w_tail = jnp.concatenate([w_cq, w_ckv, _pe_lanes(w_kpe), w_a, w_b,
                              jnp.zeros((d, LANES - 2 * A_HEADS), w_in.dtype)], axis=1)
    scale = (B_NOPE + B_ROPE) ** -0.5 * LOG2E
    wq = (w_uq * scale).reshape(B_Q_LORA, B_HEADS, B_NOPE + B_ROPE)
    wq_p = jnp.concatenate(
        [wq[:, :, :B_NOPE],
         _pe_lanes(wq[:, :, B_NOPE:].reshape(B_Q_LORA * B_HEADS, B_ROPE)).reshape(B_Q_LORA, B_HEADS, LANES)],
        axis=2).reshape(B_Q_LORA, B_HEADS * B_QK).astype(BF16)
    wkv = w_ukv.reshape(B_KV_LORA, B_HEADS, B_NOPE + B_DV)
    wkv_p = jnp.concatenate([wkv[:, :, :B_NOPE].reshape(B_KV_LORA, B_HEADS * B_NOPE),
                             wkv[:, :, B_NOPE:].reshape(B_KV_LORA, B_HEADS * B_DV)], axis=1).astype(BF16)
    return w_tail, wq_p, wkv_p


def _layer_ab(h, rope, w_in_all, j, conv_w, a_log, dt_bias, out_norm, q_norm, w_uq, kv_norm, w_ukv, batch, seq):
    cos_b, sin_b, _, _ = rope
    w_tail, wq_p, wkv_p = _ab_weights(w_in_all[j], w_uq, w_ukv)
    proj = _mm(h, w_in_all, F32, layer=j, ncols=AB_MAIN)
    tail = _mm(h, w_tail, F32, tn=w_tail.shape[1])
    cq_block, ckv_block = 0, 1
    kpe_block = (B_Q_LORA + B_KV_LORA) // LANES
    gates = _gdn_gates(tail, kpe_block + 1, a_log, dt_bias, GDN_HG)
    g_row = gates[0][:, :A_HEADS].T.reshape(A_HEADS, 1, h.shape[0])
    o_a = _gdn(proj, conv_w, gates, g_row, out_norm, batch, seq)
    q_b = _mla_q(tail, cq_block, q_norm, wq_p, cos_b, sin_b)
    k_b, v_b = _mla_kv(tail, ckv_block, kv_norm, wkv_p, kpe_block, cos_b, sin_b)
    o_b = _attention(q_b, k_b, v_b, B_QK, B_DV, B_HEADS, batch, seq, CHUNK)
    return [o_a, o_b]


def _layer_cd(h, rope, w_in_all, j, group_norm, forget_bias, batch, seq):
    _, _, cos_c, sin_c = rope
    n = C_HEADS * C_DK
    ones = jnp.ones((n,), F32)
    scale_c = jnp.concatenate([ones, ones * (C_DK ** -0.5), ones, ones])
    scale_d = jnp.concatenate([ones * (D_DH ** -0.5 * LOG2E), ones, ones])
    proj_c = _mm(h, w_in_all, F32, layer=j, ncols=4 * n, colscale=scale_c)
    proj_d = _mm(h, w_in_all, BF16, layer=j, col0=4 * n, ncols=3 * n, colscale=scale_d)
    w_f = jnp.concatenate([w_in_all[j][:, 7 * n:7 * n + D_HEADS],
                           jnp.zeros((w_in_all.shape[1], LANES - D_HEADS), w_in_all.dtype)], axis=1)
    slab = _mm(h, w_f, F32, tn=LANES)
    o_c = _retention(proj_c, cos_c, sin_c, group_norm, batch, seq)
    ck = _fox_cum(slab, forget_bias, batch, seq, ATTN_HG)
    cq = ck[0][:, :D_HEADS].reshape(batch, seq, D_HEADS).transpose(0, 2, 1).reshape(batch, D_HEADS, 1, seq)
    o_d = _attention(proj_d, proj_d, proj_d, D_DH, D_DH, D_HEADS, batch, seq, 1,
                     q_off=0, k_off=D_HEADS * D_DH, v_off=2 * D_HEADS * D_DH, bias=(cq, ck))
    return [o_c, o_d]


def kernel(x, c, positions, ada_w, ada_b, mix_pre_norm, mix_post_norm, ffn_pre_norm, ffn_post_norm, ab_w_in, ab_conv_w, ab_a_log, ab_dt_bias, ab_out_norm, ab_q_norm, ab_w_uq, ab_kv_norm, ab_w_ukv, ab_w_out, cd_w_in, cd_group_norm, cd_forget_bias, cd_w_out, ffn_w_gate, ffn_w_up, ffn_w_down):
    batch, seq, d = x.shape
    depth = ada_w.shape[0]
    t = batch * seq
    rows = 8
    c_pad = jnp.concatenate([c, jnp.zeros((rows - batch, d), c.dtype)], axis=0)
    mod = _ada(c_pad, ada_w, ada_b).reshape(depth, rows, 6, d)
    rope = _rope_tables(positions.astype(F32).reshape(t, 1))
    x2 = x.reshape(t, d)
    h = _prenorm(x2, mod[0], mix_pre_norm[0], seq, 0, 1)
    w_down = ffn_w_down.astype(BF16)
    for layer in range(depth):
        j = layer // 2
        if layer % 2 == 0:
            y_in = _layer_ab(h, rope, ab_w_in, j, ab_conv_w[j], ab_a_log[j], ab_dt_bias[j], ab_out_norm[j],
                             ab_q_norm[j], ab_w_uq[j], ab_kv_norm[j], ab_w_ukv[j], batch, seq)
            w_out = ab_w_out
        else:
            y_in = _layer_cd(h, rope, cd_w_in, j, cd_group_norm[j], cd_forget_bias[j], batch, seq)
            w_out = cd_w_out
        x2, h = _mm_post(y_in, w_out, j, x2, mod[layer], mod[layer], mix_post_norm[layer],
                         ffn_pre_norm[layer], seq, 2, 3, 4, True, tm=512)
        hid = _ffn_up(h, ffn_w_gate, ffn_w_up, layer)
        last = layer == depth - 1
        nxt = layer if last else layer + 1
        x2, h = _mm_post([hid], w_down, layer, x2, mod[layer], mod[nxt],
                         ffn_post_norm[layer], mix_pre_norm[nxt], seq, 5, 0, 1, not last, tm=256)
    return x2.reshape(batch, seq, d)
```

```python
import functools

import jax
import jax.numpy as jnp
import numpy as np
from jax import lax
from jax.experimental import pallas as pl
from jax.experimental.pallas import tpu as pltpu

F32 = jnp.float32
BF16 = jnp.bfloat16

D_MODEL = 2048
CHUNK = 64
EPS = 1e-6
ROPE_BASE = 10000.0
NEG_INF = -1e30

A_HEADS, A_DK, A_DV, A_CONV = 8, 128, 128, 4
B_HEADS, B_NOPE, B_ROPE, B_DV, B_Q_LORA, B_KV_LORA = 8, 128, 64, 128, 512, 512
C_HEADS, C_DK, C_DV = 8, 128, 128
D_HEADS, D_DH = 8, 128
LANES = 128
SUBLANES = 8
B_QK = B_NOPE + LANES

VMEM_LIMIT_BYTES = 56 * 1024 * 1024

RET_CHUNK = 128


def _cparams(n_axes):
    return pltpu.CompilerParams(dimension_semantics=("arbitrary",) * n_axes,
                                vmem_limit_bytes=VMEM_LIMIT_BYTES)


def _dot(a, b):
    return jnp.dot(a, b, preferred_element_type=F32)


def _dot_nt(a, b):
    return lax.dot_general(a, b, (((1,), (1,)), ((), ())), preferred_element_type=F32)


def _dot_tn(a, b):
    return lax.dot_general(a, b, (((0,), (0,)), ((), ())), preferred_element_type=F32)


def _split_bf16(a):
    hi = a.astype(BF16)
    lo = (a - hi.astype(F32)).astype(BF16)
    return hi, lo


def _dot3(a, b):
    ah, al = _split_bf16(a)
    bh, bl = _split_bf16(b)
    return _dot(ah, bh) + _dot(ah, bl) + _dot(al, bh)


def _sigmoid(x):
    return 1.0 / (1.0 + jnp.exp(-x))


def _silu(x):
    return x * _sigmoid(x)


def _softplus(x):
    return jnp.maximum(x, 0.0) + jnp.log(1.0 + jnp.exp(-jnp.abs(x)))


def _rms(x, w):
    return x * lax.rsqrt(jnp.mean(x * x, axis=-1, keepdims=True) + EPS) * w


def _ada_kernel(c_ref, w_ref, b_ref, o_ref):
    cond = _silu(c_ref[...]).astype(BF16)
    o_ref[...] = _dot(cond, w_ref[...].astype(BF16)) + b_ref[...]


def _ada(c_pad, ada_w, ada_b):
    depth, d, n = ada_w.shape
    rows = c_pad.shape[0]
    tn = 1024
    return pl.pallas_call(
        _ada_kernel,
        grid=(depth, n // tn),
        in_specs=[pl.BlockSpec((rows, d), lambda l, j: (0, 0)),
                  pl.BlockSpec((None, d, tn), lambda l, j: (l, 0, j)),
                  pl.BlockSpec((None, 1, tn), lambda l, j: (l, 0, j))],
        out_specs=pl.BlockSpec((None, rows, tn), lambda l, j: (l, 0, j)),
        out_shape=jax.ShapeDtypeStruct((depth, rows, n), F32),
        compiler_params=_cparams(2),
        name="ada_mod",
    )(c_pad, ada_w, ada_b.reshape(depth, 1, n))


def _rope_kernel(pos_ref, f_ref, sb_ref, sc_ref, cb_ref, snb_ref, cc_ref, snc_ref):
    ang = pos_ref[...] * f_ref[...]
    cos = jnp.cos(ang)
    sin = jnp.sin(ang)
    lane = lax.broadcasted_iota(jnp.int32, ang.shape, 1)
    nb, nc = B_ROPE // 2, C_DK // 2
    in_b = (lane < nb) | ((lane >= 2 * nb) & (lane < 3 * nb))

    def lay_b(x):
        return jnp.where(in_b, jnp.where(lane < nb, x, pltpu.roll(x, 2 * nb, 1)), 0.0)

    def lay_c(x):
        return jnp.where(lane < nc, pltpu.roll(x, LANES - nb, 1), pltpu.roll(x, nc - nb, 1))

    cb_ref[...] = lay_b(cos)
    snb_ref[...] = lay_b(sin) * sb_ref[...]
    cc_ref[...] = lay_c(cos)
    snc_ref[...] = lay_c(sin) * sc_ref[...]


def _rope_tables(pos_col):
    t = pos_col.shape[0]
    tb = min(1024, t)
    inv_b = ROPE_BASE ** (-jnp.arange(0, B_ROPE, 2, dtype=F32) / B_ROPE)
    inv_c = ROPE_BASE ** (-jnp.arange(0, C_DK, 2, dtype=F32) / C_DK)
    z32 = jnp.zeros((B_ROPE // 2,), F32)
    o32 = jnp.ones((B_ROPE // 2,), F32)
    o64 = jnp.ones((C_DK // 2,), F32)
    freqs = jnp.concatenate([inv_b, inv_c, z32])[None, :]
    sb = jnp.concatenate([-o32, z32, o32, z32])[None, :]
    sc = jnp.concatenate([-o64, o64])[None, :]
    row = pl.BlockSpec((1, LANES), lambda i: (0, 0))
    tab = pl.BlockSpec((tb, LANES), lambda i: (i, 0))
    shp = jax.ShapeDtypeStruct((t, LANES), F32)
    return pl.pallas_call(
        _rope_kernel,
        grid=(t // tb,),
        in_specs=[pl.BlockSpec((tb, 1), lambda i: (i, 0)), row, row, row],
        out_specs=[tab, tab, tab, tab],
        out_shape=[shp, shp, shp, shp],
        compiler_params=_cparams(1),
        name="rope_tables",
    )(pos_col, freqs, sb, sc)


def _prenorm_kernel(x_ref, mod_ref, w_ref, h_ref, *, shift_idx, scale_idx):
    y = _rms(x_ref[...], w_ref[...])
    h = y * (1.0 + mod_ref[scale_idx:scale_idx + 1, :]) + mod_ref[shift_idx:shift_idx + 1, :]
    h_ref[...] = h.astype(h_ref.dtype)


def _prenorm(x2, mod_l, w, seq, shift_idx, scale_idx):
    t, d = x2.shape
    tm = min(1024, seq)
    bpb = seq // tm
    return pl.pallas_call(
        functools.partial(_prenorm_kernel, shift_idx=shift_idx, scale_idx=scale_idx),
        grid=(t // tm,),
        in_specs=[pl.BlockSpec((tm, d), lambda i: (i, 0)),
                  pl.BlockSpec((None, 6, d), lambda i: (i // bpb, 0, 0)),
                  pl.BlockSpec((1, d), lambda i: (0, 0))],
        out_specs=pl.BlockSpec((tm, d), lambda i: (i, 0)),
        out_shape=jax.ShapeDtypeStruct((t, d), BF16),
        compiler_params=_cparams(1),
        name="prenorm",
    )(x2, mod_l, w.reshape(1, d))


def _mm_kernel(*refs, cast_w, scaled):
    a_ref, w_ref = refs[0], refs[1]
    s_ref = refs[2] if scaled else None
    o_ref = refs[3] if scaled else refs[2]
    if cast_w:
        wb_ref = refs[-1]

        @pl.when(pl.program_id(1) == 0)
        def _():
            wb_ref[...] = w_ref[...].astype(BF16)

        w = wb_ref[...]
    else:
        w = w_ref[...]
    y = _dot(a_ref[...], w)
    if scaled:
        y = y * s_ref[...]
    o_ref[...] = y.astype(o_ref.dtype)


def _mm(a, w, out_dtype, layer=None, col0=0, ncols=None, colscale=None, tm=1024, tn=1024):
    m, k = a.shape
    n = w.shape[-1] if ncols is None else ncols
    tm = min(tm, m)
    tn = min(tn, n)
    jb0 = col0 // tn
    cast_w = w.dtype != BF16
    if layer is None:
        wspec = pl.BlockSpec((k, tn), lambda j, i: (0, jb0 + j))
    else:
        wspec = pl.BlockSpec((None, k, tn), lambda j, i: (layer, 0, jb0 + j))
    in_specs = [pl.BlockSpec((tm, k), lambda j, i: (i, 0)), wspec]
    args = [a, w]
    if colscale is not None:
        in_specs.append(pl.BlockSpec((1, tn), lambda j, i: (0, j)))
        args.append(colscale.reshape(1, n).astype(F32))
    return pl.pallas_call(
        functools.partial(_mm_kernel, cast_w=cast_w, scaled=colscale is not None),
        grid=(n // tn, m // tm),
        in_specs=in_specs,
        out_specs=pl.BlockSpec((tm, tn), lambda j, i: (i, j)),
        out_shape=jax.ShapeDtypeStruct((m, n), out_dtype),
        scratch_shapes=[pltpu.VMEM((k, tn), BF16)] if cast_w else [],
        compiler_params=_cparams(2),
        name="matmul",
    )(*args)


MM_POST_SPLIT = 2


def _mm_post_kernel(*refs, n_in, cast_w, gate_idx, nshift_idx, nscale_idx, emit_h):
    a_refs = refs[:n_in]
    w_refs = refs[n_in:2 * n_in]
    x_ref, mod_ref, modn_ref, pw_ref, nw_ref = refs[2 * n_in:2 * n_in + 5]
    n_out = 2 if emit_h else 1
    outs = refs[2 * n_in + 5:2 * n_in + 5 + n_out]
    if cast_w:
        wb_refs = refs[2 * n_in + 5 + n_out:]

        @pl.when(pl.program_id(0) == 0)
        def _():
            for w_ref, wb_ref in zip(w_refs, wb_refs):
                wb_ref[...] = w_ref[...].astype(BF16)

        w_refs = wb_refs
    tm = x_ref.shape[0]
    for r0 in range(0, tm, tm // MM_POST_SPLIT):
        rows = slice(r0, r0 + tm // MM_POST_SPLIT)
        y = _dot(a_refs[0][rows, :], w_refs[0][...])
        for a_ref, w_ref in zip(a_refs[1:], w_refs[1:]):
            y = y + _dot(a_ref[rows, :], w_ref[...])
        xn = x_ref[rows, :] + mod_ref[gate_idx:gate_idx + 1, :] * _rms(y, pw_ref[...])
        outs[0][rows, :] = xn
        if emit_h:
            hn = _rms(xn, nw_ref[...])
            hn = hn * (1.0 + modn_ref[nscale_idx:nscale_idx + 1, :]) + modn_ref[nshift_idx:nshift_idx + 1, :]
            outs[1][rows, :] = hn.astype(outs[1].dtype)


def _mm_post(a_list, w, w_layer, x2, mod_l, modn_l, post_w, next_w, seq, gate_idx, nshift_idx, nscale_idx, emit_h,
             tm):
    m = a_list[0].shape[0]
    d = w.shape[-1]
    ka = a_list[0].shape[1]
    tm = min(tm, seq)
    bpb = seq // tm
    cast_w = w.dtype != BF16
    row = pl.BlockSpec((1, d), lambda i: (0, 0))
    modspec = pl.BlockSpec((None, 6, d), lambda i: (i // bpb, 0, 0))
    xspec = pl.BlockSpec((tm, d), lambda i: (i, 0))
    a_specs = [pl.BlockSpec((tm, ka), lambda i: (i, 0)) for _ in a_list]
    w_specs = [pl.BlockSpec((None, ka, d), lambda i, r=r: (w_layer, r, 0), pipeline_mode=pl.Buffered(1))
               for r in range(len(a_list))]
    out_specs = [xspec]
    out_shape = [jax.ShapeDtypeStruct((m, d), F32)]
    if emit_h:
        out_specs.append(xspec)
        out_shape.append(jax.ShapeDtypeStruct((m, d), BF16))
    res = pl.pallas_call(
        functools.partial(_mm_post_kernel, n_in=len(a_list), cast_w=cast_w, gate_idx=gate_idx,
                          nshift_idx=nshift_idx, nscale_idx=nscale_idx, emit_h=emit_h),
        grid=(m // tm,),
        in_specs=a_specs + w_specs + [xspec, modspec, modspec, row, row],
        out_specs=out_specs,
        out_shape=out_shape,
        scratch_shapes=[pltpu.VMEM((ka, d), BF16) for _ in a_list] if cast_w else [],
        compiler_params=_cparams(1),
        name="matmul_post",
    )(*a_list, *([w] * len(a_list)), x2, mod_l, modn_l, post_w.reshape(1, d), next_w.reshape(1, d))
    return res if emit_h else (res[0], None)


def _ffn_up_kernel(a_ref, wg_ref, wu_ref, o_ref, wgb_ref, wub_ref):
    @pl.when(pl.program_id(1) == 0)
    def _():
        wgb_ref[...] = wg_ref[...].astype(BF16)
        wub_ref[...] = wu_ref[...].astype(BF16)

    a = a_ref[...]
    g = _dot(a, wgb_ref[...])
    u = _dot(a, wub_ref[...])
    o_ref[...] = (_silu(g) * u).astype(o_ref.dtype)


def _ffn_up(a, wg, wu, layer, tm=1024, tn=512):
    m, k = a.shape
    n = wg.shape[-1]
    tm = min(tm, m)
    wspec = pl.BlockSpec((None, k, tn), lambda j, i: (layer, 0, j))
    return pl.pallas_call(
        _ffn_up_kernel,
        grid=(n // tn, m // tm),
        in_specs=[pl.BlockSpec((tm, k), lambda j, i: (i, 0)), wspec, wspec],
        out_specs=pl.BlockSpec((tm, tn), lambda j, i: (i, j)),
        out_shape=jax.ShapeDtypeStruct((m, n), BF16),
        scratch_shapes=[pltpu.VMEM((k, tn), BF16), pltpu.VMEM((k, tn), BF16)],
        compiler_params=_cparams(2),
        name="ffn_up",
    )(a, wg, wu)


def _group_slabs(x, o_ref, heads_per_group):
    for g in range(o_ref.shape[0]):
        shift = (LANES - g * heads_per_group) % LANES
        o_ref[g] = x if shift == 0 else pltpu.roll(x, shift, 1)


def _gdn_gate_kernel(s_ref, alog_ref, dt_ref, o_ref, *, hg):
    s = s_ref[...]
    g = -jnp.exp(alog_ref[...]) * _softplus(s + dt_ref[...])
    beta = _sigmoid(s)
    lane = lax.broadcasted_iota(jnp.int32, s.shape, 1)
    _group_slabs(jnp.where(lane < A_HEADS, g, beta), o_ref, hg)


def _gdn_gates(slab_src, col_block, a_log, dt_bias, hg):
    t = slab_src.shape[0]
    tb = min(1024, t)
    ng = A_HEADS // hg
    pad = jnp.zeros((LANES - A_HEADS,), F32)
    alog_row = jnp.concatenate([a_log.astype(F32), pad])[None, :]
    dt_row = jnp.concatenate([dt_bias.astype(F32), pad])[None, :]
    row = pl.BlockSpec((1, LANES), lambda i: (0, 0))
    return pl.pallas_call(
        functools.partial(_gdn_gate_kernel, hg=hg),
        grid=(t // tb,),
        in_specs=[pl.BlockSpec((tb, LANES), lambda i: (i, col_block)), row, row],
        out_specs=pl.BlockSpec((ng, tb, LANES), lambda i: (0, i, 0)),
        out_shape=jax.ShapeDtypeStruct((ng, t, LANES), F32),
        compiler_params=_cparams(1),
        name="gdn_gates",
    )(slab_src, alog_row, dt_row)


GDN_HG = 4
GDN_TB = 512


def _unit_lower_inverses(n_mats):
    c = n_mats[0].shape[0]
    r = lax.broadcasted_iota(jnp.int32, (c, c), 0)
    col = lax.broadcasted_iota(jnp.int32, (c, c), 1)
    eye = jnp.where(r == col, 1.0, 0.0)

    def dot1(a, b):
        return _dot(a.astype(BF16), b.astype(BF16))

    ps = [eye - n for n in n_mats]
    ms = [dot1(n, n) for n in n_mats]
    power = 2
    while True:
        ps = [p + dot1(p, m) for p, m in zip(ps, ms)]
        power *= 2
        if power >= c:
            break
        ms = [dot1(m, m) for m in ms]
    res = [eye - p - _dot3(n, p) for n, p in zip(n_mats, ps)]
    return [p + dot1(p, e) for p, e in zip(ps, res)]


def _gdn_kernel(q_ref, k_ref, v_ref, z_ref, cwq_ref, cwk_ref, cwv_ref, gates_ref, grow_ref, onw_ref,
                o_ref, xq_ref, xk_ref, xv_ref, st_ref, *, tb, hg):
    j = pl.program_id(2)

    @pl.when(j == 0)
    def _():
        st_ref[...] = jnp.zeros_like(st_ref)
        for h_ref in (xq_ref, xk_ref, xv_ref):
            h_ref[...] = jnp.zeros_like(h_ref)

    def conv_silu(src_ref, hist_ref, cw_ref):
        x = src_ref[...]
        prev = hist_ref[...]
        hrows = prev.shape[0]
        row = lax.broadcasted_iota(jnp.int32, prev.shape, 0)
        acc = x * cw_ref[A_CONV - 1:A_CONV, :]
        for s in range(1, A_CONV):
            xs = pltpu.roll(x, s, 0)
            top = jnp.where(row < s, pltpu.roll(prev, s, 0), xs[0:hrows])
            shifted = jnp.concatenate([top, xs[hrows:]], axis=0)
            acc = acc + shifted * cw_ref[A_CONV - 1 - s:A_CONV - s, :]
        hist_ref[...] = x[tb - hrows:tb]
        return _silu(acc)

    q_all = conv_silu(q_ref, xq_ref, cwq_ref)
    k_all = conv_silu(k_ref, xk_ref, cwk_ref)
    v_all = conv_silu(v_ref, xv_ref, cwv_ref)

    c = CHUNK
    r = lax.broadcasted_iota(jnp.int32, (c, c), 0)
    col = lax.broadcasted_iota(jnp.int32, (c, c), 1)
    tri = r >= col
    strict = r > col

    nch = tb // c
    items = [(hh, ci) for hh in range(hg) for ci in range(nch)]
    qn, kn, vn = [], [], []
    for hh in range(hg):
        lanes = slice(hh * A_DK, (hh + 1) * A_DK)
        qh = q_all[:, lanes]
        kh = k_all[:, lanes]
        qn.append(qh * lax.rsqrt(jnp.sum(qh * qh, axis=-1, keepdims=True) + EPS) * (A_DK ** -0.5))
        kn.append(kh * lax.rsqrt(jnp.sum(kh * kh, axis=-1, keepdims=True) + EPS))
        vn.append(v_all[:, lanes])

    def rows_of(ci):
        return slice(ci * c, (ci + 1) * c)

    qc = [qn[hh][rows_of(ci)] for hh, ci in items]
    kc = [kn[hh][rows_of(ci)] for hh, ci in items]
    vc = [vn[hh][rows_of(ci)] for hh, ci in items]
    b_col = [gates_ref[rows_of(ci), A_HEADS + hh:A_HEADS + hh + 1] for hh, ci in items]
    gc_col = [jnp.sum(jnp.where(tri, grow_ref[hh][:, rows_of(ci)], 0.0), axis=1, keepdims=True)
              for hh, ci in items]
    gc_row = [jnp.sum(jnp.where(r <= col, gates_ref[rows_of(ci), hh:hh + 1], 0.0), axis=0, keepdims=True)
              for hh, ci in items]
    decay = [jnp.where(tri, jnp.exp(jnp.minimum(a - b, 0.0)), 0.0) for a, b in zip(gc_col, gc_row)]
    e_gc = [jnp.exp(a) for a in gc_col]
    g_last = [a[c - 1:c, :] for a in gc_col]
    kb = [x.astype(BF16) for x in kc]
    kk = [_dot_nt(x, x) for x in kb]
    qk = [_dot_nt(x.astype(BF16), y) for x, y in zip(qc, kb)]
    n_mat = [jnp.where(strict, b * x * d, 0.0) for b, x, d in zip(b_col, kk, decay)]
    qk = [jnp.where(tri, x * d, 0.0).astype(BF16) for x, d in zip(qk, decay)]
    t_inv = _unit_lower_inverses(n_mat)
    wu = [_dot3(t, jnp.concatenate([k * (b * e), v * b], axis=1)).astype(BF16)
          for t, k, v, b, e in zip(t_inv, kc, vc, b_col, e_gc)]
    kdb = [(k * jnp.exp(gl - g)).astype(BF16) for k, gl, g in zip(kc, g_last, gc_col)]
    mb = [_dot_tn(kd, x) for kd, x in zip(kdb, wu)]
    m_c = [x[:, :A_DK].astype(BF16) for x in mb]
    b_c = [x[:, A_DK:] for x in mb]
    qwu = [_dot(a, x) for a, x in zip(qk, wu)]
    q_eff = [(q * e - x[:, :A_DK]).astype(BF16) for q, e, x in zip(qc, e_gc, qwu)]
    qku = [x[:, A_DK:] for x in qwu]
    e_gl = [jnp.exp(x) for x in g_last]
    states = [st_ref[hh] for hh in range(hg)]
    o_c = [None] * len(items)
    for ci in range(nch):
        idx = [hh * nch + ci for hh in range(hg)]
        sb = [s.astype(BF16) for s in states]
        for hh in range(hg):
            o_c[idx[hh]] = _dot(q_eff[idx[hh]], sb[hh]) + qku[idx[hh]]
        states = [states[hh] * e_gl[idx[hh]] + b_c[idx[hh]] - _dot(m_c[idx[hh]], sb[hh]) for hh in range(hg)]
    for hh in range(hg):
        st_ref[hh] = states[hh]
    for i, (hh, ci) in enumerate(items):
        lanes = slice(hh * A_DK, (hh + 1) * A_DK)
        zc = z_ref[rows_of(ci), lanes]
        o_ref[rows_of(ci), lanes] = (_rms(o_c[i], onw_ref[...]) * _silu(zc)).astype(o_ref.dtype)


def _gdn(proj, conv_w, gates, g_row, out_norm, batch, seq):
    t = proj.shape[0]
    tb = min(GDN_TB, seq)
    hg = GDN_HG
    w = hg * A_DK
    nj = seq // tb
    ng = A_HEADS // hg
    kq = A_HEADS * A_DK
    off_k = kq // w
    off_v = 2 * kq // w
    off_z = 3 * kq // w

    def colspec(off):
        return pl.BlockSpec((tb, w), lambda b, g, j: (b * nj + j, off + g))

    def cwspec(off):
        return pl.BlockSpec((A_CONV, w), lambda b, g, j: (0, off + g))

    colv = pl.BlockSpec((None, tb, LANES), lambda b, g, j: (g, b * nj + j, 0))
    rowv = pl.BlockSpec((hg, 1, tb), lambda b, g, j: (g, 0, b * nj + j))
    return pl.pallas_call(
        functools.partial(_gdn_kernel, tb=tb, hg=hg),
        grid=(batch, ng, nj),
        in_specs=[colspec(0), colspec(off_k), colspec(off_v), colspec(off_z),
                  cwspec(0), cwspec(off_k), cwspec(off_v), colv, rowv,
                  pl.BlockSpec((1, A_DV), lambda b, g, j: (0, 0))],
        out_specs=pl.BlockSpec((tb, w), lambda b, g, j: (b * nj + j, g)),
        out_shape=jax.ShapeDtypeStruct((t, A_HEADS * A_DV), BF16),
        scratch_shapes=[pltpu.VMEM((SUBLANES, w), F32), pltpu.VMEM((SUBLANES, w), F32),
                        pltpu.VMEM((SUBLANES, w), F32), pltpu.VMEM((hg, A_DK, A_DV), F32)],
        compiler_params=_cparams(3),
        name="gated_deltanet",
    )(proj, proj, proj, proj, conv_w, conv_w, conv_w, gates, g_row, out_norm.reshape(1, A_DV))


def _rope_lanes(x, cos, sin_signed):
    return x * cos + pltpu.roll(x, LANES // 2, 1) * sin_signed


def _mla_q_kernel(cq_ref, nw_ref, w_ref, cos_ref, sin_ref, o_ref):
    n = _rms(cq_ref[...], nw_ref[...]).astype(BF16)
    cos = cos_ref[...]
    sin = sin_ref[...]
    for h in range(B_HEADS):
        qh = _dot(n, w_ref[:, h * B_QK:(h + 1) * B_QK])
        o_ref[:, h * B_QK:h * B_QK + B_NOPE] = qh[:, :B_NOPE].astype(o_ref.dtype)
        o_ref[:, h * B_QK + B_NOPE:(h + 1) * B_QK] = _rope_lanes(qh[:, B_NOPE:], cos, sin).astype(o_ref.dtype)


def _mla_q(proj, col_block, q_norm, w_uq_p, cos_b, sin_b):
    t = proj.shape[0]
    tm = min(1024, t)
    n = w_uq_p.shape[1]
    return pl.pallas_call(
        _mla_q_kernel,
        grid=(t // tm,),
        in_specs=[pl.BlockSpec((tm, B_Q_LORA), lambda i: (i, col_block)),
                  pl.BlockSpec((1, B_Q_LORA), lambda i: (0, 0)),
                  pl.BlockSpec((B_Q_LORA, n), lambda i: (0, 0)),
                  pl.BlockSpec((tm, LANES), lambda i: (i, 0)),
                  pl.BlockSpec((tm, LANES), lambda i: (i, 0))],
        out_specs=pl.BlockSpec((tm, n), lambda i: (i, 0)),
        out_shape=jax.ShapeDtypeStruct((t, n), BF16),
        compiler_params=_cparams(1),
        name="mla_q_up",
    )(proj, q_norm.reshape(1, B_Q_LORA), w_uq_p, cos_b, sin_b)


def _mla_kv_kernel(ckv_ref, nw_ref, w_ref, kpe_ref, cos_ref, sin_ref, k_ref, v_ref):
    n = _rms(ckv_ref[...], nw_ref[...]).astype(BF16)
    kpe = _rope_lanes(kpe_ref[...], cos_ref[...], sin_ref[...]).astype(k_ref.dtype)
    nk = B_HEADS * B_NOPE
    kn = _dot(n, w_ref[:, :nk])
    for h in range(B_HEADS):
        k_ref[:, h * B_QK:h * B_QK + B_NOPE] = kn[:, h * B_NOPE:(h + 1) * B_NOPE].astype(k_ref.dtype)
        k_ref[:, h * B_QK + B_NOPE:(h + 1) * B_QK] = kpe
    v_ref[...] = _dot(n, w_ref[:, nk:]).astype(v_ref.dtype)


def _mla_kv(proj, col_block, kv_norm, w_ukv_p, kpe_block, cos_b, sin_b):
    t = proj.shape[0]
    tm = min(1024, t)
    n = w_ukv_p.shape[1]
    tab = pl.BlockSpec((tm, LANES), lambda i: (i, 0))
    kpe_spec = pl.BlockSpec((tm, LANES), lambda i: (i, kpe_block))
    return pl.pallas_call(
        _mla_kv_kernel,
        grid=(t // tm,),
        in_specs=[pl.BlockSpec((tm, B_KV_LORA), lambda i: (i, col_block)),
                  pl.BlockSpec((1, B_KV_LORA), lambda i: (0, 0)),
                  pl.BlockSpec((B_KV_LORA, n), lambda i: (0, 0)),
                  kpe_spec, tab, tab],
        out_specs=[pl.BlockSpec((tm, B_HEADS * B_QK), lambda i: (i, 0)),
                   pl.BlockSpec((tm, B_HEADS * B_DV), lambda i: (i, 0))],
        out_shape=[jax.ShapeDtypeStruct((t, B_HEADS * B_QK), BF16),
                   jax.ShapeDtypeStruct((t, B_HEADS * B_DV), BF16)],
        compiler_params=_cparams(1),
        name="mla_kv_up",
    )(proj, kv_norm.reshape(1, B_KV_LORA), w_ukv_p, proj, cos_b, sin_b)


ATTN_T = 512


def _attn_kernel(*refs, tq, nq, dk, dv, hg, mask_gran, has_bias):
    if has_bias:
        q_ref, k_ref, v_ref, cq_ref, ck_ref, o_ref, m_ref, acc_ref, vt_ref = refs
    else:
        q_ref, k_ref, v_ref, o_ref, m_ref, acc_ref, vt_ref = refs
    qi = pl.program_id(2)

    @pl.when(qi == 0)
    def _():
        for hh in range(hg):
            for jb in range(nq):
                vblk = v_ref[jb * tq:(jb + 1) * tq, hh * dv:(hh + 1) * dv]
                vt_ref[hh, jb, 0:dv, :] = vblk.astype(F32).T.astype(BF16)
                vt_ref[hh, jb, dv:, :] = jnp.ones((ATTN_ONES, tq), BF16)

    m_ref[...] = jnp.full(m_ref.shape, NEG_INF, F32)
    acc_ref[...] = jnp.zeros_like(acc_ref)
    heads = range(hg)

    def step(jb, masked):
        start = pl.multiple_of(jb * tq, tq)
        s = [_dot_nt(k_ref[pl.ds(start, tq), hh * dk:(hh + 1) * dk], q_ref[:, hh * dk:(hh + 1) * dk])
             for hh in heads]
        if has_bias:
            s = [s[hh] - ck_ref[pl.ds(start, tq), hh:hh + 1] for hh in heads]
        if masked:
            r = lax.broadcasted_iota(jnp.int32, (tq, tq), 0)
            c = lax.broadcasted_iota(jnp.int32, (tq, tq), 1)
            if mask_gran > 1:
                shift = int(np.log2(mask_gran))
                r = lax.shift_right_logical(r, shift)
                c = lax.shift_right_logical(c, shift)
            keep = r <= c
            s = [jnp.where(keep, s[hh], NEG_INF) for hh in heads]
        m_prev = [m_ref[hh] for hh in heads]
        m_cur = [jnp.max(s[hh], axis=0, keepdims=True) for hh in heads]
        if has_bias:
            m_cur = [m_cur[hh] + cq_ref[hh] for hh in heads]
        m_new = [jnp.maximum(m_prev[hh], m_cur[hh]) for hh in heads]
        sub = [m_new[hh] - cq_ref[hh] for hh in heads] if has_bias else m_new
        p = [jnp.exp2(s[hh] - sub[hh]).astype(BF16) for hh in heads]
        alpha = [jnp.exp2(m_prev[hh] - m_new[hh]) for hh in heads]
        for hh in heads:
            m_ref[hh] = m_new[hh]
        pv = [_dot(vt_ref[hh, jb], p[hh]) for hh in heads]
        for hh in heads:
            acc_ref[hh] = alpha[hh] * acc_ref[hh] + pv[hh]

    def body(jb, carry):
        step(jb, False)
        return carry

    lax.fori_loop(0, qi, body, 0)
    step(qi, True)
    for hh in heads:
        acc = acc_ref[hh]
        o_ref[:, hh * dv:(hh + 1) * dv] = (acc[:dv] / acc[dv:dv + 1]).T.astype(o_ref.dtype)


ATTN_HG = 4
ATTN_ONES = 16
LOG2E = 1.4426950408889634


def _attention(q, k, v, dk, dv, heads, batch, seq, mask_gran, q_off=0, k_off=0, v_off=0, bias=None):
    t = q.shape[0]
    tq = min(ATTN_T, seq)
    nq = seq // tq
    hg = ATTN_HG
    qb, kb, vb = q_off // (hg * dk), k_off // (hg * dk), v_off // (hg * dv)
    dvx = dv + ATTN_ONES
    in_specs = [pl.BlockSpec((tq, hg * dk), lambda b, g, i: (b * nq + i, qb + g)),
                pl.BlockSpec((seq, hg * dk), lambda b, g, i: (b, kb + g)),
                pl.BlockSpec((seq, hg * dv), lambda b, g, i: (b, vb + g))]
    args = [q, k, v]
    if bias is not None:
        cq, ck = bias
        in_specs += [pl.BlockSpec((None, hg, 1, tq), lambda b, g, i: (b, g, 0, i)),
                     pl.BlockSpec((None, seq, LANES), lambda b, g, i: (g, b, 0))]
        args += [cq, ck]
    return pl.pallas_call(
        functools.partial(_attn_kernel, tq=tq, nq=nq, dk=dk, dv=dv, hg=hg, mask_gran=mask_gran,
                          has_bias=bias is not None),
        grid=(batch, heads // hg, nq),
        in_specs=in_specs,
        out_specs=pl.BlockSpec((tq, hg * dv), lambda b, g, i: (b * nq + i, g)),
        out_shape=jax.ShapeDtypeStruct((t, heads * dv), BF16),
        scratch_shapes=[pltpu.VMEM((hg, 1, tq), F32), pltpu.VMEM((hg, dvx, tq), F32),
                        pltpu.VMEM((hg, nq, dvx, tq), BF16)],
        compiler_params=_cparams(3),
        name="flash_attention",
    )(*args)


RET_HG = 4
RET_TB = 512


def _ret_kernel(q_ref, k_ref, v_ref, g_ref, cos_ref, sin_ref, dec_ref, xi_ref, zeta_ref, gch_ref, gnw_ref,
                o_ref, st_ref, *, tb, hg, c):
    j = pl.program_id(2)

    @pl.when(j == 0)
    def _():
        st_ref[...] = jnp.zeros_like(st_ref)

    cos = cos_ref[...]
    sin = sin_ref[...]
    nch = tb // c
    items = [(hh, ci) for hh in range(hg) for ci in range(nch)]

    def lanes_of(hh):
        return slice(hh * C_DK, (hh + 1) * C_DK)

    def rows_of(ci):
        return slice(ci * c, (ci + 1) * c)

    qh = [_rope_lanes(q_ref[:, lanes_of(hh)], cos, sin) for hh in range(hg)]
    kh = [_rope_lanes(k_ref[:, lanes_of(hh)], cos, sin) for hh in range(hg)]
    qb = [qh[hh][rows_of(ci)].astype(BF16) for hh, ci in items]
    kc = [kh[hh][rows_of(ci)] for hh, ci in items]
    vb = [v_ref[rows_of(ci), lanes_of(hh)].astype(BF16) for hh, ci in items]
    scores = [(_dot_nt(q, k.astype(BF16)) * dec_ref[hh]).astype(BF16) for q, k, (hh, _) in zip(qb, kc, items)]
    inner = [_dot(s, v) for s, v in zip(scores, vb)]
    kzv = [_dot_tn((k * zeta_ref[hh]).astype(BF16), v) for k, v, (hh, _) in zip(kc, vb, items)]
    states = []
    for hh in range(hg):
        state = st_ref[hh]
        for ci in range(nch):
            states.append(state)
            state = state * gch_ref[hh] + kzv[hh * nch + ci]
        st_ref[hh] = state
    cross = [_dot(q, s.astype(BF16)) * xi_ref[hh] for q, s, (hh, _) in zip(qb, states, items)]
    for i, (hh, ci) in enumerate(items):
        o_c = inner[i] + cross[i]
        gate = _silu(g_ref[rows_of(ci), lanes_of(hh)])
        o_ref[rows_of(ci), lanes_of(hh)] = (_rms(o_c, gnw_ref[hh]) * gate).astype(o_ref.dtype)


def _retention(proj, cos_c, sin_c, group_norm, batch, seq):
    t = proj.shape[0]
    tb = min(RET_TB, seq)
    c = min(RET_CHUNK, seq)
    hg = RET_HG
    w = hg * C_DK
    nj = seq // tb
    ng = C_HEADS // hg
    nh = C_HEADS * C_DK // w

    log_gamma = jnp.log1p(-jnp.exp2(-5.0 - jnp.arange(C_HEADS, dtype=F32)))
    idx = jnp.arange(c, dtype=F32)
    rel = idx[:, None] - idx[None, :]
    dec = jnp.where(rel >= 0, jnp.exp(log_gamma[:, None, None] * jnp.maximum(rel, 0.0)), 0.0)
    xi = jnp.exp(log_gamma[:, None] * (idx + 1.0))
    zeta = jnp.exp(log_gamma[:, None] * (c - 1.0 - idx))
    gch = jnp.exp(log_gamma * c)
    xi_b = jnp.broadcast_to(xi[:, :, None], (C_HEADS, c, C_DV))
    zeta_b = jnp.broadcast_to(zeta[:, :, None], (C_HEADS, c, C_DK))
    gch_b = jnp.broadcast_to(gch[:, None, None], (C_HEADS, 1, C_DV))

    def colspec(off):
        return pl.BlockSpec((tb, w), lambda b, g, j: (b * nj + j, off + g))

    tab = pl.BlockSpec((tb, LANES), lambda b, g, j: (b * nj + j, 0))

    def hspec(r, cdim):
        return pl.BlockSpec((hg, r, cdim), lambda b, g, j: (g, 0, 0))

    return pl.pallas_call(
        functools.partial(_ret_kernel, tb=tb, hg=hg, c=c),
        grid=(batch, ng, nj),
        in_specs=[colspec(0), colspec(nh), colspec(2 * nh), colspec(3 * nh), tab, tab,
                  hspec(c, c), hspec(c, C_DV), hspec(c, C_DK), hspec(1, C_DV), hspec(1, C_DV)],
        out_specs=pl.BlockSpec((tb, w), lambda b, g, j: (b * nj + j, g)),
        out_shape=jax.ShapeDtypeStruct((t, C_HEADS * C_DV), BF16),
        scratch_shapes=[pltpu.VMEM((hg, C_DK, C_DV), F32)],
        compiler_params=_cparams(3),
        name="retention",
    )(proj, proj, proj, proj, cos_c, sin_c, dec, xi_b, zeta_b, gch_b, group_norm.reshape(C_HEADS, 1, C_DV))


def _fox_cum_kernel(s_ref, b_ref, o_ref, *, seq, hg):
    x = s_ref[...] + b_ref[...]
    acc = -_softplus(-x)
    row = lax.broadcasted_iota(jnp.int32, acc.shape, 0)
    shift = 1
    while shift < seq:
        acc = acc + jnp.where(row >= shift, pltpu.roll(acc, shift, 0), 0.0)
        shift *= 2
    _group_slabs(acc * LOG2E, o_ref, hg)


def _fox_cum(slab, forget_bias, batch, seq, hg):
    t = slab.shape[0]
    ng = D_HEADS // hg
    pad = jnp.zeros((LANES - D_HEADS,), F32)
    b_row = jnp.concatenate([forget_bias.astype(F32), pad])[None, :]
    return pl.pallas_call(
        functools.partial(_fox_cum_kernel, seq=seq, hg=hg),
        grid=(batch,),
        in_specs=[pl.BlockSpec((seq, LANES), lambda b: (b, 0)),
                  pl.BlockSpec((1, LANES), lambda b: (0, 0))],
        out_specs=pl.BlockSpec((ng, seq, LANES), lambda b: (0, b, 0)),
        out_shape=jax.ShapeDtypeStruct((ng, t, LANES), F32),
        compiler_params=_cparams(1),
        name="fox_cumsum",
    )(slab, b_row)


def _pe_lanes(w_pe):
    k = w_pe.shape[0]
    z = jnp.zeros((k, B_ROPE // 2), w_pe.dtype)
    return jnp.concatenate([w_pe[:, :B_ROPE // 2], z, w_pe[:, B_ROPE // 2:], z], axis=1)


AB_MAIN = A_HEADS * (2 * A_DK + A_DV) + A_HEADS * A_DV


def _ab_weights(w_in, w_uq, w_ukv):
    o = AB_MAIN
    w_a = w_in[:, o:o + A_HEADS]; o += A_HEADS
    w_b = w_in[:, o:o + A_HEADS]; o += A_HEADS
    w_cq = w_in[:, o:o + B_Q_LORA]; o += B_Q_LORA
    w_ckv = w_in[:, o:o + B_KV_LORA]; o += B_KV_LORA
    w_kpe = w_in[:, o:o + B_ROPE]
    d = w_in.shape[0]
    w_tail = jnp.concatenate([w_cq, w_ckv, _pe_lanes(w_kpe), w_a, w_b,
                              jnp.zeros((d, LANES - 2 * A_HEADS), w_in.dtype)], axis=1)
    scale = (B_NOPE + B_ROPE) ** -0.5 * LOG2E
    wq = (w_uq * scale).reshape(B_Q_LORA, B_HEADS, B_NOPE + B_ROPE)
    wq_p = jnp.concatenate(
        [wq[:, :, :B_NOPE],
         _pe_lanes(wq[:, :, B_NOPE:].reshape(B_Q_LORA * B_HEADS, B_ROPE)).reshape(B_Q_LORA, B_HEADS, LANES)],
        axis=2).reshape(B_Q_LORA, B_HEADS * B_QK).astype(BF16)
    wkv = w_ukv.reshape(B_KV_LORA, B_HEADS, B_NOPE + B_DV)
    wkv_p = jnp.concatenate([wkv[:, :, :B_NOPE].reshape(B_KV_LORA, B_HEADS * B_NOPE),
                             wkv[:, :, B_NOPE:].reshape(B_KV_LORA, B_HEADS * B_DV)], axis=1).astype(BF16)
    return w_tail, wq_p, wkv_p


def _layer_ab(h, rope, w_in_all, j, conv_w, a_log, dt_bias, out_norm, q_norm, w_uq, kv_norm, w_ukv, batch, seq):
    cos_b, sin_b, _, _ = rope
    w_tail, wq_p, wkv_p = _ab_weights(w_in_all[j], w_uq, w_ukv)
    proj = _mm(h, w_in_all, F32, layer=j, ncols=AB_MAIN)
    tail = _mm(h, w_tail, F32, tn=w_tail.shape[1])
    cq_block, ckv_block = 0, 1
    kpe_block = (B_Q_LORA + B_KV_LORA) // LANES
    gates = _gdn_gates(tail, kpe_block + 1, a_log, dt_bias, GDN_HG)
    g_row = gates[0][:, :A_HEADS].T.reshape(A_HEADS, 1, h.shape[0])
    o_a = _gdn(proj, conv_w, gates, g_row, out_norm, batch, seq)
    q_b = _mla_q(tail, cq_block, q_norm, wq_p, cos_b, sin_b)
    k_b, v_b = _mla_kv(tail, ckv_block, kv_norm, wkv_p, kpe_block, cos_b, sin_b)
    o_b = _attention(q_b, k_b, v_b, B_QK, B_DV, B_HEADS, batch, seq, CHUNK)
    return [o_a, o_b]


def _layer_cd(h, rope, w_in_all, j, group_norm, forget_bias, batch, seq):
    _, _, cos_c, sin_c = rope
    n = C_HEADS * C_DK
    ones = jnp.ones((n,), F32)
    scale_c = jnp.concatenate([ones, ones * (C_DK ** -0.5), ones, ones])
    scale_d = jnp.concatenate([ones * (D_DH ** -0.5 * LOG2E), ones, ones])
    proj_c = _mm(h, w_in_all, F32, layer=j, ncols=4 * n, colscale=scale_c)
    proj_d = _mm(h, w_in_all, BF16, layer=j, col0=4 * n, ncols=3 * n, colscale=scale_d)
    w_f = jnp.concatenate([w_in_all[j][:, 7 * n:7 * n + D_HEADS],
                           jnp.zeros((w_in_all.shape[1], LANES - D_HEADS), w_in_all.dtype)], axis=1)
    slab = _mm(h, w_f, F32, tn=LANES)
    o_c = _retention(proj_c, cos_c, sin_c, group_norm, batch, seq)
    ck = _fox_cum(slab, forget_bias, batch, seq, ATTN_HG)
    cq = ck[0][:, :D_HEADS].reshape(batch, seq, D_HEADS).transpose(0, 2, 1).reshape(batch, D_HEADS, 1, seq)
    o_d = _attention(proj_d, proj_d, proj_d, D_DH, D_DH, D_HEADS, batch, seq, 1,
                     q_off=0, k_off=D_HEADS * D_DH, v_off=2 * D_HEADS * D_DH, bias=(cq, ck))
    return [o_c, o_d]


def kernel(x, c, positions, ada_w, ada_b, mix_pre_norm, mix_post_norm, ffn_pre_norm, ffn_post_norm, ab_w_in, ab_conv_w, ab_a_log, ab_dt_bias, ab_out_norm, ab_q_norm, ab_w_uq, ab_kv_norm, ab_w_ukv, ab_w_out, cd_w_in, cd_group_norm, cd_forget_bias, cd_w_out, ffn_w_gate, ffn_w_up, ffn_w_down):
    batch, seq, d = x.shape
    depth = ada_w.shape[0]
    t = batch * seq
    rows = 8
    c_pad = jnp.concatenate([c, jnp.zeros((rows - batch, d), c.dtype)], axis=0)
    mod = _ada(c_pad, ada_w, ada_b).reshape(depth, rows, 6, d)
    rope = _rope_tables(positions.astype(F32).reshape(t, 1))
    x2 = x.reshape(t, d)
    h = _prenorm(x2, mod[0], mix_pre_norm[0], seq, 0, 1)
    w_down = ffn_w_down.astype(BF16)
    for layer in range(depth):
        j = layer // 2
        if layer % 2 == 0:
            y_in = _layer_ab(h, rope, ab_w_in, j, ab_conv_w[j], ab_a_log[j], ab_dt_bias[j], ab_out_norm[j],
                             ab_q_norm[j], ab_w_uq[j], ab_kv_norm[j], ab_w_ukv[j], batch, seq)
            w_out = ab_w_out
        else:
            y_in = _layer_cd(h, rope, cd_w_in, j, cd_group_norm[j], cd_forget_bias[j], batch, seq)
            w_out = cd_w_out
        x2, h = _mm_post(y_in, w_out, j, x2, mod[layer], mod[layer], mix_post_norm[layer],
                         ffn_pre_norm[layer], seq, 2, 3, 4, True, tm=512)
        hid = _ffn_up(h, ffn_w_gate, ffn_w_up, layer)
        last = layer == depth - 1
        nxt = layer if last else layer + 1
        x2, h = _mm_post([hid], w_down, layer, x2, mod[layer], mod[nxt],
                         ffn_post_norm[layer], mix_pre_norm[nxt], seq, 5, 0, 1, not last, tm=256)
    return x2.reshape(batch, seq, d)
```

```python
import functools

import jax
import jax.numpy as jnp
import numpy as np
from jax import lax
from jax.experimental import pallas as pl
from jax.experimental.pallas import tpu as pltpu

F32 = jnp.float32
BF16 = jnp.bfloat16

D_MODEL = 2048
CHUNK = 64
EPS = 1e-6
ROPE_BASE = 10000.0
NEG_INF = -1e30

A_HEADS, A_DK, A_DV, A_CONV = 8, 128, 128, 4
B_HEADS, B_NOPE, B_ROPE, B_DV, B_Q_LORA, B_KV_LORA = 8, 128, 64, 128, 512, 512
C_HEADS, C_DK, C_DV = 8, 128, 128
D_HEADS, D_DH = 8, 128
LANES = 128
SUBLANES = 8
B_QK = B_NOPE + LANES

VMEM_LIMIT_BYTES = 56 * 1024 * 1024

RET_CHUNK = 128


VMEM_LIMIT_BYTES_FFN_DOWN = 62 * 1024 * 1024


def _cparams(n_axes, vmem_limit_bytes=VMEM_LIMIT_BYTES):
    return pltpu.CompilerParams(dimension_semantics=("arbitrary",) * n_axes,
                                vmem_limit_bytes=vmem_limit_bytes)


def _dot(a, b):
    return jnp.dot(a, b, preferred_element_type=F32)


def _dot_nt(a, b):
    return lax.dot_general(a, b, (((1,), (1,)), ((), ())), preferred_element_type=F32)


def _dot_tn(a, b):
    return lax.dot_general(a, b, (((0,), (0,)), ((), ())), preferred_element_type=F32)


def _split_bf16(a):
    hi = a.astype(BF16)
    lo = (a - hi.astype(F32)).astype(BF16)
    return hi, lo


def _dot3(a, b):
    ah, al = _split_bf16(a)
    bh, bl = _split_bf16(b)
    return _dot(ah, bh) + _dot(ah, bl) + _dot(al, bh)


def _sigmoid(x):
    return 1.0 / (1.0 + jnp.exp(-x))


def _silu(x):
    return x * _sigmoid(x)


def _softplus(x):
    return jnp.maximum(x, 0.0) + jnp.log(1.0 + jnp.exp(-jnp.abs(x)))


def _rms(x, w):
    return x * lax.rsqrt(jnp.mean(x * x, axis=-1, keepdims=True) + EPS) * w


def _ada_kernel(c_ref, w_ref, b_ref, o_ref):
    cond = _silu(c_ref[...]).astype(BF16)
    o_ref[...] = _dot(cond, w_ref[...].astype(BF16)) + b_ref[...]


def _ada(c_pad, ada_w, ada_b):
    depth, d, n = ada_w.shape
    rows = c_pad.shape[0]
    tn = 1024
    return pl.pallas_call(
        _ada_kernel,
        grid=(depth, n // tn),
        in_specs=[pl.BlockSpec((rows, d), lambda l, j: (0, 0)),
                  pl.BlockSpec((None, d, tn), lambda l, j: (l, 0, j)),
                  pl.BlockSpec((None, 1, tn), lambda l, j: (l, 0, j))],
        out_specs=pl.BlockSpec((None, rows, tn), lambda l, j: (l, 0, j)),
        out_shape=jax.ShapeDtypeStruct((depth, rows, n), F32),
        compiler_params=_cparams(2),
        name="ada_mod",
    )(c_pad, ada_w, ada_b.reshape(depth, 1, n))


def _rope_kernel(pos_ref, f_ref, sb_ref, sc_ref, cb_ref, snb_ref, cc_ref, snc_ref):
    ang = pos_ref[...] * f_ref[...]
    cos = jnp.cos(ang)
    sin = jnp.sin(ang)
    lane = lax.broadcasted_iota(jnp.int32, ang.shape, 1)
    nb, nc = B_ROPE // 2, C_DK // 2
    in_b = (lane < nb) | ((lane >= 2 * nb) & (lane < 3 * nb))

    def lay_b(x):
        return jnp.where(in_b, jnp.where(lane < nb, x, pltpu.roll(x, 2 * nb, 1)), 0.0)

    def lay_c(x):
        return jnp.where(lane < nc, pltpu.roll(x, LANES - nb, 1), pltpu.roll(x, nc - nb, 1))

    cb_ref[...] = lay_b(cos)
    snb_ref[...] = lay_b(sin) * sb_ref[...]
    cc_ref[...] = lay_c(cos)
    snc_ref[...] = lay_c(sin) * sc_ref[...]


def _rope_tables(pos_col):
    t = pos_col.shape[0]
    tb = min(1024, t)
    inv_b = ROPE_BASE ** (-jnp.arange(0, B_ROPE, 2, dtype=F32) / B_ROPE)
    inv_c = ROPE_BASE ** (-jnp.arange(0, C_DK, 2, dtype=F32) / C_DK)
    z32 = jnp.zeros((B_ROPE // 2,), F32)
    o32 = jnp.ones((B_ROPE // 2,), F32)
    o64 = jnp.ones((C_DK // 2,), F32)
    freqs = jnp.concatenate([inv_b, inv_c, z32])[None, :]
    sb = jnp.concatenate([-o32, z32, o32, z32])[None, :]
    sc = jnp.concatenate([-o64, o64])[None, :]
    row = pl.BlockSpec((1, LANES), lambda i: (0, 0))
    tab = pl.BlockSpec((tb, LANES), lambda i: (i, 0))
    shp = jax.ShapeDtypeStruct((t, LANES), F32)
    return pl.pallas_call(
        _rope_kernel,
        grid=(t // tb,),
        in_specs=[pl.BlockSpec((tb, 1), lambda i: (i, 0)), row, row, row],
        out_specs=[tab, tab, tab, tab],
        out_shape=[shp, shp, shp, shp],
        compiler_params=_cparams(1),
        name="rope_tables",
    )(pos_col, freqs, sb, sc)


def _prenorm_kernel(x_ref, mod_ref, w_ref, h_ref, *, shift_idx, scale_idx):
    y = _rms(x_ref[...], w_ref[...])
    h = y * (1.0 + mod_ref[scale_idx:scale_idx + 1, :]) + mod_ref[shift_idx:shift_idx + 1, :]
    h_ref[...] = h.astype(h_ref.dtype)


def _prenorm(x2, mod_l, w, seq, shift_idx, scale_idx):
    t, d = x2.shape
    tm = min(1024, seq)
    bpb = seq // tm
    return pl.pallas_call(
        functools.partial(_prenorm_kernel, shift_idx=shift_idx, scale_idx=scale_idx),
        grid=(t // tm,),
        in_specs=[pl.BlockSpec((tm, d), lambda i: (i, 0)),
                  pl.BlockSpec((None, 6, d), lambda i: (i // bpb, 0, 0)),
                  pl.BlockSpec((1, d), lambda i: (0, 0))],
        out_specs=pl.BlockSpec((tm, d), lambda i: (i, 0)),
        out_shape=jax.ShapeDtypeStruct((t, d), BF16),
        compiler_params=_cparams(1),
        name="prenorm",
    )(x2, mod_l, w.reshape(1, d))


def _mm_kernel(*refs, cast_w, scaled):
    a_ref, w_ref = refs[0], refs[1]
    s_ref = refs[2] if scaled else None
    o_ref = refs[3] if scaled else refs[2]
    if cast_w:
        wb_ref = refs[-1]

        @pl.when(pl.program_id(1) == 0)
        def _():
            wb_ref[...] = w_ref[...].astype(BF16)

        w = wb_ref[...]
    else:
        w = w_ref[...]
    y = _dot(a_ref[...], w)
    if scaled:
        y = y * s_ref[...]
    o_ref[...] = y.astype(o_ref.dtype)


def _mm(a, w, out_dtype, layer=None, col0=0, ncols=None, colscale=None, tm=1024, tn=1024):
    m, k = a.shape
    n = w.shape[-1] if ncols is None else ncols
    tm = min(tm, m)
    tn = min(tn, n)
    jb0 = col0 // tn
    cast_w = w.dtype != BF16
    if layer is None:
        wspec = pl.BlockSpec((k, tn), lambda j, i: (0, jb0 + j))
    else:
        wspec = pl.BlockSpec((None, k, tn), lambda j, i: (layer, 0, jb0 + j))
    in_specs = [pl.BlockSpec((tm, k), lambda j, i: (i, 0)), wspec]
    args = [a, w]
    if colscale is not None:
        in_specs.append(pl.BlockSpec((1, tn), lambda j, i: (0, j)))
        args.append(colscale.reshape(1, n).astype(F32))
    return pl.pallas_call(
        functools.partial(_mm_kernel, cast_w=cast_w, scaled=colscale is not None),
        grid=(n // tn, m // tm),
        in_specs=in_specs,
        out_specs=pl.BlockSpec((tm, tn), lambda j, i: (i, j)),
        out_shape=jax.ShapeDtypeStruct((m, n), out_dtype),
        scratch_shapes=[pltpu.VMEM((k, tn), BF16)] if cast_w else [],
        compiler_params=_cparams(2),
        name="matmul",
    )(*args)


MM_POST_SPLIT = 2


def _mm_post_kernel(*refs, n_in, cast_w, gate_idx, nshift_idx, nscale_idx, emit_h):
    a_refs = refs[:n_in]
    w_refs = refs[n_in:2 * n_in]
    x_ref, mod_ref, modn_ref, pw_ref, nw_ref = refs[2 * n_in:2 * n_in + 5]
    n_out = 2 if emit_h else 1
    outs = refs[2 * n_in + 5:2 * n_in + 5 + n_out]
    if cast_w:
        wb_refs = refs[2 * n_in + 5 + n_out:]

        @pl.when(pl.program_id(0) == 0)
        def _():
            for w_ref, wb_ref in zip(w_refs, wb_refs):
                wb_ref[...] = w_ref[...].astype(BF16)

        w_refs = wb_refs
    tm = x_ref.shape[0]
    for r0 in range(0, tm, tm // MM_POST_SPLIT):
        rows = slice(r0, r0 + tm // MM_POST_SPLIT)
        y = _dot(a_refs[0][rows, :], w_refs[0][...])
        for a_ref, w_ref in zip(a_refs[1:], w_refs[1:]):
            y = y + _dot(a_ref[rows, :], w_ref[...])
        xn = x_ref[rows, :] + mod_ref[gate_idx:gate_idx + 1, :] * _rms(y, pw_ref[...])
        outs[0][rows, :] = xn
        if emit_h:
            hn = _rms(xn, nw_ref[...])
            hn = hn * (1.0 + modn_ref[nscale_idx:nscale_idx + 1, :]) + modn_ref[nshift_idx:nshift_idx + 1, :]
            outs[1][rows, :] = hn.astype(outs[1].dtype)


def _mm_post(a_list, w, w_layer, x2, mod_l, modn_l, post_w, next_w, seq, gate_idx, nshift_idx, nscale_idx, emit_h,
             tm, vmem_limit_bytes=VMEM_LIMIT_BYTES):
    m = a_list[0].shape[0]
    d = w.shape[-1]
    ka = a_list[0].shape[1]
    tm = min(tm, seq)
    bpb = seq // tm
    cast_w = w.dtype != BF16
    row = pl.BlockSpec((1, d), lambda i: (0, 0))
    modspec = pl.BlockSpec((None, 6, d), lambda i: (i // bpb, 0, 0))
    xspec = pl.BlockSpec((tm, d), lambda i: (i, 0))
    a_specs = [pl.BlockSpec((tm, ka), lambda i: (i, 0)) for _ in a_list]
    w_specs = [pl.BlockSpec((None, ka, d), lambda i, r=r: (w_layer, r, 0), pipeline_mode=pl.Buffered(1))
               for r in range(len(a_list))]
    out_specs = [xspec]
    out_shape = [jax.ShapeDtypeStruct((m, d), F32)]
    if emit_h:
        out_specs.append(xspec)
        out_shape.append(jax.ShapeDtypeStruct((m, d), BF16))
    res = pl.pallas_call(
        functools.partial(_mm_post_kernel, n_in=len(a_list), cast_w=cast_w, gate_idx=gate_idx,
                          nshift_idx=nshift_idx, nscale_idx=nscale_idx, emit_h=emit_h),
        grid=(m // tm,),
        in_specs=a_specs + w_specs + [xspec, modspec, modspec, row, row],
        out_specs=out_specs,
        out_shape=out_shape,
        scratch_shapes=[pltpu.VMEM((ka, d), BF16) for _ in a_list] if cast_w else [],
        compiler_params=_cparams(1, vmem_limit_bytes),
        name="matmul_post",
    )(*a_list, *([w] * len(a_list)), x2, mod_l, modn_l, post_w.reshape(1, d), next_w.reshape(1, d))
    return res if emit_h else (res[0], None)


def _ffn_up_kernel(a_ref, wg_ref, wu_ref, o_ref, wgb_ref, wub_ref):
    @pl.when(pl.program_id(1) == 0)
    def _():
        wgb_ref[...] = wg_ref[...].astype(BF16)
        wub_ref[...] = wu_ref[...].astype(BF16)

    a = a_ref[...]
    g = _dot(a, wgb_ref[...])
    u = _dot(a, wub_ref[...])
    o_ref[...] = (_silu(g) * u).astype(o_ref.dtype)


def _ffn_up(a, wg, wu, layer, tm=1024, tn=512):
    m, k = a.shape
    n = wg.shape[-1]
    tm = min(tm, m)
    wspec = pl.BlockSpec((None, k, tn), lambda j, i: (layer, 0, j))
    return pl.pallas_call(
        _ffn_up_kernel,
        grid=(n // tn, m // tm),
        in_specs=[pl.BlockSpec((tm, k), lambda j, i: (i, 0)), wspec, wspec],
        out_specs=pl.BlockSpec((tm, tn), lambda j, i: (i, j)),
        out_shape=jax.ShapeDtypeStruct((m, n), BF16),
        scratch_shapes=[pltpu.VMEM((k, tn), BF16), pltpu.VMEM((k, tn), BF16)],
        compiler_params=_cparams(2),
        name="ffn_up",
    )(a, wg, wu)


def _group_slabs(x, o_ref, heads_per_group):
    for g in range(o_ref.shape[0]):
        shift = (LANES - g * heads_per_group) % LANES
        o_ref[g] = x if shift == 0 else pltpu.roll(x, shift, 1)


def _gdn_gate_kernel(s_ref, alog_ref, dt_ref, o_ref, *, hg):
    s = s_ref[...]
    g = -jnp.exp(alog_ref[...]) * _softplus(s + dt_ref[...])
    beta = _sigmoid(s)
    lane = lax.broadcasted_iota(jnp.int32, s.shape, 1)
    _group_slabs(jnp.where(lane < A_HEADS, g, beta), o_ref, hg)


def _gdn_gates(slab_src, col_block, a_log, dt_bias, hg):
    t = slab_src.shape[0]
    tb = min(1024, t)
    ng = A_HEADS // hg
    pad = jnp.zeros((LANES - A_HEADS,), F32)
    alog_row = jnp.concatenate([a_log.astype(F32), pad])[None, :]
    dt_row = jnp.concatenate([dt_bias.astype(F32), pad])[None, :]
    row = pl.BlockSpec((1, LANES), lambda i: (0, 0))
    return pl.pallas_call(
        functools.partial(_gdn_gate_kernel, hg=hg),
        grid=(t // tb,),
        in_specs=[pl.BlockSpec((tb, LANES), lambda i: (i, col_block)), row, row],
        out_specs=pl.BlockSpec((ng, tb, LANES), lambda i: (0, i, 0)),
        out_shape=jax.ShapeDtypeStruct((ng, t, LANES), F32),
        compiler_params=_cparams(1),
        name="gdn_gates",
    )(slab_src, alog_row, dt_row)


GDN_HG = 4
GDN_TB = 512


def _unit_lower_inverses(n_mats):
    c = n_mats[0].shape[0]
    r = lax.broadcasted_iota(jnp.int32, (c, c), 0)
    col = lax.broadcasted_iota(jnp.int32, (c, c), 1)
    eye = jnp.where(r == col, 1.0, 0.0)

    def dot1(a, b):
        return _dot(a.astype(BF16), b.astype(BF16))

    ps = [eye - n for n in n_mats]
    ms = [dot1(n, n) for n in n_mats]
    power = 2
    while True:
        ps = [p + dot1(p, m) for p, m in zip(ps, ms)]
        power *= 2
        if power >= c:
            break
        ms = [dot1(m, m) for m in ms]
    res = [eye - p - _dot3(n, p) for n, p in zip(n_mats, ps)]
    return [p + dot1(p, e) for p, e in zip(ps, res)]


def _gdn_kernel(q_ref, k_ref, v_ref, z_ref, cwq_ref, cwk_ref, cwv_ref, gates_ref, grow_ref, onw_ref,
                o_ref, xq_ref, xk_ref, xv_ref, st_ref, *, tb, hg):
    j = pl.program_id(2)

    @pl.when(j == 0)
    def _():
        st_ref[...] = jnp.zeros_like(st_ref)
        for h_ref in (xq_ref, xk_ref, xv_ref):
            h_ref[...] = jnp.zeros_like(h_ref)

    def conv_silu(src_ref, hist_ref, cw_ref):
        x = src_ref[...]
        prev = hist_ref[...]
        hrows = prev.shape[0]
        row = lax.broadcasted_iota(jnp.int32, prev.shape, 0)
        acc = x * cw_ref[A_CONV - 1:A_CONV, :]
        for s in range(1, A_CONV):
            xs = pltpu.roll(x, s, 0)
            top = jnp.where(row < s, pltpu.roll(prev, s, 0), xs[0:hrows])
            shifted = jnp.concatenate([top, xs[hrows:]], axis=0)
            acc = acc + shifted * cw_ref[A_CONV - 1 - s:A_CONV - s, :]
        hist_ref[...] = x[tb - hrows:tb]
        return _silu(acc)

    q_all = conv_silu(q_ref, xq_ref, cwq_ref)
    k_all = conv_silu(k_ref, xk_ref, cwk_ref)
    v_all = conv_silu(v_ref, xv_ref, cwv_ref)

    c = CHUNK
    r = lax.broadcasted_iota(jnp.int32, (c, c), 0)
    col = lax.broadcasted_iota(jnp.int32, (c, c), 1)
    tri = r >= col
    strict = r > col

    nch = tb // c
    items = [(hh, ci) for hh in range(hg) for ci in range(nch)]
    qn, kn, vn = [], [], []
    for hh in range(hg):
        lanes = slice(hh * A_DK, (hh + 1) * A_DK)
        qh = q_all[:, lanes]
        kh = k_all[:, lanes]
        qn.append(qh * lax.rsqrt(jnp.sum(qh * qh, axis=-1, keepdims=True) + EPS) * (A_DK ** -0.5))
        kn.append(kh * lax.rsqrt(jnp.sum(kh * kh, axis=-1, keepdims=True) + EPS))
        vn.append(v_all[:, lanes])

    def rows_of(ci):
        return slice(ci * c, (ci + 1) * c)

    qc = [qn[hh][rows_of(ci)] for hh, ci in items]
    kc = [kn[hh][rows_of(ci)] for hh, ci in items]
    vc = [vn[hh][rows_of(ci)] for hh, ci in items]
    b_col = [gates_ref[rows_of(ci), A_HEADS + hh:A_HEADS + hh + 1] for hh, ci in items]
    gc_col = [jnp.sum(jnp.where(tri, grow_ref[hh][:, rows_of(ci)], 0.0), axis=1, keepdims=True)
              for hh, ci in items]
    gc_row = [jnp.sum(jnp.where(r <= col, gates_ref[rows_of(ci), hh:hh + 1], 0.0), axis=0, keepdims=True)
              for hh, ci in items]
    decay = [jnp.where(tri, jnp.exp(jnp.minimum(a - b, 0.0)), 0.0) for a, b in zip(gc_col, gc_row)]
    e_gc = [jnp.exp(a) for a in gc_col]
    g_last = [a[c - 1:c, :] for a in gc_col]
    kb = [x.astype(BF16) for x in kc]
    kk = [_dot_nt(x, x) for x in kb]
    qk = [_dot_nt(x.astype(BF16), y) for x, y in zip(qc, kb)]
    n_mat = [jnp.where(strict, b * x * d, 0.0) for b, x, d in zip(b_col, kk, decay)]
    qk = [jnp.where(tri, x * d, 0.0).astype(BF16) for x, d in zip(qk, decay)]
    t_inv = _unit_lower_inverses(n_mat)
    wu = [_dot3(t, jnp.concatenate([k * (b * e), v * b], axis=1)).astype(BF16)
          for t, k, v, b, e in zip(t_inv, kc, vc, b_col, e_gc)]
    kdb = [(k * jnp.exp(gl - g)).astype(BF16) for k, gl, g in zip(kc, g_last, gc_col)]
    mb = [_dot_tn(kd, x) for kd, x in zip(kdb, wu)]
    m_c = [x[:, :A_DK].astype(BF16) for x in mb]
    b_c = [x[:, A_DK:] for x in mb]
    qwu = [_dot(a, x) for a, x in zip(qk, wu)]
    q_eff = [(q * e - x[:, :A_DK]).astype(BF16) for q, e, x in zip(qc, e_gc, qwu)]
    qku = [x[:, A_DK:] for x in qwu]
    e_gl = [jnp.exp(x) for x in g_last]
    states = [st_ref[hh] for hh in range(hg)]
    o_c = [None] * len(items)
    for ci in range(nch):
        idx = [hh * nch + ci for hh in range(hg)]
        sb = [s.astype(BF16) for s in states]
        for hh in range(hg):
            o_c[idx[hh]] = _dot(q_eff[idx[hh]], sb[hh]) + qku[idx[hh]]
        states = [states[hh] * e_gl[idx[hh]] + b_c[idx[hh]] - _dot(m_c[idx[hh]], sb[hh]) for hh in range(hg)]
    for hh in range(hg):
        st_ref[hh] = states[hh]
    for i, (hh, ci) in enumerate(items):
        lanes = slice(hh * A_DK, (hh + 1) * A_DK)
        zc = z_ref[rows_of(ci), lanes]
        o_ref[rows_of(ci), lanes] = (_rms(o_c[i], onw_ref[...]) * _silu(zc)).astype(o_ref.dtype)


def _gdn(proj, conv_w, gates, g_row, out_norm, batch, seq):
    t = proj.shape[0]
    tb = min(GDN_TB, seq)
    hg = GDN_HG
    w = hg * A_DK
    nj = seq // tb
    ng = A_HEADS // hg
    kq = A_HEADS * A_DK
    off_k = kq // w
    off_v = 2 * kq // w
    off_z = 3 * kq // w

    def colspec(off):
        return pl.BlockSpec((tb, w), lambda b, g, j: (b * nj + j, off + g))

    def cwspec(off):
        return pl.BlockSpec((A_CONV, w), lambda b, g, j: (0, off + g))

    colv = pl.BlockSpec((None, tb, LANES), lambda b, g, j: (g, b * nj + j, 0))
    rowv = pl.BlockSpec((hg, 1, tb), lambda b, g, j: (g, 0, b * nj + j))
    return pl.pallas_call(
        functools.partial(_gdn_kernel, tb=tb, hg=hg),
        grid=(batch, ng, nj),
        in_specs=[colspec(0), colspec(off_k), colspec(off_v), colspec(off_z),
                  cwspec(0), cwspec(off_k), cwspec(off_v), colv, rowv,
                  pl.BlockSpec((1, A_DV), lambda b, g, j: (0, 0))],
        out_specs=pl.BlockSpec((tb, w), lambda b, g, j: (b * nj + j, g)),
        out_shape=jax.ShapeDtypeStruct((t, A_HEADS * A_DV), BF16),
        scratch_shapes=[pltpu.VMEM((SUBLANES, w), F32), pltpu.VMEM((SUBLANES, w), F32),
                        pltpu.VMEM((SUBLANES, w), F32), pltpu.VMEM((hg, A_DK, A_DV), F32)],
        compiler_params=_cparams(3),
        name="gated_deltanet",
    )(proj, proj, proj, proj, conv_w, conv_w, conv_w, gates, g_row, out_norm.reshape(1, A_DV))


def _rope_lanes(x, cos, sin_signed):
    return x * cos + pltpu.roll(x, LANES // 2, 1) * sin_signed


def _mla_q_kernel(cq_ref, nw_ref, w_ref, cos_ref, sin_ref, o_ref):
    n = _rms(cq_ref[...], nw_ref[...]).astype(BF16)
    cos = cos_ref[...]
    sin = sin_ref[...]
    for h in range(B_HEADS):
        qh = _dot(n, w_ref[:, h * B_QK:(h + 1) * B_QK])
        o_ref[:, h * B_QK:h * B_QK + B_NOPE] = qh[:, :B_NOPE].astype(o_ref.dtype)
        o_ref[:, h * B_QK + B_NOPE:(h + 1) * B_QK] = _rope_lanes(qh[:, B_NOPE:], cos, sin).astype(o_ref.dtype)


def _mla_q(proj, col_block, q_norm, w_uq_p, cos_b, sin_b):
    t = proj.shape[0]
    tm = min(1024, t)
    n = w_uq_p.shape[1]
    return pl.pallas_call(
        _mla_q_kernel,
        grid=(t // tm,),
        in_specs=[pl.BlockSpec((tm, B_Q_LORA), lambda i: (i, col_block)),
                  pl.BlockSpec((1, B_Q_LORA), lambda i: (0, 0)),
                  pl.BlockSpec((B_Q_LORA, n), lambda i: (0, 0)),
                  pl.BlockSpec((tm, LANES), lambda i: (i, 0)),
                  pl.BlockSpec((tm, LANES), lambda i: (i, 0))],
        out_specs=pl.BlockSpec((tm, n), lambda i: (i, 0)),
        out_shape=jax.ShapeDtypeStruct((t, n), BF16),
        compiler_params=_cparams(1),
        name="mla_q_up",
    )(proj, q_norm.reshape(1, B_Q_LORA), w_uq_p, cos_b, sin_b)


def _mla_kv_kernel(ckv_ref, nw_ref, w_ref, kpe_ref, cos_ref, sin_ref, k_ref, v_ref):
    n = _rms(ckv_ref[...], nw_ref[...]).astype(BF16)
    kpe = _rope_lanes(kpe_ref[...], cos_ref[...], sin_ref[...]).astype(k_ref.dtype)
    nk = B_HEADS * B_NOPE
    kn = _dot(n, w_ref[:, :nk])
    for h in range(B_HEADS):
        k_ref[:, h * B_QK:h * B_QK + B_NOPE] = kn[:, h * B_NOPE:(h + 1) * B_NOPE].astype(k_ref.dtype)
        k_ref[:, h * B_QK + B_NOPE:(h + 1) * B_QK] = kpe
    v_ref[...] = _dot(n, w_ref[:, nk:]).astype(v_ref.dtype)


def _mla_kv(proj, col_block, kv_norm, w_ukv_p, kpe_block, cos_b, sin_b):
    t = proj.shape[0]
    tm = min(1024, t)
    n = w_ukv_p.shape[1]
    tab = pl.BlockSpec((tm, LANES), lambda i: (i, 0))
    kpe_spec = pl.BlockSpec((tm, LANES), lambda i: (i, kpe_block))
    return pl.pallas_call(
        _mla_kv_kernel,
        grid=(t // tm,),
        in_specs=[pl.BlockSpec((tm, B_KV_LORA), lambda i: (i, col_block)),
                  pl.BlockSpec((1, B_KV_LORA), lambda i: (0, 0)),
                  pl.BlockSpec((B_KV_LORA, n), lambda i: (0, 0)),
                  kpe_spec, tab, tab],
        out_specs=[pl.BlockSpec((tm, B_HEADS * B_QK), lambda i: (i, 0)),
                   pl.BlockSpec((tm, B_HEADS * B_DV), lambda i: (i, 0))],
        out_shape=[jax.ShapeDtypeStruct((t, B_HEADS * B_QK), BF16),
                   jax.ShapeDtypeStruct((t, B_HEADS * B_DV), BF16)],
        compiler_params=_cparams(1),
        name="mla_kv_up",
    )(proj, kv_norm.reshape(1, B_KV_LORA), w_ukv_p, proj, cos_b, sin_b)


ATTN_T = 512


def _attn_kernel(*refs, tq, nq, dk, dv, hg, mask_gran, has_bias):
    if has_bias:
        q_ref, k_ref, v_ref, cq_ref, ck_ref, o_ref, m_ref, acc_ref, vt_ref = refs
    else:
        q_ref, k_ref, v_ref, o_ref, m_ref, acc_ref, vt_ref = refs
    qi = pl.program_id(2)

    @pl.when(qi == 0)
    def _():
        for hh in range(hg):
            for jb in range(nq):
                vblk = v_ref[jb * tq:(jb + 1) * tq, hh * dv:(hh + 1) * dv]
                vt_ref[hh, jb, 0:dv, :] = vblk.astype(F32).T.astype(BF16)
                vt_ref[hh, jb, dv:, :] = jnp.ones((ATTN_ONES, tq), BF16)

    m_ref[...] = jnp.full(m_ref.shape, NEG_INF, F32)
    acc_ref[...] = jnp.zeros_like(acc_ref)
    heads = range(hg)

    def step(jb, masked):
        start = pl.multiple_of(jb * tq, tq)
        s = [_dot_nt(k_ref[pl.ds(start, tq), hh * dk:(hh + 1) * dk], q_ref[:, hh * dk:(hh + 1) * dk])
             for hh in heads]
        if has_bias:
            s = [s[hh] - ck_ref[pl.ds(start, tq), hh:hh + 1] for hh in heads]
        if masked:
            r = lax.broadcasted_iota(jnp.int32, (tq, tq), 0)
            c = lax.broadcasted_iota(jnp.int32, (tq, tq), 1)
            if mask_gran > 1:
                shift = int(np.log2(mask_gran))
                r = lax.shift_right_logical(r, shift)
                c = lax.shift_right_logical(c, shift)
            keep = r <= c
            s = [jnp.where(keep, s[hh], NEG_INF) for hh in heads]
        m_prev = [m_ref[hh] for hh in heads]
        m_cur = [jnp.max(s[hh], axis=0, keepdims=True) for hh in heads]
        if has_bias:
            m_cur = [m_cur[hh] + cq_ref[hh] for hh in heads]
        m_new = [jnp.maximum(m_prev[hh], m_cur[hh]) for hh in heads]
        sub = [m_new[hh] - cq_ref[hh] for hh in heads] if has_bias else m_new
        p = [jnp.exp2(s[hh] - sub[hh]).astype(BF16) for hh in heads]
        alpha = [jnp.exp2(m_prev[hh] - m_new[hh]) for hh in heads]
        for hh in heads:
            m_ref[hh] = m_new[hh]
        pv = [_dot(vt_ref[hh, jb], p[hh]) for hh in heads]
        for hh in heads:
            acc_ref[hh] = alpha[hh] * acc_ref[hh] + pv[hh]

    def body(jb, carry):
        step(jb, False)
        return carry

    lax.fori_loop(0, qi, body, 0)
    step(qi, True)
    for hh in heads:
        acc = acc_ref[hh]
        o_ref[:, hh * dv:(hh + 1) * dv] = (acc[:dv] / acc[dv:dv + 1]).T.astype(o_ref.dtype)


ATTN_HG = 4
ATTN_ONES = 16
LOG2E = 1.4426950408889634


def _attention(q, k, v, dk, dv, heads, batch, seq, mask_gran, q_off=0, k_off=0, v_off=0, bias=None):
    t = q.shape[0]
    tq = min(ATTN_T, seq)
    nq = seq // tq
    hg = ATTN_HG
    qb, kb, vb = q_off // (hg * dk), k_off // (hg * dk), v_off // (hg * dv)
    dvx = dv + ATTN_ONES
    in_specs = [pl.BlockSpec((tq, hg * dk), lambda b, g, i: (b * nq + i, qb + g)),
                pl.BlockSpec((seq, hg * dk), lambda b, g, i: (b, kb + g)),
                pl.BlockSpec((seq, hg * dv), lambda b, g, i: (b, vb + g))]
    args = [q, k, v]
    if bias is not None:
        cq, ck = bias
        in_specs += [pl.BlockSpec((None, hg, 1, tq), lambda b, g, i: (b, g, 0, i)),
                     pl.BlockSpec((None, seq, LANES), lambda b, g, i: (g, b, 0))]
        args += [cq, ck]
    return pl.pallas_call(
        functools.partial(_attn_kernel, tq=tq, nq=nq, dk=dk, dv=dv, hg=hg, mask_gran=mask_gran,
                          has_bias=bias is not None),
        grid=(batch, heads // hg, nq),
        in_specs=in_specs,
        out_specs=pl.BlockSpec((tq, hg * dv), lambda b, g, i: (b * nq + i, g)),
        out_shape=jax.ShapeDtypeStruct((t, heads * dv), BF16),
        scratch_shapes=[pltpu.VMEM((hg, 1, tq), F32), pltpu.VMEM((hg, dvx, tq), F32),
                        pltpu.VMEM((hg, nq, dvx, tq), BF16)],
        compiler_params=_cparams(3),
        name="flash_attention",
    )(*args)


RET_HG = 4
RET_TB = 512


def _ret_kernel(q_ref, k_ref, v_ref, g_ref, cos_ref, sin_ref, dec_ref, xi_ref, zeta_ref, gch_ref, gnw_ref,
                o_ref, st_ref, *, tb, hg, c):
    j = pl.program_id(2)

    @pl.when(j == 0)
    def _():
        st_ref[...] = jnp.zeros_like(st_ref)

    cos = cos_ref[...]
    sin = sin_ref[...]
    nch = tb // c
    items = [(hh, ci) for hh in range(hg) for ci in range(nch)]

    def lanes_of(hh):
        return slice(hh * C_DK, (hh + 1) * C_DK)

    def rows_of(ci):
        return slice(ci * c, (ci + 1) * c)

    qh = [_rope_lanes(q_ref[:, lanes_of(hh)], cos, sin) for hh in range(hg)]
    kh = [_rope_lanes(k_ref[:, lanes_of(hh)], cos, sin) for hh in range(hg)]
    qb = [qh[hh][rows_of(ci)].astype(BF16) for hh, ci in items]
    kc = [kh[hh][rows_of(ci)] for hh, ci in items]
    vb = [v_ref[rows_of(ci), lanes_of(hh)].astype(BF16) for hh, ci in items]
    scores = [(_dot_nt(q, k.astype(BF16)) * dec_ref[hh]).astype(BF16) for q, k, (hh, _) in zip(qb, kc, items)]
    inner = [_dot(s, v) for s, v in zip(scores, vb)]
    kzv = [_dot_tn((k * zeta_ref[hh]).astype(BF16), v) for k, v, (hh, _) in zip(kc, vb, items)]
    states = []
    for hh in range(hg):
        state = st_ref[hh]
        for ci in range(nch):
            states.append(state)
            state = state * gch_ref[hh] + kzv[hh * nch + ci]
        st_ref[hh] = state
    cross = [_dot(q, s.astype(BF16)) * xi_ref[hh] for q, s, (hh, _) in zip(qb, states, items)]
    for i, (hh, ci) in enumerate(items):
        o_c = inner[i] + cross[i]
        gate = _silu(g_ref[rows_of(ci), lanes_of(hh)])
        o_ref[rows_of(ci), lanes_of(hh)] = (_rms(o_c, gnw_ref[hh]) * gate).astype(o_ref.dtype)


def _retention(proj, cos_c, sin_c, group_norm, batch, seq):
    t = proj.shape[0]
    tb = min(RET_TB, seq)
    c = min(RET_CHUNK, seq)
    hg = RET_HG
    w = hg * C_DK
    nj = seq // tb
    ng = C_HEADS // hg
    nh = C_HEADS * C_DK // w

    log_gamma = jnp.log1p(-jnp.exp2(-5.0 - jnp.arange(C_HEADS, dtype=F32)))
    idx = jnp.arange(c, dtype=F32)
    rel = idx[:, None] - idx[None, :]
    dec = jnp.where(rel >= 0, jnp.exp(log_gamma[:, None, None] * jnp.maximum(rel, 0.0)), 0.0)
    xi = jnp.exp(log_gamma[:, None] * (idx + 1.0))
    zeta = jnp.exp(log_gamma[:, None] * (c - 1.0 - idx))
    gch = jnp.exp(log_gamma * c)
    xi_b = jnp.broadcast_to(xi[:, :, None], (C_HEADS, c, C_DV))
    zeta_b = jnp.broadcast_to(zeta[:, :, None], (C_HEADS, c, C_DK))
    gch_b = jnp.broadcast_to(gch[:, None, None], (C_HEADS, 1, C_DV))

    def colspec(off):
        return pl.BlockSpec((tb, w), lambda b, g, j: (b * nj + j, off + g))

    tab = pl.BlockSpec((tb, LANES), lambda b, g, j: (b * nj + j, 0))

    def hspec(r, cdim):
        return pl.BlockSpec((hg, r, cdim), lambda b, g, j: (g, 0, 0))

    return pl.pallas_call(
        functools.partial(_ret_kernel, tb=tb, hg=hg, c=c),
        grid=(batch, ng, nj),
        in_specs=[colspec(0), colspec(nh), colspec(2 * nh), colspec(3 * nh), tab, tab,
                  hspec(c, c), hspec(c, C_DV), hspec(c, C_DK), hspec(1, C_DV), hspec(1, C_DV)],
        out_specs=pl.BlockSpec((tb, w), lambda b, g, j: (b * nj + j, g)),
        out_shape=jax.ShapeDtypeStruct((t, C_HEADS * C_DV), BF16),
        scratch_shapes=[pltpu.VMEM((hg, C_DK, C_DV), F32)],
        compiler_params=_cparams(3),
        name="retention",
    )(proj, proj, proj, proj, cos_c, sin_c, dec, xi_b, zeta_b, gch_b, group_norm.reshape(C_HEADS, 1, C_DV))


def _fox_cum_kernel(s_ref, b_ref, o_ref, *, seq, hg):
    x = s_ref[...] + b_ref[...]
    acc = -_softplus(-x)
    row = lax.broadcasted_iota(jnp.int32, acc.shape, 0)
    shift = 1
    while shift < seq:
        acc = acc + jnp.where(row >= shift, pltpu.roll(acc, shift, 0), 0.0)
        shift *= 2
    _group_slabs(acc * LOG2E, o_ref, hg)


def _fox_cum(slab, forget_bias, batch, seq, hg):
    t = slab.shape[0]
    ng = D_HEADS // hg
    pad = jnp.zeros((LANES - D_HEADS,), F32)
    b_row = jnp.concatenate([forget_bias.astype(F32), pad])[None, :]
    return pl.pallas_call(
        functools.partial(_fox_cum_kernel, seq=seq, hg=hg),
        grid=(batch,),
        in_specs=[pl.BlockSpec((seq, LANES), lambda b: (b, 0)),
                  pl.BlockSpec((1, LANES), lambda b: (0, 0))],
        out_specs=pl.BlockSpec((ng, seq, LANES), lambda b: (0, b, 0)),
        out_shape=jax.ShapeDtypeStruct((ng, t, LANES), F32),
        compiler_params=_cparams(1),
        name="fox_cumsum",
    )(slab, b_row)


def _pe_lanes(w_pe):
    k = w_pe.shape[0]
    z = jnp.zeros((k, B_ROPE // 2), w_pe.dtype)
    return jnp.concatenate([w_pe[:, :B_ROPE // 2], z, w_pe[:, B_ROPE // 2:], z], axis=1)


AB_MAIN = A_HEADS * (2 * A_DK + A_DV) + A_HEADS * A_DV


def _ab_weights(w_in, w_uq, w_ukv):
    o = AB_MAIN
    w_a = w_in[:, o:o + A_HEADS]; o += A_HEADS
    w_b = w_in[:, o:o + A_HEADS]; o += A_HEADS
    w_cq = w_in[:, o:o + B_Q_LORA]; o += B_Q_LORA
    w_ckv = w_in[:, o:o + B_KV_LORA]; o += B_KV_LORA
    w_kpe = w_in[:, o:o + B_ROPE]
    d = w_in.shape[0]
    w_tail = jnp.concatenate([w_cq, w_ckv, _pe_lanes(w_kpe), w_a, w_b,
                              jnp.zeros((d, LANES - 2 * A_HEADS), w_in.dtype)], axis=1)
    scale = (B_NOPE + B_ROPE) ** -0.5 * LOG2E
    wq = (w_uq * scale).reshape(B_Q_LORA, B_HEADS, B_NOPE + B_ROPE)
    wq_p = jnp.concatenate(
        [wq[:, :, :B_NOPE],
         _pe_lanes(wq[:, :, B_NOPE:].reshape(B_Q_LORA * B_HEADS, B_ROPE)).reshape(B_Q_LORA, B_HEADS, LANES)],
        axis=2).reshape(B_Q_LORA, B_HEADS * B_QK).astype(BF16)
    wkv = w_ukv.reshape(B_KV_LORA, B_HEADS, B_NOPE + B_DV)
    wkv_p = jnp.concatenate([wkv[:, :, :B_NOPE].reshape(B_KV_LORA, B_HEADS * B_NOPE),
                             wkv[:, :, B_NOPE:].reshape(B_KV_LORA, B_HEADS * B_DV)], axis=1).astype(BF16)
    return w_tail, wq_p, wkv_p


def _layer_ab(h, rope, w_in_all, j, conv_w, a_log, dt_bias, out_norm, q_norm, w_uq, kv_norm, w_ukv, batch, seq):
    cos_b, sin_b, _, _ = rope
    w_tail, wq_p, wkv_p = _ab_weights(w_in_all[j], w_uq, w_ukv)
    proj = _mm(h, w_in_all, F32, layer=j, ncols=AB_MAIN)
    tail = _mm(h, w_tail, F32, tn=w_tail.shape[1])
    cq_block, ckv_block = 0, 1
    kpe_block = (B_Q_LORA + B_KV_LORA) // LANES
    gates = _gdn_gates(tail, kpe_block + 1, a_log, dt_bias, GDN_HG)
    g_row = gates[0][:, :A_HEADS].T.reshape(A_HEADS, 1, h.shape[0])
    o_a = _gdn(proj, conv_w, gates, g_row, out_norm, batch, seq)
    q_b = _mla_q(tail, cq_block, q_norm, wq_p, cos_b, sin_b)
    k_b, v_b = _mla_kv(tail, ckv_block, kv_norm, wkv_p, kpe_block, cos_b, sin_b)
    o_b = _attention(q_b, k_b, v_b, B_QK, B_DV, B_HEADS, batch, seq, CHUNK)
    return [o_a, o_b]


def _layer_cd(h, rope, w_in_all, j, group_norm, forget_bias, batch, seq):
    _, _, cos_c, sin_c = rope
    n = C_HEADS * C_DK
    ones = jnp.ones((n,), F32)
    scale_c = jnp.concatenate([ones, ones * (C_DK ** -0.5), ones, ones])
    scale_d = jnp.concatenate([ones * (D_DH ** -0.5 * LOG2E), ones, ones])
    proj_c = _mm(h, w_in_all, F32, layer=j, ncols=4 * n, colscale=scale_c)
    proj_d = _mm(h, w_in_all, BF16, layer=j, col0=4 * n, ncols=3 * n, colscale=scale_d)
    w_f = jnp.concatenate([w_in_all[j][:, 7 * n:7 * n + D_HEADS],
                           jnp.zeros((w_in_all.shape[1], LANES - D_HEADS), w_in_all.dtype)], axis=1)
    slab = _mm(h, w_f, F32, tn=LANES)
    o_c = _retention(proj_c, cos_c, sin_c, group_norm, batch, seq)
    ck = _fox_cum(slab, forget_bias, batch, seq, ATTN_HG)
    cq = ck[0][:, :D_HEADS].reshape(batch, seq, D_HEADS).transpose(0, 2, 1).reshape(batch, D_HEADS, 1, seq)
    o_d = _attention(proj_d, proj_d, proj_d, D_DH, D_DH, D_HEADS, batch, seq, 1,
                     q_off=0, k_off=D_HEADS * D_DH, v_off=2 * D_HEADS * D_DH, bias=(cq, ck))
    return [o_c, o_d]


def kernel(x, c, positions, ada_w, ada_b, mix_pre_norm, mix_post_norm, ffn_pre_norm, ffn_post_norm, ab_w_in, ab_conv_w, ab_a_log, ab_dt_bias, ab_out_norm, ab_q_norm, ab_w_uq, ab_kv_norm, ab_w_ukv, ab_w_out, cd_w_in, cd_group_norm, cd_forget_bias, cd_w_out, ffn_w_gate, ffn_w_up, ffn_w_down):
    batch, seq, d = x.shape
    depth = ada_w.shape[0]
    t = batch * seq
    rows = 8
    c_pad = jnp.concatenate([c, jnp.zeros((rows - batch, d), c.dtype)], axis=0)
    mod = _ada(c_pad, ada_w, ada_b).reshape(depth, rows, 6, d)
    rope = _rope_tables(positions.astype(F32).reshape(t, 1))
    x2 = x.reshape(t, d)
    h = _prenorm(x2, mod[0], mix_pre_norm[0], seq, 0, 1)
    w_down = ffn_w_down.astype(BF16)
    for layer in range(depth):
        j = layer // 2
        if layer % 2 == 0:
            y_in = _layer_ab(h, rope, ab_w_in, j, ab_conv_w[j], ab_a_log[j], ab_dt_bias[j], ab_out_norm[j],
                             ab_q_norm[j], ab_w_uq[j], ab_kv_norm[j], ab_w_ukv[j], batch, seq)
            w_out = ab_w_out
        else:
            y_in = _layer_cd(h, rope, cd_w_in, j, cd_group_norm[j], cd_forget_bias[j], batch, seq)
            w_out = cd_w_out
        x2, h = _mm_post(y_in, w_out, j, x2, mod[layer], mod[layer], mix_post_norm[layer],
                         ffn_pre_norm[layer], seq, 2, 3, 4, True, tm=512)
        hid = _ffn_up(h, ffn_w_gate, ffn_w_up, layer)
        last = layer == depth - 1
        nxt = layer if last else layer + 1
        x2, h = _mm_post([hid], w_down, layer, x2, mod[layer], mod[nxt],
                         ffn_post_norm[layer], mix_pre_norm[nxt], seq, 5, 0, 1, not last, tm=512,
                         vmem_limit_bytes=VMEM_LIMIT_BYTES_FFN_DOWN)
    return x2.reshape(batch, seq, d)
```
